```python
import jax
import jax.numpy as jnp
from jax import lax
import numpy as np

D_MODEL = 2048
BATCH = 8
SEQ = 2048
DEPTH = 4

HEAD_DIM = 128
ROPE_THETA = 10000.0
NORM_EPS = 1e-6
D_FF = 5632
HALF_STEP = 0.5
N_EVEN = (DEPTH + 1) // 2
N_ODD = DEPTH // 2

A_HEADS = 12
A_WIDTH = A_HEADS * HEAD_DIM
A_PATTERNS = ((128, 1), (512, 4), (2048, 16))
A_BLOCK = 128
B_WINDOWS = (2, 4, 8, 16)
B_GROUPS = 4
B_GROUP_DIM = 128
B_WIDTH = B_GROUPS * B_GROUP_DIM
EVEN_IN = 3 * A_WIDTH + B_WIDTH
EVEN_MIX = A_WIDTH + B_WIDTH

C_HEADS = 12
C_KV_HEADS = 2
C_WIDTH = C_HEADS * HEAD_DIM
C_KV_WIDTH = C_KV_HEADS * HEAD_DIM
C_BRANCHES = 3
CMP_BLOCK = 32
CMP_STRIDE = 16
CMP_HIDDEN = 256
SLC_BLOCK = 64
SLC_TOPN = 8
WIN_SIZE = 512
WIN_BLOCK = 128
SLC_QUERY_CHUNK = 64
FORCED_SCORE = 1e9
D_WIDTH = 512
CONV_WIDTH = 3
ODD_IN = C_WIDTH + 6 * C_KV_WIDTH + C_BRANCHES * C_HEADS + 3 * D_WIDTH
ODD_MIX = C_WIDTH + D_WIDTH

kernel_name = 'hybrid_dilated_pool_nsa_shortconv_macaron'


def rms_norm(x, w):
    xf = x.astype(jnp.float32)
    y = xf * lax.rsqrt(jnp.mean(xf * xf, axis=-1, keepdims=True) + NORM_EPS)
    return (y * w.astype(jnp.float32)).astype(x.dtype)


def swiglu(x, w_gate, w_up, w_down):
    return (jax.nn.silu(x @ w_gate) * (x @ w_up)) @ w_down


def to_heads(t, n_heads):
    b, s, _ = t.shape
    return t.reshape(b, s, n_heads, HEAD_DIM).transpose(0, 2, 1, 3)


def from_heads(t):
    b, h, s, d = t.shape
    return t.transpose(0, 2, 1, 3).reshape(b, s, h * d)


def rope_tables(positions):
    inv_freq = 1.0 / (ROPE_THETA ** (jnp.arange(0, HEAD_DIM, 2, dtype=jnp.float32) / HEAD_DIM))
    ang = positions.astype(jnp.float32)[..., None] * inv_freq
    return jnp.cos(ang), jnp.sin(ang)


def apply_rope(t, cos, sin):
    t1, t2 = jnp.split(t.astype(jnp.float32), 2, axis=-1)
    c, s = cos[:, None], sin[:, None]
    return jnp.concatenate([t1 * c - t2 * s, t1 * s + t2 * c], axis=-1).astype(t.dtype)


def banded_causal_attention(q, k, v, n_back, blk):
    L, hd = q.shape[-2], q.shape[-1]
    nb = -(-L // blk)
    lp = nb * blk
    n_prev = -(-n_back // blk)
    qp = jnp.pad(q, [(0, 0)] * (q.ndim - 2) + [(0, lp - L), (0, 0)])
    kv_pad = [(0, 0)] * (k.ndim - 2) + [(n_prev * blk, lp - L), (0, 0)]
    kp = jnp.pad(k, kv_pad).reshape(k.shape[:-2] + (nb + n_prev, blk, hd))
    vp = jnp.pad(v, kv_pad).reshape(v.shape[:-2] + (nb + n_prev, blk, hd))
    qb = qp.reshape(q.shape[:-2] + (nb, blk, hd))
    kb = jnp.concatenate([kp[..., j:j + nb, :, :] for j in range(n_prev + 1)], axis=-2)
    vb = jnp.concatenate([vp[..., j:j + nb, :, :] for j in range(n_prev + 1)], axis=-2)
    span = (n_prev + 1) * blk
    qi = jnp.arange(blk)[:, None]
    kj = jnp.arange(span)[None, :]
    dist = qi - kj + n_prev * blk
    key_pos = (jnp.arange(nb)[:, None, None] - n_prev) * blk + kj[None]
    mask = (dist >= 0) & (dist <= n_back) & (key_pos >= 0)
    s = jnp.einsum('...hnqd,...nkd->...hnqk', qb, kb, preferred_element_type=jnp.float32) * (hd ** -0.5)
    s = jnp.where(mask, s, -jnp.inf)
    m = jnp.max(s, axis=-1, keepdims=True)
    p = jnp.exp(s - m)
    den = jnp.sum(p, axis=-1, keepdims=True)
    o = jnp.einsum('...hnqk,...nkd->...hnqd', p, vb.astype(jnp.float32)) / den
    lse = (m + jnp.log(den))[..., 0]
    o = o.reshape(o.shape[:-3] + (lp, hd))[..., :L, :]
    lse = lse.reshape(lse.shape[:-2] + (lp,))[..., :L]
    return o, lse


def dilated_attention(q, k, v):
    b, h, s, hd = q.shape
    outs, lses = [], []
    for window, dil in A_PATTERNS:
        L = s // dil
        def strided(t):
            return t.reshape(b, h, L, dil, hd).transpose(0, 1, 3, 2, 4)
        o, lse = banded_causal_attention(strided(q)[..., None, :, :], strided(k), strided(v), window // dil, A_BLOCK)
        outs.append(o[:, :, :, 0].transpose(0, 1, 3, 2, 4).reshape(b, h, s, hd))
        lses.append(lse[:, :, :, 0].transpose(0, 1, 3, 2).reshape(b, h, s))
    weights = jax.nn.softmax(jnp.stack(lses, axis=0), axis=0)
    return jnp.sum(weights[..., None] * jnp.stack(outs, axis=0), axis=0)


def multiscale_pool(u, pool_w, pool_scale):
    b, s, _ = u.shape
    uf = u.astype(jnp.float32).reshape(b, s, B_GROUPS, B_GROUP_DIM)
    cs = jnp.cumsum(uf, axis=1)
    pos1 = jnp.arange(1, s + 1, dtype=jnp.float32)
    groups = []
    for g, w in enumerate(B_WINDOWS):
        c = cs[:, :, g]
        prev = jnp.pad(c, ((0, 0), (w, 0), (0, 0)))[:, :s]
        cnt = jnp.minimum(pos1, float(w))
        groups.append((c - prev) / cnt[None, :, None] - uf[:, :, g])
    pooled = jnp.stack(groups, axis=2)
    mixed = jnp.einsum('bsgc,gcd->bsgd', pooled, pool_w.astype(jnp.float32))
    return mixed.reshape(b, s, B_WIDTH) * pool_scale.astype(jnp.float32)


def even_mixer(hn, cos, sin, w_in, w_out, pool_w, pool_scale):
    proj = hn @ w_in
    q, k, v, u = jnp.split(proj, [A_WIDTH, 2 * A_WIDTH, 3 * A_WIDTH], axis=-1)
    q = apply_rope(to_heads(q, A_HEADS), cos, sin)
    k = apply_rope(to_heads(k, A_HEADS), cos, sin)
    o_a = dilated_attention(q, k, to_heads(v, A_HEADS))
    o_b = multiscale_pool(u, pool_w, pool_scale)
    mixed = jnp.concatenate([from_heads(o_a), o_b], axis=-1).astype(hn.dtype)
    return mixed @ w_out


def compress_blocks(t, pe, w1, w2):
    b, g, s, hd = t.shape
    r = t.reshape(b, g, s // CMP_STRIDE, CMP_STRIDE, hd)
    n_sub = CMP_BLOCK // CMP_STRIDE
    n_blk = s // CMP_STRIDE - n_sub + 1
    blocks = jnp.concatenate([r[:, :, j:j + n_blk] for j in range(n_sub)], axis=3) + pe
    flat = blocks.reshape(b, g, n_blk, CMP_BLOCK * hd)
    return jax.nn.gelu(flat @ w1) @ w2


def nsa_attention(q, kc, vc, ks, vs, kw, vw, gate_logits, cos, sin, pe_k, w1_k, w2_k, pe_v, w1_v, w2_v):
    b, h, s, hd = q.shape
    g = C_KV_HEADS
    hg = h // g
    scale = hd ** -0.5
    t = jnp.arange(s)

    k_cmp = compress_blocks(kc, pe_k, w1_k, w2_k)
    v_cmp = compress_blocks(vc, pe_v, w1_v, w2_v)
    n_cmp = k_cmp.shape[2]
    qg = q.reshape(b, g, hg, s, hd)
    sc = jnp.einsum('bghsd,bgnd->bghsn', qg, k_cmp, preferred_element_type=jnp.float32) * scale
    cmp_start = jnp.arange(n_cmp) * CMP_STRIDE
    cmask = (cmp_start[None, :] + CMP_BLOCK - 1) <= t[:, None]
    sc = jnp.where(cmask, sc, -jnp.inf)
    m = jnp.max(sc, axis=-1, keepdims=True)
    m = jnp.where(jnp.isfinite(m), m, 0.0)
    p_cmp = jnp.exp(sc - m)
    den = jnp.sum(p_cmp, axis=-1, keepdims=True)
    p_cmp = p_cmp / jnp.maximum(den, 1.0)
    o_cmp = jnp.einsum('bghsn,bgnd->bghsd', p_cmp, v_cmp.astype(jnp.float32))

    n_slc = s // SLC_BLOCK
    slc_start = jnp.arange(n_slc) * SLC_BLOCK
    cover = ((cmp_start[:, None] < slc_start[None, :] + SLC_BLOCK)
             & (cmp_start[:, None] + CMP_BLOCK > slc_start[None, :])).astype(jnp.float32)
    imp = jnp.einsum('bghsn,nj->bgsj', p_cmp, cover)
    cur = t // SLC_BLOCK
    jj = jnp.arange(n_slc)
    visible = jj[None, :] <= cur[:, None]
    forced = visible & ((jj[None, :] == 0) | (jj[None, :] >= cur[:, None] - 1))
    score = jnp.where(forced, FORCED_SCORE, jnp.where(visible, imp, -FORCED_SCORE))
    top_n = min(SLC_TOPN, n_slc)
    _, sel = lax.top_k(score, top_n)

    q_rot = apply_rope(q, cos, sin).reshape(b, g, hg, s, hd)
    ks_blocks = apply_rope(ks, cos, sin).reshape(b, g, n_slc, SLC_BLOCK, hd)
    vs_blocks = vs.reshape(b, g, n_slc, SLC_BLOCK, hd)
    gather = jax.vmap(jax.vmap(lambda blocks, idx: blocks[idx]))

    def sel_chunk(args):
        qc, ic, tc = args
        kg = gather(ks_blocks, ic)
        vg = gather(vs_blocks, ic)
        s_ = jnp.einsum('bghcd,bgcnld->bghcnl', qc, kg, preferred_element_type=jnp.float32) * scale
        key_pos = ic[..., None] * SLC_BLOCK + jnp.arange(SLC_BLOCK)
        smask = key_pos <= tc[None, None, :, None, None]
        s_ = jnp.where(smask[:, :, None], s_, -jnp.inf)
        pr = jax.nn.softmax(s_.reshape(s_.shape[:4] + (-1,)), axis=-1).reshape(s_.shape)
        return jnp.einsum('bghcnl,bgcnld->bghcd', pr, vg.astype(jnp.float32))

    n_chunk = s // SLC_QUERY_CHUNK
    q_chunks = jnp.moveaxis(q_rot.reshape(b, g, hg, n_chunk, SLC_QUERY_CHUNK, hd), 3, 0)
    i_chunks = jnp.moveaxis(sel.reshape(b, g, n_chunk, SLC_QUERY_CHUNK, top_n), 2, 0)
    t_chunks = t.reshape(n_chunk, SLC_QUERY_CHUNK)
    o_slc = lax.map(sel_chunk, (q_chunks, i_chunks, t_chunks))
    o_slc = jnp.moveaxis(o_slc, 0, 3).reshape(b, g, hg, s, hd)

    o_win, _ = banded_causal_attention(q_rot, apply_rope(kw, cos, sin), vw, WIN_SIZE - 1, WIN_BLOCK)

    gates = jax.nn.sigmoid(gate_logits.astype(jnp.float32)).reshape(b, s, h, C_BRANCHES)
    gates = gates.transpose(0, 2, 1, 3).reshape(b, g, hg, s, C_BRANCHES)
    o = gates[..., 0:1] * o_cmp + gates[..., 1:2] * o_slc + gates[..., 2:3] * o_win
    return o.reshape(b, h, s, hd)


def short_conv_mixer(u_in, c_gate, b_gate, conv_w):
    s = u_in.shape[1]
    u = c_gate * u_in
    up = jnp.pad(u, ((0, 0), (CONV_WIDTH - 1, 0), (0, 0)))
    conv = up[:, 0:s] * conv_w[0]
    for j in range(1, CONV_WIDTH):
        conv = conv + up[:, j:j + s] * conv_w[j]
    return b_gate * conv


def odd_mixer(hn, cos, sin, w_in, w_out, pe_k, w1_k, w2_k, pe_v, w1_v, w2_v, conv_w):
    proj = hn @ w_in
    cuts = np.cumsum([C_WIDTH] + [C_KV_WIDTH] * 6 + [C_BRANCHES * C_HEADS, D_WIDTH, D_WIDTH]).tolist()
    q, kc, vc, ks, vs, kw, vw, gl, u_in, c_gate, b_gate = jnp.split(proj, cuts, axis=-1)
    kvh = lambda t: to_heads(t, C_KV_HEADS)
    o_c = nsa_attention(to_heads(q, C_HEADS), kvh(kc), kvh(vc), kvh(ks), kvh(vs), kvh(kw), kvh(vw), gl,
                        cos, sin, pe_k, w1_k, w2_k, pe_v, w1_v, w2_v)
    y_d = short_conv_mixer(u_in, c_gate, b_gate, conv_w)
    mixed = jnp.concatenate([from_heads(o_c).astype(hn.dtype), y_d.astype(hn.dtype)], axis=-1)
    return mixed @ w_out


def setup_inputs(seed: int = 0) -> dict:
    key = jax.random.key(seed)
    k = jax.random.split(key, 19)

    def normal(kk, shape, fan_in):
        return jax.random.normal(kk, shape, jnp.float32) * (float(fan_in) ** -0.5)

    x = jax.random.normal(k[0], (BATCH, SEQ, D_MODEL), jnp.float32)
    offsets = jax.random.randint(k[1], (BATCH, 1), 0, 1024, dtype=jnp.int32)
    positions = offsets + jnp.arange(SEQ, dtype=jnp.int32)[None, :]
    norm_w = 1.0 + 0.02 * jax.random.normal(k[2], (DEPTH, 6, D_MODEL), jnp.float32)
    ffn_w_gate = normal(k[3], (DEPTH, 2, D_MODEL, D_FF), D_MODEL)
    ffn_w_up = normal(k[4], (DEPTH, 2, D_MODEL, D_FF), D_MODEL)
    ffn_w_down = normal(k[5], (DEPTH, 2, D_FF, D_MODEL), D_FF)
    ev_w_in = normal(k[6], (N_EVEN, D_MODEL, EVEN_IN), D_MODEL)
    ev_w_out = normal(k[7], (N_EVEN, EVEN_MIX, D_MODEL), EVEN_MIX)
    pool_w = normal(k[8], (N_EVEN, B_GROUPS, B_GROUP_DIM, B_GROUP_DIM), B_GROUP_DIM)
    pool_scale = 1.0 + 0.1 * jax.random.normal(k[9], (N_EVEN, B_WIDTH), jnp.float32)
    od_w_in = normal(k[10], (N_ODD, D_MODEL, ODD_IN), D_MODEL)
    od_w_out = normal(k[11], (N_ODD, ODD_MIX, D_MODEL), ODD_MIX)
    cmp_pe_k = 0.1 * jax.random.normal(k[12], (N_ODD, CMP_BLOCK, HEAD_DIM), jnp.float32)
    cmp_w1_k = normal(k[13], (N_ODD, CMP_BLOCK * HEAD_DIM, CMP_HIDDEN), CMP_BLOCK * HEAD_DIM)
    cmp_w2_k = normal(k[14], (N_ODD, CMP_HIDDEN, HEAD_DIM), CMP_HIDDEN)
    cmp_pe_v = 0.1 * jax.random.normal(k[15], (N_ODD, CMP_BLOCK, HEAD_DIM), jnp.float32)
    cmp_w1_v = normal(k[16], (N_ODD, CMP_BLOCK * HEAD_DIM, CMP_HIDDEN), CMP_BLOCK * HEAD_DIM)
    cmp_w2_v = normal(k[17], (N_ODD, CMP_HIDDEN, HEAD_DIM), CMP_HIDDEN)
    conv_w = normal(k[18], (N_ODD, CONV_WIDTH, D_WIDTH), CONV_WIDTH)
    return {'x': x, 'positions': positions, 'norm_w': norm_w,
            'ffn_w_gate': ffn_w_gate, 'ffn_w_up': ffn_w_up, 'ffn_w_down': ffn_w_down,
            'ev_w_in': ev_w_in, 'ev_w_out': ev_w_out, 'pool_w': pool_w, 'pool_scale': pool_scale,
            'od_w_in': od_w_in, 'od_w_out': od_w_out,
            'cmp_pe_k': cmp_pe_k, 'cmp_w1_k': cmp_w1_k, 'cmp_w2_k': cmp_w2_k,
            'cmp_pe_v': cmp_pe_v, 'cmp_w1_v': cmp_w1_v, 'cmp_w2_v': cmp_w2_v,
            'conv_w': conv_w}


def reference(x, positions, norm_w, ffn_w_gate, ffn_w_up, ffn_w_down, ev_w_in, ev_w_out, pool_w, pool_scale,
              od_w_in, od_w_out, cmp_pe_k, cmp_w1_k, cmp_w2_k, cmp_pe_v, cmp_w1_v, cmp_w2_v, conv_w):
    cos, sin = rope_tables(positions)
    h = x
    for layer in range(DEPTH):
        nw = norm_w[layer]
        f = swiglu(rms_norm(h, nw[0]), ffn_w_gate[layer, 0], ffn_w_up[layer, 0], ffn_w_down[layer, 0])
        h = h + HALF_STEP * rms_norm(f, nw[1])
        hn = rms_norm(h, nw[2])
        i = layer // 2
        if layer % 2 == 0:
            m = even_mixer(hn, cos, sin, ev_w_in[i], ev_w_out[i], pool_w[i], pool_scale[i])
        else:
            m = odd_mixer(hn, cos, sin, od_w_in[i], od_w_out[i], cmp_pe_k[i], cmp_w1_k[i], cmp_w2_k[i],
                          cmp_pe_v[i], cmp_w1_v[i], cmp_w2_v[i], conv_w[i])
        h = h + rms_norm(m, nw[3])
        f = swiglu(rms_norm(h, nw[4]), ffn_w_gate[layer, 1], ffn_w_up[layer, 1], ffn_w_down[layer, 1])
        h = h + HALF_STEP * rms_norm(f, nw[5])
    return h
```

```python
import functools

import jax
import jax.numpy as jnp
from jax import lax
from jax.experimental import pallas as pl
from jax.experimental.pallas import tpu as pltpu

HEAD_DIM = 128
ROPE_THETA = 10000.0
NORM_EPS = 1e-6
HALF_STEP = 0.5
LANE = 128

A_HEADS = 12
A_DILATIONS = (1, 4, 16)
A_BACK = 128
B_WINDOWS = (2, 4, 8, 16)
B_GROUPS = 4

C_HEADS = 12
C_KV_HEADS = 2
C_GROUP = C_HEADS // C_KV_HEADS
CMP_BLOCK = 32
CMP_STRIDE = 16
SLC_BLOCK = 64
SLC_TOPN = 8
WIN_SIZE = 512
FORCED_SCORE = 1e9
D_WIDTH = 512
PROJ_SLABS = 40

MASKED = -1e30
VMEM_LIMIT = 56 * 1024 * 1024

_BF16 = jnp.bfloat16
_F32 = jnp.float32


def _params(*sem):
    return pltpu.CompilerParams(dimension_semantics=sem, vmem_limit_bytes=VMEM_LIMIT)


def _rms(x, w):
    return x * lax.rsqrt(jnp.mean(x * x, axis=-1, keepdims=True) + NORM_EPS) * w


def _dot(a, b):
    return jnp.dot(a, b, preferred_element_type=_F32)


def _dot_t(a, b):
    return lax.dot_general(a, b, (((1,), (1,)), ((), ())), preferred_element_type=_F32)


def _rope(x, cos, sin):
    return x * cos + pltpu.roll(x, HEAD_DIM // 2, 1) * sin


def _shift_rows(x, k, row):
    return jnp.where(row >= k, pltpu.roll(x, k, 0), 0.0)


def _proj_kernel(h_ref, nw_ref, w_ref, o_ref, xn_ref):
    @pl.when(pl.program_id(1) == 0)
    def _():
        xn_ref[...] = _rms(h_ref[...], nw_ref[...]).astype(_BF16)

    acc = _dot(xn_ref[...], w_ref[...])
    for s in range(o_ref.shape[1]):
        o_ref[0, s] = acc[:, s * LANE:(s + 1) * LANE]


def _norm_proj(h, nw, w, batch, seq, tm=1024, tn=512):
    t, d = h.shape
    n = w.shape[1]
    per_b = seq // tm
    return pl.pallas_call(
        _proj_kernel,
        grid=(t // tm, n // tn),
        in_specs=[pl.BlockSpec((tm, d), lambda i, j: (i, 0)),
                  pl.BlockSpec((1, d), lambda i, j: (0, 0)),
                  pl.BlockSpec((d, tn), lambda i, j: (0, j))],
        out_specs=pl.BlockSpec((1, tn // LANE, tm, LANE), lambda i, j: (i // per_b, j, i % per_b, 0)),
        out_shape=jax.ShapeDtypeStruct((batch, n // LANE, seq, LANE), _F32),
        scratch_shapes=[pltpu.VMEM((tm, d), _BF16)],
        compiler_params=_params("arbitrary", "arbitrary"),
        name="norm_proj",
    )(h, nw, w)


def _ffn_up_kernel(h_ref, nw_ref, wg_ref, wu_ref, o_ref, xn_ref):
    @pl.when(pl.program_id(1) == 0)
    def _():
        xn_ref[...] = _rms(h_ref[...], nw_ref[...]).astype(_BF16)

    xn = xn_ref[...]
    g = _dot(xn, wg_ref[...])
    u = _dot(xn, wu_ref[...])
    o_ref[...] = (jax.nn.silu(g) * u).astype(_BF16)


def _ffn_up(h, nw, wg, wu, tm=1024, tn=512):
    t, d = h.shape
    f = wg.shape[1]
    return pl.pallas_call(
        _ffn_up_kernel,
        grid=(t // tm, f // tn),
        in_specs=[pl.BlockSpec((tm, d), lambda i, j: (i, 0)),
                  pl.BlockSpec((1, d), lambda i, j: (0, 0)),
                  pl.BlockSpec((d, tn), lambda i, j: (0, j)),
                  pl.BlockSpec((d, tn), lambda i, j: (0, j))],
        out_specs=pl.BlockSpec((tm, tn), lambda i, j: (i, j)),
        out_shape=jax.ShapeDtypeStruct((t, f), _BF16),
        scratch_shapes=[pltpu.VMEM((tm, d), _BF16)],
        compiler_params=_params("arbitrary", "arbitrary"),
        name="ffn_up",
    )(h, nw, wg, wu)


def _ffn_down_kernel(a_ref, w_ref, h_ref, nw_ref, o_ref, acc_ref, *, scale):
    k = pl.program_id(1)

    @pl.when(k == 0)
    def _():
        acc_ref[...] = jnp.zeros_like(acc_ref)

    acc_ref[...] += _dot(a_ref[...], w_ref[...])

    @pl.when(k == pl.num_programs(1) - 1)
    def _():
        o_ref[...] = h_ref[...] + scale * _rms(acc_ref[...], nw_ref[...])


def _ffn_down(a, w, h, nw, scale, tm=512, tk=512):
    t, f = a.shape
    d = w.shape[1]
    return pl.pallas_call(
        functools.partial(_ffn_down_kernel, scale=scale),
        grid=(t // tm, f // tk),
        in_specs=[pl.BlockSpec((tm, tk), lambda i, k: (i, k)),
                  pl.BlockSpec((tk, d), lambda i, k: (k, 0)),
                  pl.BlockSpec((tm, d), lambda i, k: (i, 0)),
                  pl.BlockSpec((1, d), lambda i, k: (0, 0))],
        out_specs=pl.BlockSpec((tm, d), lambda i, k: (i, 0)),
        out_shape=jax.ShapeDtypeStruct((t, d), _F32),
        scratch_shapes=[pltpu.VMEM((tm, d), _F32)],
        compiler_params=_params("arbitrary", "arbitrary"),
        name="ffn_down",
    )(a, w, h, nw)


def _mix_out_kernel(x1_ref, x2_ref, w_ref, h_ref, nw_ref, o_ref):
    c1 = x1_ref.shape[1]
    m = _dot(x1_ref[...], w_ref[0:c1, :]) + _dot(x2_ref[...], w_ref[c1:, :])
    o_ref[...] = h_ref[...] + _rms(m, nw_ref[...])


def _mix_out(x1, x2, w, h, nw, tm=512):
    t, c1 = x1.shape
    c2 = x2.shape[1]
    d = w.shape[1]
    return pl.pallas_call(
        _mix_out_kernel,
        grid=(t // tm,),
        in_specs=[pl.BlockSpec((tm, c1), lambda i: (i, 0)),
                  pl.BlockSpec((tm, c2), lambda i: (i, 0)),
                  pl.BlockSpec((c1 + c2, d), lambda i: (0, 0)),
                  pl.BlockSpec((tm, d), lambda i: (i, 0)),
                  pl.BlockSpec((1, d), lambda i: (0, 0))],
        out_specs=pl.BlockSpec((tm, d), lambda i: (i, 0)),
        out_shape=jax.ShapeDtypeStruct((t, d), _F32),
        compiler_params=_params("arbitrary"),
        name="mix_out",
    )(x1, x2, w, h, nw)


def _rows(start, size, stride):
    return pl.ds(start, size) if stride == 1 else pl.ds(start, size, stride=stride)


def _dilated_kernel(q_ref, k_ref, v_ref, cos_ref, sin_ref, o_ref, qs, ks, pv_s, m_s, l_s):
    seq = q_ref.shape[2]
    blk = A_BACK
    scale = HEAD_DIM ** -0.5
    chunk = 256

    def rope_chunk(c, _):
        r = pl.ds(pl.multiple_of(c * chunk, chunk), chunk)
        cos, sin = cos_ref[0, r, :], sin_ref[0, r, :]
        qs[r, :] = _rope(q_ref[0, 0, r, :], cos, sin)
        ks[r, :] = _rope(k_ref[0, 0, r, :], cos, sin)
        return 0

    lax.fori_loop(0, seq // chunk, rope_chunk, 0)

    qi = lax.broadcasted_iota(jnp.int32, (blk, blk), 0)
    kj = lax.broadcasted_iota(jnp.int32, (blk, blk), 1)
    cur_ok = kj <= qi

    for g, dil in enumerate(A_DILATIONS):
        nblk = seq // (blk * dil)

        def block(idx, _, g=g, dil=dil, nblk=nblk):
            r = idx // nblk
            n = idx % nblk
            start = r + n * (blk * dil)
            rows = _rows(start, blk, dil)
            qb = qs[rows, :].astype(_BF16)
            s_cur = jnp.where(cur_ok, _dot_t(qb, ks[rows, :].astype(_BF16)) * scale, -jnp.inf)
            m = jnp.max(s_cur, axis=-1, keepdims=True)
            if nblk > 1:
                prev_rows = _rows(r + jnp.maximum(n - 1, 0) * (blk * dil), blk, dil)
                s_prev = _dot_t(qb, ks[prev_rows, :].astype(_BF16)) * scale
                s_prev = jnp.where(kj >= qi + jnp.where(n > 0, 0, blk), s_prev, -jnp.inf)
                m = jnp.maximum(m, jnp.max(s_prev, axis=-1, keepdims=True))
            p_cur = jnp.exp(s_cur - m)
            den = jnp.sum(p_cur, axis=-1, keepdims=True)
            pv = _dot(p_cur.astype(_BF16), v_ref[0, 0, rows, :].astype(_BF16))
            if nblk > 1:
                p_prev = jnp.exp(s_prev - m)
                den = den + jnp.sum(p_prev, axis=-1, keepdims=True)
                pv = pv + _dot(p_prev.astype(_BF16), v_ref[0, 0, prev_rows, :].astype(_BF16))
            pv_s[g, rows, :] = pv
            m_s[g, rows, :] = jnp.broadcast_to(m, (blk, HEAD_DIM))
            l_s[g, rows, :] = jnp.broadcast_to(den, (blk, HEAD_DIM))
            return 0

        lax.fori_loop(0, seq // blk, block, 0)

    def merge_chunk(c, _):
        r = pl.ds(pl.multiple_of(c * chunk, chunk), chunk)
        m_all = jnp.maximum(jnp.maximum(m_s[0, r, :], m_s[1, r, :]), m_s[2, r, :])
        num = jnp.zeros((chunk, HEAD_DIM), _F32)
        den = jnp.zeros((chunk, HEAD_DIM), _F32)
        for g in range(len(A_DILATIONS)):
            w = jnp.exp(m_s[g, r, :] - m_all)
            num = num + w * pv_s[g, r, :]
            den = den + w * l_s[g, r, :]
        o_ref[0, r, :] = (num / den).astype(o_ref.dtype)
        return 0

    lax.fori_loop(0, seq // chunk, merge_chunk, 0)


def _dilated_attention(slabs, cos, sin):
    batch, _, seq, _ = slabs.shape
    head = lambda off: pl.BlockSpec((1, 1, seq, LANE), lambda b, h: (b, off + h, 0, 0))
    table = pl.BlockSpec((1, seq, LANE), lambda b, h: (b, 0, 0))
    return pl.pallas_call(
        _dilated_kernel,
        grid=(batch, A_HEADS),
        in_specs=[head(0), head(A_HEADS), head(2 * A_HEADS), table, table],
        out_specs=pl.BlockSpec((1, seq, LANE), lambda b, h: (b, 0, h)),
        out_shape=jax.ShapeDtypeStruct((batch, seq, A_HEADS * HEAD_DIM), _BF16),
        scratch_shapes=[pltpu.VMEM((seq, HEAD_DIM), _F32), pltpu.VMEM((seq, HEAD_DIM), _F32),
                        pltpu.VMEM((3, seq, HEAD_DIM), _F32), pltpu.VMEM((3, seq, HEAD_DIM), _F32),
                        pltpu.VMEM((3, seq, HEAD_DIM), _F32)],
        compiler_params=_params("arbitrary", "arbitrary"),
        name="dilated_attention",
    )(slabs, slabs, slabs, cos, sin)


def _pool_kernel(u_ref, w_ref, sc_ref, o_ref):
    seq = u_ref.shape[2]
    row = lax.broadcasted_iota(jnp.int32, (seq, LANE), 0)
    for g, win in enumerate(B_WINDOWS):
        x = u_ref[0, g]
        s = x
        step = 1
        while step < win:
            s = s + _shift_rows(s, step, row)
            step *= 2
        cnt = jnp.minimum(row + 1, win).astype(_F32)
        pooled = s / cnt - x
        mixed = _dot(pooled.astype(_BF16), w_ref[g]) * sc_ref[:, g * LANE:(g + 1) * LANE]
        o_ref[0, :, g * LANE:(g + 1) * LANE] = mixed.astype(o_ref.dtype)


def _multiscale_pool(slabs, pool_w, pool_scale):
    batch, _, seq, _ = slabs.shape
    first = 3 * A_HEADS // B_GROUPS
    return pl.pallas_call(
        _pool_kernel,
        grid=(batch,),
        in_specs=[pl.BlockSpec((1, B_GROUPS, seq, LANE), lambda b: (b, first, 0, 0)),
                  pl.BlockSpec((B_GROUPS, LANE, LANE), lambda b: (0, 0, 0)),
                  pl.BlockSpec((1, B_GROUPS * LANE), lambda b: (0, 0))],
        out_specs=pl.BlockSpec((1, seq, B_GROUPS * LANE), lambda b: (b, 0, 0)),
        out_shape=jax.ShapeDtypeStruct((batch, seq, B_GROUPS * LANE), _BF16),
        compiler_params=_params("arbitrary"),
        name="multiscale_pool",
    )(slabs, pool_w, pool_scale)


def _compress_kernel(r_ref, pe_ref, w1_ref, w2_ref, o_ref):
    half = r_ref.shape[3]
    r = r_ref[0, 0]
    a = _dot((r + pe_ref[0, 0]).astype(_BF16), w1_ref[0, 0:half, :])
    b = _dot((r + pe_ref[0, 1]).astype(_BF16), w1_ref[0, half:, :])
    hid = a + pltpu.roll(b, b.shape[0] - 1, 0)
    o_ref[0, 0, 0] = _dot(jax.nn.gelu(hid).astype(_BF16), w2_ref[0])


def _compress(slabs, pe, w1, w2, first_slab):
    batch, _, nrow, width = slabs.shape
    return pl.pallas_call(
        _compress_kernel,
        grid=(2, batch, C_KV_HEADS),
        in_specs=[pl.BlockSpec((1, 1, nrow, width), lambda kv, b, g: (b, first_slab + 2 * kv + g, 0, 0)),
                  pl.BlockSpec((1, 2, 1, width), lambda kv, b, g: (kv, 0, 0, 0)),
                  pl.BlockSpec((1, 2 * width, w1.shape[2]), lambda kv, b, g: (kv, 0, 0)),
                  pl.BlockSpec((1, w2.shape[1], HEAD_DIM), lambda kv, b, g: (kv, 0, 0))],
        out_specs=pl.BlockSpec((1, 1, 1, nrow, HEAD_DIM), lambda kv, b, g: (kv, b, g, 0, 0)),
        out_shape=jax.ShapeDtypeStruct((2, batch, C_KV_HEADS, nrow, HEAD_DIM), _F32),
        compiler_params=_params("arbitrary", "arbitrary", "arbitrary"),
        name="compress",
    )(slabs, pe, w1, w2)


def _cmp_select_kernel(q_ref, kc_ref, vc_ref, o_ref, sel_ref, *, nslc):
    tq = q_ref.shape[2]
    ncmp = kc_ref.shape[3]
    scale = HEAD_DIM ** -0.5
    t = pl.program_id(2) * tq + lax.broadcasted_iota(jnp.int32, (tq, ncmp), 0)
    n = lax.broadcasted_iota(jnp.int32, (tq, ncmp), 1)
    cmask = n * CMP_STRIDE + (CMP_BLOCK - 1) <= t

    cn = lax.broadcasted_iota(jnp.int32, (ncmp, LANE), 0) * CMP_STRIDE
    cj = lax.broadcasted_iota(jnp.int32, (ncmp, LANE), 1) * SLC_BLOCK
    cover = ((cn < cj + SLC_BLOCK) & (cn + CMP_BLOCK > cj)).astype(_BF16)

    kc = kc_ref[0, 0, 0].astype(_BF16)
    vc = vc_ref[0, 0, 0].astype(_BF16)
    imp = jnp.zeros((tq, LANE), _F32)
    for hh in range(C_GROUP):
        sc = jnp.where(cmask, _dot_t(q_ref[0, hh].astype(_BF16), kc) * scale, -jnp.inf)
        m = jnp.max(sc, axis=-1, keepdims=True)
        m = jnp.where(jnp.isfinite(m), m, 0.0)
        p = jnp.exp(sc - m)
        den = jnp.sum(p, axis=-1, keepdims=True)
        p = (p / jnp.maximum(den, 1.0)).astype(_BF16)
        o_ref[0, hh] = _dot(p, vc)
        imp = imp + _dot(p, cover)

    j = n
    cur = lax.shift_right_logical(t, SLC_BLOCK.bit_length() - 1)
    visible = j <= cur
    forced = visible & ((j == 0) | (j >= cur - 1))
    score = jnp.where(forced, FORCED_SCORE, jnp.where(visible, imp, -FORCED_SCORE))
    score = jnp.where(j < nslc, score, -jnp.inf)
    jf = j.astype(_F32)
    sel = jnp.zeros((tq, LANE), _F32)
    for _ in range(SLC_TOPN):
        best = jnp.max(score, axis=-1, keepdims=True)
        pick = jnp.min(jnp.where(score == best, jf, float(LANE)), axis=-1, keepdims=True)
        hit = jf == pick
        sel = jnp.where(hit, 1.0, sel)
        score = jnp.where(hit, -jnp.inf, score)
    sel_ref[0, 0] = sel.astype(sel_ref.dtype)


def _cmp_select(slabs, kv_cmp, tq=256):
    batch, _, seq, _ = slabs.shape
    cmp_spec = lambda which: pl.BlockSpec((1, 1, 1) + kv_cmp.shape[3:], lambda b, g, i: (which, b, g, 0, 0))
    return pl.pallas_call(
        functools.partial(_cmp_select_kernel, nslc=seq // SLC_BLOCK),
        grid=(batch, C_KV_HEADS, seq // tq),
        in_specs=[pl.BlockSpec((1, C_GROUP, tq, LANE), lambda b, g, i: (b, g, i, 0)),
                  cmp_spec(0), cmp_spec(1)],
        out_specs=[pl.BlockSpec((1, C_GROUP, tq, LANE), lambda b, g, i: (b, g, i, 0)),
                   pl.BlockSpec((1, 1, tq, LANE), lambda b, g, i: (b, g, i, 0))],
        out_shape=[jax.ShapeDtypeStruct((batch, C_HEADS, seq, HEAD_DIM), _F32),
                   jax.ShapeDtypeStruct((batch, C_KV_HEADS, seq, LANE), _BF16)],
        compiler_params=_params("arbitrary", "arbitrary", "arbitrary"),
        name="cmp_select",
    )(slabs, kv_cmp, kv_cmp)


def _stream_softmax(q, k_ref, v_ref, lo, hi, tk, mask_fn, m_s, l_s, acc_s):
    rows = q.shape[0]
    tq = rows // C_GROUP
    scale = HEAD_DIM ** -0.5
    m_s[...] = jnp.full(m_s.shape, MASKED, _F32)
    l_s[...] = jnp.zeros(l_s.shape, _F32)
    acc_s[...] = jnp.zeros(acc_s.shape, _F32)

    def tile(kt, _):
        k0 = pl.multiple_of(kt * tk, tk)
        s = _dot_t(q, k_ref[pl.ds(k0, tk), :]) * scale
        ok = mask_fn(k0)
        s = jnp.where(ok[None], s.reshape(C_GROUP, tq, tk), MASKED).reshape(rows, tk)
        m_prev = m_s[...]
        m_new = jnp.maximum(m_prev, jnp.max(s, axis=-1, keepdims=True))
        alpha = jnp.exp(m_prev - m_new)
        p = jnp.exp(s - m_new[:, 0:1])
        l_s[...] = alpha * l_s[...] + jnp.sum(p, axis=-1, keepdims=True)
        acc_s[...] = alpha * acc_s[...] + _dot(p.astype(_BF16), v_ref[pl.ds(k0, tk), :])
        m_s[...] = m_new
        return 0

    lax.fori_loop(lo, hi, tile, 0)
    return acc_s[...] / l_s[...]


def _nsa_kernel(q_ref, ks_ref, vs_ref, kw_ref, vw_ref, sel_ref, oc_ref, gl_ref, cos_ref, sin_ref, o_ref,
                ksr, vsb, kwr, vwb, qr, m_s, l_s, acc_s):
    tq = q_ref.shape[2]
    seq = ks_ref.shape[2]
    qi = pl.program_id(2)
    chunk = 256

    @pl.when(qi == 0)
    def _():
        def prep(c, _):
            r = pl.ds(pl.multiple_of(c * chunk, chunk), chunk)
            cos, sin = cos_ref[0, r, :], sin_ref[0, r, :]
            ksr[r, :] = _rope(ks_ref[0, 0, r, :], cos, sin).astype(_BF16)
            kwr[r, :] = _rope(kw_ref[0, 0, r, :], cos, sin).astype(_BF16)
            vsb[r, :] = vs_ref[0, 0, r, :].astype(_BF16)
            vwb[r, :] = vw_ref[0, 0, r, :].astype(_BF16)
            return 0

        lax.fori_loop(0, seq // chunk, prep, 0)

    q0 = pl.multiple_of(qi * tq, tq)
    cos_q, sin_q = cos_ref[0, pl.ds(q0, tq), :], sin_ref[0, pl.ds(q0, tq), :]
    for hh in range(C_GROUP):
        qr[hh * tq:(hh + 1) * tq, :] = _rope(q_ref[0, hh], cos_q, sin_q).astype(_BF16)
    q = qr[...]

    tk_s = 256
    sel = sel_ref[0, 0]
    t_s = q0 + lax.broadcasted_iota(jnp.int32, (tq, tk_s), 0)
    col_s = lax.broadcasted_iota(jnp.int32, (tq, tk_s), 1)
    e_blk = lax.broadcasted_iota(jnp.int32, (LANE, tk_s), 0)
    e_col = lax.broadcasted_iota(jnp.int32, (LANE, tk_s), 1)
    shift = SLC_BLOCK.bit_length() - 1

    def slc_mask(k0):
        expand = (e_blk == lax.shift_right_logical(k0 + e_col, shift)).astype(_BF16)
        return (_dot(sel, expand) > 0.5) & (k0 + col_s <= t_s)

    o_slc = _stream_softmax(q, ksr, vsb, 0, q0 // tk_s + 1, tk_s, slc_mask, m_s, l_s, acc_s)

    tk_w = tq
    t_w = q0 + lax.broadcasted_iota(jnp.int32, (tq, tk_w), 0)
    col_w = lax.broadcasted_iota(jnp.int32, (tq, tk_w), 1)

    def win_mask(k0):
        key = k0 + col_w
        return (key <= t_w) & (key > t_w - WIN_SIZE)

    o_win = _stream_softmax(q, kwr, vwb, jnp.maximum(qi - WIN_SIZE // tk_w, 0), qi + 1, tk_w, win_mask,
                            m_s, l_s, acc_s)

    gates = jax.nn.sigmoid(gl_ref[0, 0])
    for hh in range(C_GROUP):
        r = slice(hh * tq, (hh + 1) * tq)
        g_cmp, g_slc, g_win = (gates[:, 3 * hh + c:3 * hh + c + 1] for c in range(3))
        o = g_cmp * oc_ref[0, hh] + g_slc * o_slc[r] + g_win * o_win[r]
        o_ref[0, :, hh * HEAD_DIM:(hh + 1) * HEAD_DIM] = o.astype(o_ref.dtype)


def _nsa_attention(slabs, sel, o_cmp, cos, sin, tq=128):
    batch, _, seq, _ = slabs.shape
    kv = lambda off: pl.BlockSpec((1, 1, seq, LANE), lambda b, g, i: (b, off + g, 0, 0))
    table = pl.BlockSpec((1, seq, LANE), lambda b, g, i: (b, 0, 0))
    rows = C_GROUP * tq
    return pl.pallas_call(
        _nsa_kernel,
        grid=(batch, C_KV_HEADS, seq // tq),
        in_specs=[pl.BlockSpec((1, C_GROUP, tq, LANE), lambda b, g, i: (b, g, i, 0)),
                  kv(16), kv(18), kv(20), kv(22),
                  pl.BlockSpec((1, 1, tq, LANE), lambda b, g, i: (b, g, i, 0)),
                  pl.BlockSpec((1, C_GROUP, tq, LANE), lambda b, g, i: (b, g, i, 0)),
                  pl.BlockSpec((1, 1, tq, LANE), lambda b, g, i: (b, 24 + g, i, 0)),
                  table, table],
        out_specs=pl.BlockSpec((1, tq, C_GROUP * HEAD_DIM), lambda b, g, i: (b, i, g)),
        out_shape=jax.ShapeDtypeStruct((batch, seq, C_HEADS * HEAD_DIM), _BF16),
        scratch_shapes=[pltpu.VMEM((seq, HEAD_DIM), _BF16)] * 4
        + [pltpu.VMEM((rows, HEAD_DIM), _BF16)]
        + [pltpu.VMEM((rows, HEAD_DIM), _F32)] * 3,
        compiler_params=_params("arbitrary", "arbitrary", "arbitrary"),
        name="nsa_attention",
    )(slabs, slabs, slabs, slabs, slabs, sel, o_cmp, slabs, cos, sin)


def _conv_kernel(u_ref, c_ref, b_ref, w_ref, o_ref):
    seq = u_ref.shape[2]
    row = lax.broadcasted_iota(jnp.int32, (seq, LANE), 0)
    u = c_ref[0, 0] * u_ref[0, 0]
    taps = w_ref.shape[0]
    conv = _shift_rows(u, taps - 1, row) * w_ref[0:1, :]
    for j in range(1, taps):
        shifted = u if j == taps - 1 else _shift_rows(u, taps - 1 - j, row)
        conv = conv + shifted * w_ref[j:j + 1, :]
    o_ref[0] = (b_ref[0, 0] * conv).astype(o_ref.dtype)


def _short_conv(slabs, conv_w, first_slab):
    batch, _, seq, _ = slabs.shape
    nd = D_WIDTH // LANE
    part = lambda off: pl.BlockSpec((1, 1, seq, LANE), lambda b, j: (b, first_slab + off + j, 0, 0))
    return pl.pallas_call(
        _conv_kernel,
        grid=(batch, nd),
        in_specs=[part(0), part(nd), part(2 * nd),
                  pl.BlockSpec((conv_w.shape[0], LANE), lambda b, j: (0, j))],
        out_specs=pl.BlockSpec((1, seq, LANE), lambda b, j: (b, 0, j)),
        out_shape=jax.ShapeDtypeStruct((batch, seq, D_WIDTH), _BF16),
        compiler_params=_params("arbitrary", "arbitrary"),
        name="short_conv",
    )(slabs, slabs, slabs, conv_w)


def _pad_cols(w, total):
    return jnp.pad(w, ((0, 0), (0, total - w.shape[1])))


def _odd_in_weight(w):
    qkv = C_HEADS * HEAD_DIM + 6 * C_KV_HEADS * HEAD_DIM
    ngate = 3 * C_GROUP
    gates = [_pad_cols(w[:, qkv + g * ngate:qkv + (g + 1) * ngate], LANE) for g in range(C_KV_HEADS)]
    rest = w[:, qkv + C_KV_HEADS * ngate:]
    return _pad_cols(jnp.concatenate([w[:, :qkv]] + gates + [rest], axis=1), PROJ_SLABS * LANE)


def kernel(x, positions, norm_w, ffn_w_gate, ffn_w_up, ffn_w_down, ev_w_in, ev_w_out, pool_w, pool_scale,
           od_w_in, od_w_out, cmp_pe_k, cmp_w1_k, cmp_w2_k, cmp_pe_v, cmp_w1_v, cmp_w2_v, conv_w):
    batch, seq, d = x.shape
    depth = norm_w.shape[0]
    bf = lambda a: a.astype(_BF16)

    inv_freq = 1.0 / (ROPE_THETA ** (jnp.arange(0, HEAD_DIM, 2, dtype=_F32) / HEAD_DIM))
    ang = positions.astype(_F32)[..., None] * inv_freq
    cos = jnp.concatenate([jnp.cos(ang), jnp.cos(ang)], axis=-1)
    sin = jnp.concatenate([-jnp.sin(ang), jnp.sin(ang)], axis=-1)

    nw = norm_w.reshape(depth, 6, 1, d)
    h = x.reshape(batch * seq, d)

    def ffn(h, layer, which):
        a = _ffn_up(h, nw[layer, 4 * which], bf(ffn_w_gate[layer, which]), bf(ffn_w_up[layer, which]))
        return _ffn_down(a, bf(ffn_w_down[layer, which]), h, nw[layer, 4 * which + 1], HALF_STEP)

    for layer in range(depth):
        i = layer // 2
        h = ffn(h, layer, 0)
        if layer % 2 == 0:
            slabs = _norm_proj(h, nw[layer, 2], bf(ev_w_in[i]), batch, seq)
            o_a = _dilated_attention(slabs, cos, sin)
            o_b = _multiscale_pool(slabs, bf(pool_w[i]), pool_scale[i].reshape(1, -1))
            h = _mix_out(o_a.reshape(batch * seq, -1), o_b.reshape(batch * seq, -1), bf(ev_w_out[i]), h,
                         nw[layer, 3])
        else:
            slabs = _norm_proj(h, nw[layer, 2], bf(_odd_in_weight(od_w_in[i])), batch, seq)
            half = CMP_STRIDE * HEAD_DIM
            pe = jnp.stack([cmp_pe_k[i], cmp_pe_v[i]]).reshape(2, 2, 1, half)
            kv_cmp = _compress(slabs.reshape(batch, PROJ_SLABS, seq // CMP_STRIDE, half), pe,
                               bf(jnp.stack([cmp_w1_k[i], cmp_w1_v[i]])),
                               bf(jnp.stack([cmp_w2_k[i], cmp_w2_v[i]])), C_HEADS)
            o_cmp, sel = _cmp_select(slabs, kv_cmp)
            o_c = _nsa_attention(slabs, sel, o_cmp, cos, sin)
            y_d = _short_conv(slabs, conv_w[i], 26)
            h = _mix_out(o_c.reshape(batch * seq, -1), y_d.reshape(batch * seq, -1), bf(od_w_out[i]), h,
                         nw[layer, 3])
        h = ffn(h, layer, 1)
    return h.reshape(batch, seq, d)
```

```python
import functools

import jax
import jax.numpy as jnp
from jax import lax
from jax.experimental import pallas as pl
from jax.experimental.pallas import tpu as pltpu

HEAD_DIM = 128
ROPE_THETA = 10000.0
NORM_EPS = 1e-6
HALF_STEP = 0.5
LANE = 128

A_HEADS = 12
A_DILATIONS = (1, 4, 16)
A_BACK = 128
B_WINDOWS = (2, 4, 8, 16)
B_GROUPS = 4

C_HEADS = 12
C_KV_HEADS = 2
C_GROUP = C_HEADS // C_KV_HEADS
CMP_BLOCK = 32
CMP_STRIDE = 16
SLC_BLOCK = 64
SLC_TOPN = 8
WIN_SIZE = 512
FORCED_SCORE = 1e9
D_WIDTH = 512
PROJ_SLABS = 40

VMEM_LIMIT = 56 * 1024 * 1024

_BF16 = jnp.bfloat16
_F32 = jnp.float32


def _params(*sem):
    return pltpu.CompilerParams(dimension_semantics=sem, vmem_limit_bytes=VMEM_LIMIT)


def _rms(x, w):
    return x * lax.rsqrt(jnp.mean(x * x, axis=-1, keepdims=True) + NORM_EPS) * w


def _dot(a, b):
    return jnp.dot(a, b, preferred_element_type=_F32)


def _dot_t(a, b):
    return lax.dot_general(a, b, (((1,), (1,)), ((), ())), preferred_element_type=_F32)


def _rope(x, cos, sin):
    return x * cos + pltpu.roll(x, HEAD_DIM // 2, 1) * sin


def _shift_rows(x, k, row):
    return jnp.where(row >= k, pltpu.roll(x, k, 0), 0.0)


def _proj_kernel(h_ref, nw_ref, w_ref, o_ref, xn_ref):
    @pl.when(pl.program_id(1) == 0)
    def _():
        xn_ref[...] = _rms(h_ref[...], nw_ref[...]).astype(_BF16)

    acc = _dot(xn_ref[...], w_ref[...])
    for s in range(o_ref.shape[1]):
        o_ref[0, s] = acc[:, s * LANE:(s + 1) * LANE]


def _norm_proj(h, nw, w, batch, seq, tm=1024, tn=512):
    t, d = h.shape
    n = w.shape[1]
    per_b = seq // tm
    return pl.pallas_call(
        _proj_kernel,
        grid=(t // tm, n // tn),
        in_specs=[pl.BlockSpec((tm, d), lambda i, j: (i, 0)),
                  pl.BlockSpec((1, d), lambda i, j: (0, 0)),
                  pl.BlockSpec((d, tn), lambda i, j: (0, j))],
        out_specs=pl.BlockSpec((1, tn // LANE, tm, LANE), lambda i, j: (i // per_b, j, i % per_b, 0)),
        out_shape=jax.ShapeDtypeStruct((batch, n // LANE, seq, LANE), _F32),
        scratch_shapes=[pltpu.VMEM((tm, d), _BF16)],
        compiler_params=_params("arbitrary", "arbitrary"),
        name="norm_proj",
    )(h, nw, w)


def _ffn_up_kernel(h_ref, nw_ref, wg_ref, wu_ref, o_ref, xn_ref):
    @pl.when(pl.program_id(1) == 0)
    def _():
        xn_ref[...] = _rms(h_ref[...], nw_ref[...]).astype(_BF16)

    xn = xn_ref[...]
    g = _dot(xn, wg_ref[...])
    u = _dot(xn, wu_ref[...])
    o_ref[...] = (jax.nn.silu(g) * u).astype(_BF16)


def _ffn_up(h, nw, wg, wu, tm=1024, tn=512):
    t, d = h.shape
    f = wg.shape[1]
    return pl.pallas_call(
        _ffn_up_kernel,
        grid=(t // tm, f // tn),
        in_specs=[pl.BlockSpec((tm, d), lambda i, j: (i, 0)),
                  pl.BlockSpec((1, d), lambda i, j: (0, 0)),
                  pl.BlockSpec((d, tn), lambda i, j: (0, j)),
                  pl.BlockSpec((d, tn), lambda i, j: (0, j))],
        out_specs=pl.BlockSpec((tm, tn), lambda i, j: (i, j)),
        out_shape=jax.ShapeDtypeStruct((t, f), _BF16),
        scratch_shapes=[pltpu.VMEM((tm, d), _BF16)],
        compiler_params=_params("arbitrary", "arbitrary"),
        name="ffn_up",
    )(h, nw, wg, wu)


def _ffn_down_kernel(a_ref, w_ref, h_ref, nw_ref, o_ref, acc_ref, *, scale):
    k = pl.program_id(1)

    @pl.when(k == 0)
    def _():
        acc_ref[...] = jnp.zeros_like(acc_ref)

    acc_ref[...] += _dot(a_ref[...], w_ref[...])

    @pl.when(k == pl.num_programs(1) - 1)
    def _():
        o_ref[...] = h_ref[...] + scale * _rms(acc_ref[...], nw_ref[...])


def _ffn_down(a, w, h, nw, scale, tm=512, tk=1408):
    t, f = a.shape
    d = w.shape[1]
    return pl.pallas_call(
        functools.partial(_ffn_down_kernel, scale=scale),
        grid=(t // tm, f // tk),
        in_specs=[pl.BlockSpec((tm, tk), lambda i, k: (i, k)),
                  pl.BlockSpec((tk, d), lambda i, k: (k, 0)),
                  pl.BlockSpec((tm, d), lambda i, k: (i, 0)),
                  pl.BlockSpec((1, d), lambda i, k: (0, 0))],
        out_specs=pl.BlockSpec((tm, d), lambda i, k: (i, 0)),
        out_shape=jax.ShapeDtypeStruct((t, d), _F32),
        scratch_shapes=[pltpu.VMEM((tm, d), _F32)],
        compiler_params=_params("arbitrary", "arbitrary"),
        name="ffn_down",
    )(a, w, h, nw)


def _mix_out_kernel(x1_ref, x2_ref, w_ref, h_ref, nw_ref, o_ref):
    c1 = x1_ref.shape[1]
    m = _dot(x1_ref[...], w_ref[0:c1, :]) + _dot(x2_ref[...], w_ref[c1:, :])
    o_ref[...] = h_ref[...] + _rms(m, nw_ref[...])


def _mix_out(x1, x2, w, h, nw, tm=512):
    t, c1 = x1.shape
    c2 = x2.shape[1]
    d = w.shape[1]
    return pl.pallas_call(
        _mix_out_kernel,
        grid=(t // tm,),
        in_specs=[pl.BlockSpec((tm, c1), lambda i: (i, 0)),
                  pl.BlockSpec((tm, c2), lambda i: (i, 0)),
                  pl.BlockSpec((c1 + c2, d), lambda i: (0, 0)),
                  pl.BlockSpec((tm, d), lambda i: (i, 0)),
                  pl.BlockSpec((1, d), lambda i: (0, 0))],
        out_specs=pl.BlockSpec((tm, d), lambda i: (i, 0)),
        out_shape=jax.ShapeDtypeStruct((t, d), _F32),
        compiler_params=_params("arbitrary"),
        name="mix_out",
    )(x1, x2, w, h, nw)


def _rows(start, size, stride):
    return pl.ds(start, size) if stride == 1 else pl.ds(start, size, stride=stride)


def _dilated_kernel(q_ref, k_ref, v_ref, cos_ref, sin_ref, o_ref, qs, ks, pv_s, m_s, l_s):
    seq = q_ref.shape[2]
    blk = A_BACK
    scale = HEAD_DIM ** -0.5
    chunk = 256

    def rope_chunk(c, _):
        r = pl.ds(pl.multiple_of(c * chunk, chunk), chunk)
        cos, sin = cos_ref[0, r, :], sin_ref[0, r, :]
        qs[r, :] = _rope(q_ref[0, 0, r, :], cos, sin) * scale
        ks[r, :] = _rope(k_ref[0, 0, r, :], cos, sin)
        return 0

    lax.fori_loop(0, seq // chunk, rope_chunk, 0)

    qi = lax.broadcasted_iota(jnp.int32, (blk, blk), 0)
    kj = lax.broadcasted_iota(jnp.int32, (blk, blk), 1)
    cur_ok = kj <= qi

    for g, dil in enumerate(A_DILATIONS):
        nblk = seq // (blk * dil)

        def block(idx, _, g=g, dil=dil, nblk=nblk):
            r = idx // nblk
            n = idx % nblk
            start = r + n * (blk * dil)
            rows = _rows(start, blk, dil)
            qb = qs[rows, :].astype(_BF16)
            s_cur = jnp.where(cur_ok, _dot_t(qb, ks[rows, :].astype(_BF16)), -jnp.inf)
            m = jnp.max(s_cur, axis=-1, keepdims=True)
            if nblk > 1:
                prev_rows = _rows(r + jnp.maximum(n - 1, 0) * (blk * dil), blk, dil)
                s_prev = _dot_t(qb, ks[prev_rows, :].astype(_BF16))
                s_prev = jnp.where(kj >= qi + jnp.where(n > 0, 0, blk), s_prev, -jnp.inf)
                m = jnp.maximum(m, jnp.max(s_prev, axis=-1, keepdims=True))
            p_cur = jnp.exp(s_cur - m)
            den = jnp.sum(p_cur, axis=-1, keepdims=True)
            pv = _dot(p_cur.astype(_BF16), v_ref[0, 0, rows, :].astype(_BF16))
            if nblk > 1:
                p_prev = jnp.exp(s_prev - m)
                den = den + jnp.sum(p_prev, axis=-1, keepdims=True)
                pv = pv + _dot(p_prev.astype(_BF16), v_ref[0, 0, prev_rows, :].astype(_BF16))
            pv_s[g, rows, :] = pv
            m_s[g, rows, :] = jnp.broadcast_to(m, (blk, HEAD_DIM))
            l_s[g, rows, :] = jnp.broadcast_to(den, (blk, HEAD_DIM))
            return 0

        lax.fori_loop(0, seq // blk, block, 0, unroll=8)

    def merge_chunk(c, _):
        r = pl.ds(pl.multiple_of(c * chunk, chunk), chunk)
        m_all = jnp.maximum(jnp.maximum(m_s[0, r, :], m_s[1, r, :]), m_s[2, r, :])
        num = jnp.zeros((chunk, HEAD_DIM), _F32)
        den = jnp.zeros((chunk, HEAD_DIM), _F32)
        for g in range(len(A_DILATIONS)):
            w = jnp.exp(m_s[g, r, :] - m_all)
            num = num + w * pv_s[g, r, :]
            den = den + w * l_s[g, r, :]
        o_ref[0, r, :] = (num / den).astype(o_ref.dtype)
        return 0

    lax.fori_loop(0, seq // chunk, merge_chunk, 0)


def _dilated_attention(slabs, cos, sin):
    batch, _, seq, _ = slabs.shape
    head = lambda off: pl.BlockSpec((1, 1, seq, LANE), lambda b, h: (b, off + h, 0, 0))
    table = pl.BlockSpec((1, seq, LANE), lambda b, h: (b, 0, 0))
    return pl.pallas_call(
        _dilated_kernel,
        grid=(batch, A_HEADS),
        in_specs=[head(0), head(A_HEADS), head(2 * A_HEADS), table, table],
        out_specs=pl.BlockSpec((1, seq, LANE), lambda b, h: (b, 0, h)),
        out_shape=jax.ShapeDtypeStruct((batch, seq, A_HEADS * HEAD_DIM), _BF16),
        scratch_shapes=[pltpu.VMEM((seq, HEAD_DIM), _F32), pltpu.VMEM((seq, HEAD_DIM), _F32),
                        pltpu.VMEM((3, seq, HEAD_DIM), _F32), pltpu.VMEM((3, seq, HEAD_DIM), _F32),
                        pltpu.VMEM((3, seq, HEAD_DIM), _F32)],
        compiler_params=_params("arbitrary", "arbitrary"),
        name="dilated_attention",
    )(slabs, slabs, slabs, cos, sin)


def _pool_kernel(u_ref, w_ref, sc_ref, o_ref):
    seq = u_ref.shape[2]
    row = lax.broadcasted_iota(jnp.int32, (seq, LANE), 0)
    for g, win in enumerate(B_WINDOWS):
        x = u_ref[0, g]
        s = x
        step = 1
        while step < win:
            s = s + _shift_rows(s, step, row)
            step *= 2
        cnt = jnp.minimum(row + 1, win).astype(_F32)
        pooled = s / cnt - x
        mixed = _dot(pooled.astype(_BF16), w_ref[g]) * sc_ref[:, g * LANE:(g + 1) * LANE]
        o_ref[0, :, g * LANE:(g + 1) * LANE] = mixed.astype(o_ref.dtype)


def _multiscale_pool(slabs, pool_w, pool_scale):
    batch, _, seq, _ = slabs.shape
    first = 3 * A_HEADS // B_GROUPS
    return pl.pallas_call(
        _pool_kernel,
        grid=(batch,),
        in_specs=[pl.BlockSpec((1, B_GROUPS, seq, LANE), lambda b: (b, first, 0, 0)),
                  pl.BlockSpec((B_GROUPS, LANE, LANE), lambda b: (0, 0, 0)),
                  pl.BlockSpec((1, B_GROUPS * LANE), lambda b: (0, 0))],
        out_specs=pl.BlockSpec((1, seq, B_GROUPS * LANE), lambda b: (b, 0, 0)),
        out_shape=jax.ShapeDtypeStruct((batch, seq, B_GROUPS * LANE), _BF16),
        compiler_params=_params("arbitrary"),
        name="multiscale_pool",
    )(slabs, pool_w, pool_scale)


def _compress_kernel(x_ref, pe_ref, w1_ref, w2_ref, o_ref):
    nrow = o_ref.shape[3]
    hidden = w1_ref.shape[2]
    a = jnp.zeros((nrow, hidden), _F32)
    b = jnp.zeros((nrow, hidden), _F32)
    for tok in range(CMP_STRIDE):
        x = x_ref[0, 0, pl.ds(tok, nrow, stride=CMP_STRIDE), :]
        lo, hi = tok, CMP_STRIDE + tok
        a = a + _dot((x + pe_ref[0, lo:lo + 1, :]).astype(_BF16), w1_ref[0, lo * HEAD_DIM:(lo + 1) * HEAD_DIM, :])
        b = b + _dot((x + pe_ref[0, hi:hi + 1, :]).astype(_BF16), w1_ref[0, hi * HEAD_DIM:(hi + 1) * HEAD_DIM, :])
    hid = a + pltpu.roll(b, nrow - 1, 0)
    o_ref[0, 0, 0] = _dot(jax.nn.gelu(hid).astype(_BF16), w2_ref[0])


def _compress(slabs, pe, w1, w2, first_slab):
    batch, _, seq, _ = slabs.shape
    nrow = seq // CMP_STRIDE
    return pl.pallas_call(
        _compress_kernel,
        grid=(2, batch, C_KV_HEADS),
        in_specs=[pl.BlockSpec((1, 1, seq, LANE), lambda kv, b, g: (b, first_slab + 2 * kv + g, 0, 0)),
                  pl.BlockSpec((1, CMP_BLOCK, HEAD_DIM), lambda kv, b, g: (kv, 0, 0)),
                  pl.BlockSpec((1, CMP_BLOCK * HEAD_DIM, w1.shape[2]), lambda kv, b, g: (kv, 0, 0)),
                  pl.BlockSpec((1, w2.shape[1], HEAD_DIM), lambda kv, b, g: (kv, 0, 0))],
        out_specs=pl.BlockSpec((1, 1, 1, nrow, HEAD_DIM), lambda kv, b, g: (kv, b, g, 0, 0)),
        out_shape=jax.ShapeDtypeStruct((2, batch, C_KV_HEADS, nrow, HEAD_DIM), _F32),
        compiler_params=_params("arbitrary", "arbitrary", "arbitrary"),
        name="compress",
    )(slabs, pe, w1, w2)


def _cmp_select_kernel(q_ref, kc_ref, vc_ref, o_ref, sel_ref, *, nslc):
    tq = q_ref.shape[2]
    ncmp = kc_ref.shape[3]
    scale = HEAD_DIM ** -0.5
    t = pl.program_id(2) * tq + lax.broadcasted_iota(jnp.int32, (tq, ncmp), 0)
    n = lax.broadcasted_iota(jnp.int32, (tq, ncmp), 1)
    cmask = n * CMP_STRIDE + (CMP_BLOCK - 1) <= t

    cn = lax.broadcasted_iota(jnp.int32, (ncmp, LANE), 0) * CMP_STRIDE
    cj = lax.broadcasted_iota(jnp.int32, (ncmp, LANE), 1) * SLC_BLOCK
    cover = ((cn < cj + SLC_BLOCK) & (cn + CMP_BLOCK > cj)).astype(_BF16)

    kc = kc_ref[0, 0, 0].astype(_BF16)
    vc = vc_ref[0, 0, 0].astype(_BF16)
    imp = jnp.zeros((tq, LANE), _F32)
    for hh in range(C_GROUP):
        sc = jnp.where(cmask, _dot_t(q_ref[0, hh].astype(_BF16), kc) * scale, -jnp.inf)
        m = jnp.max(sc, axis=-1, keepdims=True)
        m = jnp.where(jnp.isfinite(m), m, 0.0)
        p = jnp.exp(sc - m)
        den = jnp.sum(p, axis=-1, keepdims=True)
        p = (p / jnp.maximum(den, 1.0)).astype(_BF16)
        o_ref[0, hh] = _dot(p, vc)
        imp = imp + _dot(p, cover)

    j = n
    cur = lax.shift_right_logical(t, SLC_BLOCK.bit_length() - 1)
    visible = j <= cur
    forced = visible & ((j == 0) | (j >= cur - 1))
    score = jnp.where(forced, FORCED_SCORE, jnp.where(visible, imp, -FORCED_SCORE))
    score = jnp.where(j < nslc, score, -jnp.inf)
    jf = j.astype(_F32)
    sel = jnp.zeros((tq, LANE), _F32)
    for _ in range(SLC_TOPN):
        best = jnp.max(score, axis=-1, keepdims=True)
        pick = jnp.min(jnp.where(score == best, jf, float(LANE)), axis=-1, keepdims=True)
        hit = jf == pick
        sel = jnp.where(hit, 1.0, sel)
        score = jnp.where(hit, -jnp.inf, score)
    sel_ref[0, 0] = sel.astype(sel_ref.dtype)


def _cmp_select(slabs, kv_cmp, tq=256):
    batch, _, seq, _ = slabs.shape
    cmp_spec = lambda which: pl.BlockSpec((1, 1, 1) + kv_cmp.shape[3:], lambda b, g, i: (which, b, g, 0, 0))
    return pl.pallas_call(
        functools.partial(_cmp_select_kernel, nslc=seq // SLC_BLOCK),
        grid=(batch, C_KV_HEADS, seq // tq),
        in_specs=[pl.BlockSpec((1, C_GROUP, tq, LANE), lambda b, g, i: (b, g, i, 0)),
                  cmp_spec(0), cmp_spec(1)],
        out_specs=[pl.BlockSpec((1, C_GROUP, tq, LANE), lambda b, g, i: (b, g, i, 0)),
                   pl.BlockSpec((1, 1, tq, LANE), lambda b, g, i: (b, g, i, 0))],
        out_shape=[jax.ShapeDtypeStruct((batch, C_HEADS, seq, HEAD_DIM), _F32),
                   jax.ShapeDtypeStruct((batch, C_KV_HEADS, seq, LANE), _BF16)],
        compiler_params=_params("arbitrary", "arbitrary", "arbitrary"),
        name="cmp_select",
    )(slabs, kv_cmp, kv_cmp)


def _group_softmax_pv(s, ok, v, p_s, o_s):
    rows, nk = s.shape
    tq = rows // C_GROUP
    den = []
    for hh in range(C_GROUP):
        r = slice(hh * tq, (hh + 1) * tq)
        sh = jnp.where(ok, s[r], -jnp.inf)
        p = jnp.exp(sh - jnp.max(sh, axis=-1, keepdims=True))
        den.append(jnp.sum(p, axis=-1, keepdims=True))
        p_s[r, 0:nk] = p.astype(_BF16)
    o = _dot(p_s[:, 0:nk], v)
    for hh in range(C_GROUP):
        r = slice(hh * tq, (hh + 1) * tq)
        o_s[r, :] = o[r] / den[hh]


def _nsa_kernel(q_ref, ks_ref, vs_ref, kw_ref, vw_ref, sel_ref, oc_ref, gl_ref, cos_ref, sin_ref, o_ref,
                ksr, vsb, kwr, vwb, expand_s, qr, p_s, oslc_s, owin_s):
    tq = q_ref.shape[2]
    seq = ks_ref.shape[2]
    qi = pl.program_id(2)
    chunk = 256
    shift = SLC_BLOCK.bit_length() - 1

    @pl.when(qi == 0)
    def _():
        def prep(c, _):
            r = pl.ds(pl.multiple_of(c * chunk, chunk), chunk)
            cos, sin = cos_ref[0, r, :], sin_ref[0, r, :]
            ksr[r, :] = _rope(ks_ref[0, 0, r, :], cos, sin).astype(_BF16)
            kwr[r, :] = _rope(kw_ref[0, 0, r, :], cos, sin).astype(_BF16)
            vsb[r, :] = vs_ref[0, 0, r, :].astype(_BF16)
            vwb[r, :] = vw_ref[0, 0, r, :].astype(_BF16)
            return 0

        lax.fori_loop(0, seq // chunk, prep, 0)
        blk = lax.broadcasted_iota(jnp.int32, (LANE, seq), 0)
        key = lax.broadcasted_iota(jnp.int32, (LANE, seq), 1)
        expand_s[...] = (blk == lax.shift_right_logical(key, shift)).astype(_BF16)

    q0 = pl.multiple_of(qi * tq, tq)
    cos_q, sin_q = cos_ref[0, pl.ds(q0, tq), :], sin_ref[0, pl.ds(q0, tq), :]
    for hh in range(C_GROUP):
        qr[hh * tq:(hh + 1) * tq, :] = (_rope(q_ref[0, hh], cos_q, sin_q) * HEAD_DIM ** -0.5).astype(_BF16)
    q = qr[...]

    n_extent = 4
    step = seq // n_extent
    for nt in range(1, n_extent + 1):
        @pl.when(q0 // step + 1 == nt)
        def _(nk=nt * step):
            s = _dot_t(q, ksr[0:nk, :])
            t = q0 + lax.broadcasted_iota(jnp.int32, (tq, nk), 0)
            key = lax.broadcasted_iota(jnp.int32, (tq, nk), 1)
            ok = (_dot(sel_ref[0, 0], expand_s[:, 0:nk]) > 0.5) & (key <= t)
            _group_softmax_pv(s, ok, vsb[0:nk, :], p_s, oslc_s)

    span = WIN_SIZE + tq
    k0 = pl.multiple_of(jnp.maximum(q0 - WIN_SIZE, 0), tq)
    s = _dot_t(q, kwr[pl.ds(k0, span), :])
    t = q0 + lax.broadcasted_iota(jnp.int32, (tq, span), 0)
    key = k0 + lax.broadcasted_iota(jnp.int32, (tq, span), 1)
    _group_softmax_pv(s, (key <= t) & (key > t - WIN_SIZE), vwb[pl.ds(k0, span), :], p_s, owin_s)

    gates = jax.nn.sigmoid(gl_ref[0, 0])
    for hh in range(C_GROUP):
        r = slice(hh * tq, (hh + 1) * tq)
        g_cmp, g_slc, g_win = (gates[:, 3 * hh + c:3 * hh + c + 1] for c in range(3))
        o = g_cmp * oc_ref[0, hh] + g_slc * oslc_s[r, :] + g_win * owin_s[r, :]
        o_ref[0, :, hh * HEAD_DIM:(hh + 1) * HEAD_DIM] = o.astype(o_ref.dtype)


def _nsa_attention(slabs, sel, o_cmp, cos, sin, tq=128):
    batch, _, seq, _ = slabs.shape
    kv = lambda off: pl.BlockSpec((1, 1, seq, LANE), lambda b, g, i: (b, off + g, 0, 0))
    table = pl.BlockSpec((1, seq, LANE), lambda b, g, i: (b, 0, 0))
    rows = C_GROUP * tq
    return pl.pallas_call(
        _nsa_kernel,
        grid=(batch, C_KV_HEADS, seq // tq),
        in_specs=[pl.BlockSpec((1, C_GROUP, tq, LANE), lambda b, g, i: (b, g, i, 0)),
                  kv(16), kv(18), kv(20), kv(22),
                  pl.BlockSpec((1, 1, tq, LANE), lambda b, g, i: (b, g, i, 0)),
                  pl.BlockSpec((1, C_GROUP, tq, LANE), lambda b, g, i: (b, g, i, 0)),
                  pl.BlockSpec((1, 1, tq, LANE), lambda b, g, i: (b, 24 + g, i, 0)),
                  table, table],
        out_specs=pl.BlockSpec((1, tq, C_GROUP * HEAD_DIM), lambda b, g, i: (b, i, g)),
        out_shape=jax.ShapeDtypeStruct((batch, seq, C_HEADS * HEAD_DIM), _BF16),
        scratch_shapes=[pltpu.VMEM((seq, HEAD_DIM), _BF16)] * 4
        + [pltpu.VMEM((LANE, seq), _BF16), pltpu.VMEM((rows, HEAD_DIM), _BF16), pltpu.VMEM((rows, seq), _BF16)]
        + [pltpu.VMEM((rows, HEAD_DIM), _F32)] * 2,
        compiler_params=_params("arbitrary", "arbitrary", "arbitrary"),
        name="nsa_attention",
    )(slabs, slabs, slabs, slabs, slabs, sel, o_cmp, slabs, cos, sin)


def _conv_kernel(u_ref, c_ref, b_ref, w_ref, o_ref):
    seq = u_ref.shape[2]
    row = lax.broadcasted_iota(jnp.int32, (seq, LANE), 0)
    u = c_ref[0, 0] * u_ref[0, 0]
    taps = w_ref.shape[0]
    conv = _shift_rows(u, taps - 1, row) * w_ref[0:1, :]
    for j in range(1, taps):
        shifted = u if j == taps - 1 else _shift_rows(u, taps - 1 - j, row)
        conv = conv + shifted * w_ref[j:j + 1, :]
    o_ref[0] = (b_ref[0, 0] * conv).astype(o_ref.dtype)


def _short_conv(slabs, conv_w, first_slab):
    batch, _, seq, _ = slabs.shape
    nd = D_WIDTH // LANE
    part = lambda off: pl.BlockSpec((1, 1, seq, LANE), lambda b, j: (b, first_slab + off + j, 0, 0))
    return pl.pallas_call(
        _conv_kernel,
        grid=(batch, nd),
        in_specs=[part(0), part(nd), part(2 * nd),
                  pl.BlockSpec((conv_w.shape[0], LANE), lambda b, j: (0, j))],
        out_specs=pl.BlockSpec((1, seq, LANE), lambda b, j: (b, 0, j)),
        out_shape=jax.ShapeDtypeStruct((batch, seq, D_WIDTH), _BF16),
        compiler_params=_params("arbitrary", "arbitrary"),
        name="short_conv",
    )(slabs, slabs, slabs, conv_w)


def _pad_cols(w, total):
    return jnp.pad(w, ((0, 0), (0, total - w.shape[1])))


def _odd_in_weight(w):
    qkv = C_HEADS * HEAD_DIM + 6 * C_KV_HEADS * HEAD_DIM
    ngate = 3 * C_GROUP
    gates = [_pad_cols(w[:, qkv + g * ngate:qkv + (g + 1) * ngate], LANE) for g in range(C_KV_HEADS)]
    rest = w[:, qkv + C_KV_HEADS * ngate:]
    return _pad_cols(jnp.concatenate([w[:, :qkv]] + gates + [rest], axis=1), PROJ_SLABS * LANE)


def kernel(x, positions, norm_w, ffn_w_gate, ffn_w_up, ffn_w_down, ev_w_in, ev_w_out, pool_w, pool_scale,
           od_w_in, od_w_out, cmp_pe_k, cmp_w1_k, cmp_w2_k, cmp_pe_v, cmp_w1_v, cmp_w2_v, conv_w):
    batch, seq, d = x.shape
    depth = norm_w.shape[0]
    bf = lambda a: a.astype(_BF16)

    inv_freq = 1.0 / (ROPE_THETA ** (jnp.arange(0, HEAD_DIM, 2, dtype=_F32) / HEAD_DIM))
    ang = positions.astype(_F32)[..., None] * inv_freq
    cos = jnp.concatenate([jnp.cos(ang), jnp.cos(ang)], axis=-1)
    sin = jnp.concatenate([-jnp.sin(ang), jnp.sin(ang)], axis=-1)

    nw = norm_w.reshape(depth, 6, 1, d)
    h = x.reshape(batch * seq, d)

    def ffn(h, layer, which):
        a = _ffn_up(h, nw[layer, 4 * which], bf(ffn_w_gate[layer, which]), bf(ffn_w_up[layer, which]))
        return _ffn_down(a, bf(ffn_w_down[layer, which]), h, nw[layer, 4 * which + 1], HALF_STEP)

    for layer in range(depth):
        i = layer // 2
        h = ffn(h, layer, 0)
        if layer % 2 == 0:
            slabs = _norm_proj(h, nw[layer, 2], bf(ev_w_in[i]), batch, seq)
            o_a = _dilated_attention(slabs, cos, sin)
            o_b = _multiscale_pool(slabs, bf(pool_w[i]), pool_scale[i].reshape(1, -1))
            h = _mix_out(o_a.reshape(batch * seq, -1), o_b.reshape(batch * seq, -1), bf(ev_w_out[i]), h,
                         nw[layer, 3])
        else:
            slabs = _norm_proj(h, nw[layer, 2], bf(_odd_in_weight(od_w_in[i])), batch, seq)
            kv_cmp = _compress(slabs, jnp.stack([cmp_pe_k[i], cmp_pe_v[i]]),
                               bf(jnp.stack([cmp_w1_k[i], cmp_w1_v[i]])),
                               bf(jnp.stack([cmp_w2_k[i], cmp_w2_v[i]])), C_HEADS)
            o_cmp, sel = _cmp_select(slabs, kv_cmp)
            o_c = _nsa_attention(slabs, sel, o_cmp, cos, sin)
            y_d = _short_conv(slabs, conv_w[i], 26)
            h = _mix_out(o_c.reshape(batch * seq, -1), y_d.reshape(batch * seq, -1), bf(od_w_out[i]), h,
                         nw[layer, 3])
        h = ffn(h, layer, 1)
    return h.reshape(batch, seq, d)
```

```python
import functools

import jax
import jax.numpy as jnp
from jax import lax
from jax.experimental import pallas as pl
from jax.experimental.pallas import tpu as pltpu

HEAD_DIM = 128
ROPE_THETA = 10000.0
NORM_EPS = 1e-6
HALF_STEP = 0.5
LANE = 128

A_HEADS = 12
A_DILATIONS = (1, 4, 16)
A_BACK = 128
B_WINDOWS = (2, 4, 8, 16)
B_GROUPS = 4

C_HEADS = 12
C_KV_HEADS = 2
C_GROUP = C_HEADS // C_KV_HEADS
CMP_BLOCK = 32
CMP_STRIDE = 16
SLC_BLOCK = 64
SLC_TOPN = 8
WIN_SIZE = 512
FORCED_SCORE = 1e9
D_WIDTH = 512
PROJ_SLABS = 40

VMEM_LIMIT = 56 * 1024 * 1024

_BF16 = jnp.bfloat16
_F32 = jnp.float32


def _params(*sem):
    return pltpu.CompilerParams(dimension_semantics=sem, vmem_limit_bytes=VMEM_LIMIT)


def _rms(x, w):
    return x * lax.rsqrt(jnp.mean(x * x, axis=-1, keepdims=True) + NORM_EPS) * w


def _dot(a, b):
    return jnp.dot(a, b, preferred_element_type=_F32)


def _dot_t(a, b):
    return lax.dot_general(a, b, (((1,), (1,)), ((), ())), preferred_element_type=_F32)


def _rope(x, cos, sin):
    return x * cos + pltpu.roll(x, HEAD_DIM // 2, 1) * sin


def _shift_rows(x, k, row):
    return jnp.where(row >= k, pltpu.roll(x, k, 0), 0.0)


def _proj_kernel(h_ref, nw_ref, w_ref, o_ref, xn_ref):
    @pl.when(pl.program_id(1) == 0)
    def _():
        xn_ref[...] = _rms(h_ref[...], nw_ref[...]).astype(_BF16)

    acc = _dot(xn_ref[...], w_ref[...])
    for s in range(o_ref.shape[1]):
        o_ref[0, s] = acc[:, s * LANE:(s + 1) * LANE]


def _norm_proj(h, nw, w, batch, seq, tm=1024, tn=1024):
    t, d = h.shape
    n = w.shape[1]
    per_b = seq // tm
    return pl.pallas_call(
        _proj_kernel,
        grid=(t // tm, n // tn),
        in_specs=[pl.BlockSpec((tm, d), lambda i, j: (i, 0)),
                  pl.BlockSpec((1, d), lambda i, j: (0, 0)),
                  pl.BlockSpec((d, tn), lambda i, j: (0, j))],
        out_specs=pl.BlockSpec((1, tn // LANE, tm, LANE), lambda i, j: (i // per_b, j, i % per_b, 0)),
        out_shape=jax.ShapeDtypeStruct((batch, n // LANE, seq, LANE), _F32),
        scratch_shapes=[pltpu.VMEM((tm, d), _BF16)],
        compiler_params=_params("arbitrary", "arbitrary"),
        name="norm_proj",
    )(h, nw, w)


def _ffn_up_kernel(h_ref, nw_ref, wg_ref, wu_ref, o_ref, xn_ref):
    @pl.when(pl.program_id(1) == 0)
    def _():
        xn_ref[...] = _rms(h_ref[...], nw_ref[...]).astype(_BF16)

    xn = xn_ref[...]
    g = _dot(xn, wg_ref[...])
    u = _dot(xn, wu_ref[...])
    o_ref[...] = (jax.nn.silu(g) * u).astype(_BF16)


def _ffn_up(h, nw, wg, wu, tm=1024, tn=512):
    t, d = h.shape
    f = wg.shape[1]
    return pl.pallas_call(
        _ffn_up_kernel,
        grid=(t // tm, f // tn),
        in_specs=[pl.BlockSpec((tm, d), lambda i, j: (i, 0)),
                  pl.BlockSpec((1, d), lambda i, j: (0, 0)),
                  pl.BlockSpec((d, tn), lambda i, j: (0, j)),
                  pl.BlockSpec((d, tn), lambda i, j: (0, j))],
        out_specs=pl.BlockSpec((tm, tn), lambda i, j: (i, j)),
        out_shape=jax.ShapeDtypeStruct((t, f), _BF16),
        scratch_shapes=[pltpu.VMEM((tm, d), _BF16)],
        compiler_params=_params("arbitrary", "arbitrary"),
        name="ffn_up",
    )(h, nw, wg, wu)


def _ffn_down_kernel(a_ref, w_ref, h_ref, nw_ref, o_ref, acc_ref, *, scale):
    k = pl.program_id(1)

    @pl.when(k == 0)
    def _():
        acc_ref[...] = jnp.zeros_like(acc_ref)

    acc_ref[...] += _dot(a_ref[...], w_ref[...])

    @pl.when(k == pl.num_programs(1) - 1)
    def _():
        o_ref[...] = h_ref[...] + scale * _rms(acc_ref[...], nw_ref[...])


def _ffn_down(a, w, h, nw, scale, tm=512, tk=2816):
    t, f = a.shape
    d = w.shape[1]
    return pl.pallas_call(
        functools.partial(_ffn_down_kernel, scale=scale),
        grid=(t // tm, f // tk),
        in_specs=[pl.BlockSpec((tm, tk), lambda i, k: (i, k)),
                  pl.BlockSpec((tk, d), lambda i, k: (k, 0)),
                  pl.BlockSpec((tm, d), lambda i, k: (i, 0)),
                  pl.BlockSpec((1, d), lambda i, k: (0, 0))],
        out_specs=pl.BlockSpec((tm, d), lambda i, k: (i, 0)),
        out_shape=jax.ShapeDtypeStruct((t, d), _F32),
        scratch_shapes=[pltpu.VMEM((tm, d), _F32)],
        compiler_params=_params("arbitrary", "arbitrary"),
        name="ffn_down",
    )(a, w, h, nw)


def _mix_out_kernel(x1_ref, x2_ref, w_ref, h_ref, nw_ref, o_ref):
    c1 = x1_ref.shape[1]
    m = _dot(x1_ref[...], w_ref[0:c1, :]) + _dot(x2_ref[...], w_ref[c1:, :])
    o_ref[...] = h_ref[...] + _rms(m, nw_ref[...])


def _mix_out(x1, x2, w, h, nw, tm=512):
    t, c1 = x1.shape
    c2 = x2.shape[1]
    d = w.shape[1]
    return pl.pallas_call(
        _mix_out_kernel,
        grid=(t // tm,),
        in_specs=[pl.BlockSpec((tm, c1), lambda i: (i, 0)),
                  pl.BlockSpec((tm, c2), lambda i: (i, 0)),
                  pl.BlockSpec((c1 + c2, d), lambda i: (0, 0)),
                  pl.BlockSpec((tm, d), lambda i: (i, 0)),
                  pl.BlockSpec((1, d), lambda i: (0, 0))],
        out_specs=pl.BlockSpec((tm, d), lambda i: (i, 0)),
        out_shape=jax.ShapeDtypeStruct((t, d), _F32),
        compiler_params=_params("arbitrary"),
        name="mix_out",
    )(x1, x2, w, h, nw)


def _rows(start, size, stride):
    return pl.ds(start, size) if stride == 1 else pl.ds(start, size, stride=stride)


def _dilated_kernel(q_ref, k_ref, v_ref, cos_ref, sin_ref, o_ref, qs, ks, pv_s, m_s, l_s):
    seq = q_ref.shape[2]
    blk = A_BACK
    scale = HEAD_DIM ** -0.5
    chunk = 256

    def prep_chunk(c, _):
        r = pl.ds(pl.multiple_of(c * chunk, chunk), chunk)
        cos, sin = cos_ref[0, r, :], sin_ref[0, r, :]
        qs[r, :] = _rope(q_ref[0, 0, r, :], cos, sin) * scale
        ks[r, :] = _rope(k_ref[0, 0, r, :], cos, sin)
        return 0

    lax.fori_loop(0, seq // chunk, prep_chunk, 0)

    qi = lax.broadcasted_iota(jnp.int32, (blk, blk), 0)
    kj = lax.broadcasted_iota(jnp.int32, (blk, blk), 1)
    cur_ok = kj <= qi
    prev_ok = kj >= qi
    ones = jnp.ones((blk, HEAD_DIM), _BF16)

    for g, dil in enumerate(A_DILATIONS):
        for res in range(dil):
            k_prev = v_prev = None
            for n in range(seq // (blk * dil)):
                rows = _rows(res + n * blk * dil, blk, dil)
                qb = qs[rows, :].astype(_BF16)
                k_cur = ks[rows, :].astype(_BF16)
                v_cur = jnp.concatenate([v_ref[0, 0, rows, :].astype(_BF16), ones], axis=1)
                s_cur = jnp.where(cur_ok, _dot_t(qb, k_cur), -jnp.inf)
                if n == 0:
                    m = jnp.max(s_cur, axis=-1, keepdims=True)
                    pva = _dot(jnp.exp(s_cur - m).astype(_BF16), v_cur)
                else:
                    s_prev = jnp.where(prev_ok, _dot_t(qb, k_prev), -jnp.inf)
                    m = jnp.max(jnp.maximum(s_cur, s_prev), axis=-1, keepdims=True)
                    pva = (_dot(jnp.exp(s_cur - m).astype(_BF16), v_cur)
                           + _dot(jnp.exp(s_prev - m).astype(_BF16), v_prev))
                pv_s[g, rows, :] = pva[:, 0:HEAD_DIM]
                l_s[g, rows, :] = pva[:, HEAD_DIM:]
                m_s[g, rows, :] = jnp.broadcast_to(m, (blk, HEAD_DIM))
                k_prev, v_prev = k_cur, v_cur

    def merge_chunk(c, _):
        r = pl.ds(pl.multiple_of(c * chunk, chunk), chunk)
        m_all = jnp.maximum(jnp.maximum(m_s[0, r, :], m_s[1, r, :]), m_s[2, r, :])
        num = jnp.zeros((chunk, HEAD_DIM), _F32)
        den = jnp.zeros((chunk, HEAD_DIM), _F32)
        for g in range(len(A_DILATIONS)):
            w = jnp.exp(m_s[g, r, :] - m_all)
            num = num + w * pv_s[g, r, :]
            den = den + w * l_s[g, r, :]
        o_ref[0, r, :] = (num / den).astype(o_ref.dtype)
        return 0

    lax.fori_loop(0, seq // chunk, merge_chunk, 0)


def _dilated_attention(slabs, cos, sin):
    batch, _, seq, _ = slabs.shape
    head = lambda off: pl.BlockSpec((1, 1, seq, LANE), lambda b, h: (b, off + h, 0, 0))
    table = pl.BlockSpec((1, seq, LANE), lambda b, h: (b, 0, 0))
    return pl.pallas_call(
        _dilated_kernel,
        grid=(batch, A_HEADS),
        in_specs=[head(0), head(A_HEADS), head(2 * A_HEADS), table, table],
        out_specs=pl.BlockSpec((1, seq, LANE), lambda b, h: (b, 0, h)),
        out_shape=jax.ShapeDtypeStruct((batch, seq, A_HEADS * HEAD_DIM), _BF16),
        scratch_shapes=[pltpu.VMEM((seq, HEAD_DIM), _F32), pltpu.VMEM((seq, HEAD_DIM), _F32),
                        pltpu.VMEM((3, seq, HEAD_DIM), _F32), pltpu.VMEM((3, seq, HEAD_DIM), _F32),
                        pltpu.VMEM((3, seq, HEAD_DIM), _F32)],
        compiler_params=_params("arbitrary", "arbitrary"),
        name="dilated_attention",
    )(slabs, slabs, slabs, cos, sin)


def _pool_kernel(u_ref, w_ref, sc_ref, o_ref):
    seq = u_ref.shape[2]
    row = lax.broadcasted_iota(jnp.int32, (seq, LANE), 0)
    for g, win in enumerate(B_WINDOWS):
        x = u_ref[0, g]
        s = x
        step = 1
        while step < win:
            s = s + _shift_rows(s, step, row)
            step *= 2
        cnt = jnp.minimum(row + 1, win).astype(_F32)
        pooled = s / cnt - x
        mixed = _dot(pooled.astype(_BF16), w_ref[g]) * sc_ref[:, g * LANE:(g + 1) * LANE]
        o_ref[0, :, g * LANE:(g + 1) * LANE] = mixed.astype(o_ref.dtype)


def _multiscale_pool(slabs, pool_w, pool_scale):
    batch, _, seq, _ = slabs.shape
    first = 3 * A_HEADS // B_GROUPS
    return pl.pallas_call(
        _pool_kernel,
        grid=(batch,),
        in_specs=[pl.BlockSpec((1, B_GROUPS, seq, LANE), lambda b: (b, first, 0, 0)),
                  pl.BlockSpec((B_GROUPS, LANE, LANE), lambda b: (0, 0, 0)),
                  pl.BlockSpec((1, B_GROUPS * LANE), lambda b: (0, 0))],
        out_specs=pl.BlockSpec((1, seq, B_GROUPS * LANE), lambda b: (b, 0, 0)),
        out_shape=jax.ShapeDtypeStruct((batch, seq, B_GROUPS * LANE), _BF16),
        compiler_params=_params("arbitrary"),
        name="multiscale_pool",
    )(slabs, pool_w, pool_scale)


def _compress_kernel(x_ref, pe_ref, w1_ref, w2_ref, o_ref):
    nrow = o_ref.shape[3]
    hidden = w1_ref.shape[2]
    a = jnp.zeros((nrow, hidden), _F32)
    b = jnp.zeros((nrow, hidden), _F32)
    for tok in range(CMP_STRIDE):
        x = x_ref[0, 0, pl.ds(tok, nrow, stride=CMP_STRIDE), :]
        lo, hi = tok, CMP_STRIDE + tok
        a = a + _dot((x + pe_ref[0, lo:lo + 1, :]).astype(_BF16), w1_ref[0, lo * HEAD_DIM:(lo + 1) * HEAD_DIM, :])
        b = b + _dot((x + pe_ref[0, hi:hi + 1, :]).astype(_BF16), w1_ref[0, hi * HEAD_DIM:(hi + 1) * HEAD_DIM, :])
    hid = a + pltpu.roll(b, nrow - 1, 0)
    o_ref[0, 0, 0] = _dot(jax.nn.gelu(hid).astype(_BF16), w2_ref[0])


def _compress(slabs, pe, w1, w2, first_slab):
    batch, _, seq, _ = slabs.shape
    nrow = seq // CMP_STRIDE
    return pl.pallas_call(
        _compress_kernel,
        grid=(2, batch, C_KV_HEADS),
        in_specs=[pl.BlockSpec((1, 1, seq, LANE), lambda kv, b, g: (b, first_slab + 2 * kv + g, 0, 0)),
                  pl.BlockSpec((1, CMP_BLOCK, HEAD_DIM), lambda kv, b, g: (kv, 0, 0)),
                  pl.BlockSpec((1, CMP_BLOCK * HEAD_DIM, w1.shape[2]), lambda kv, b, g: (kv, 0, 0)),
                  pl.BlockSpec((1, w2.shape[1], HEAD_DIM), lambda kv, b, g: (kv, 0, 0))],
        out_specs=pl.BlockSpec((1, 1, 1, nrow, HEAD_DIM), lambda kv, b, g: (kv, b, g, 0, 0)),
        out_shape=jax.ShapeDtypeStruct((2, batch, C_KV_HEADS, nrow, HEAD_DIM), _F32),
        compiler_params=_params("arbitrary", "arbitrary", "arbitrary"),
        name="compress",
    )(slabs, pe, w1, w2)


def _cmp_select_kernel(q_ref, kc_ref, vc_ref, o_ref, sel_ref, *, nslc):
    tq = q_ref.shape[2]
    ncmp = kc_ref.shape[3]
    scale = HEAD_DIM ** -0.5
    t = pl.program_id(2) * tq + lax.broadcasted_iota(jnp.int32, (tq, ncmp), 0)
    n = lax.broadcasted_iota(jnp.int32, (tq, ncmp), 1)
    cmask = n * CMP_STRIDE + (CMP_BLOCK - 1) <= t

    cn = lax.broadcasted_iota(jnp.int32, (ncmp, LANE), 0) * CMP_STRIDE
    cj = lax.broadcasted_iota(jnp.int32, (ncmp, LANE), 1) * SLC_BLOCK
    cover = ((cn < cj + SLC_BLOCK) & (cn + CMP_BLOCK > cj)).astype(_BF16)

    kc = kc_ref[0, 0, 0].astype(_BF16)
    vc = vc_ref[0, 0, 0].astype(_BF16)
    imp = jnp.zeros((tq, LANE), _F32)
    for hh in range(C_GROUP):
        sc = jnp.where(cmask, _dot_t(q_ref[0, hh].astype(_BF16), kc) * scale, -jnp.inf)
        m = jnp.max(sc, axis=-1, keepdims=True)
        m = jnp.where(jnp.isfinite(m), m, 0.0)
        p = jnp.exp(sc - m)
        den = jnp.sum(p, axis=-1, keepdims=True)
        p = (p / jnp.maximum(den, 1.0)).astype(_BF16)
        o_ref[0, hh] = _dot(p, vc)
        imp = imp + _dot(p, cover)

    j = n
    cur = lax.shift_right_logical(t, SLC_BLOCK.bit_length() - 1)
    visible = j <= cur
    forced = visible & ((j == 0) | (j >= cur - 1))
    score = jnp.where(forced, FORCED_SCORE, jnp.where(visible, imp, -FORCED_SCORE))
    score = jnp.where(j < nslc, score, -jnp.inf)
    jf = j.astype(_F32)
    sel = jnp.zeros((tq, LANE), _F32)
    for _ in range(SLC_TOPN):
        best = jnp.max(score, axis=-1, keepdims=True)
        pick = jnp.min(jnp.where(score == best, jf, float(LANE)), axis=-1, keepdims=True)
        hit = jf == pick
        sel = jnp.where(hit, 1.0, sel)
        score = jnp.where(hit, -jnp.inf, score)
    sel_ref[0, 0] = sel.astype(sel_ref.dtype)


def _cmp_select(slabs, kv_cmp, tq=256):
    batch, _, seq, _ = slabs.shape
    cmp_spec = lambda which: pl.BlockSpec((1, 1, 1) + kv_cmp.shape[3:], lambda b, g, i: (which, b, g, 0, 0))
    return pl.pallas_call(
        functools.partial(_cmp_select_kernel, nslc=seq // SLC_BLOCK),
        grid=(batch, C_KV_HEADS, seq // tq),
        in_specs=[pl.BlockSpec((1, C_GROUP, tq, LANE), lambda b, g, i: (b, g, i, 0)),
                  cmp_spec(0), cmp_spec(1)],
        out_specs=[pl.BlockSpec((1, C_GROUP, tq, LANE), lambda b, g, i: (b, g, i, 0)),
                   pl.BlockSpec((1, 1, tq, LANE), lambda b, g, i: (b, g, i, 0))],
        out_shape=[jax.ShapeDtypeStruct((batch, C_HEADS, seq, HEAD_DIM), _F32),
                   jax.ShapeDtypeStruct((batch, C_KV_HEADS, seq, LANE), _BF16)],
        compiler_params=_params("arbitrary", "arbitrary", "arbitrary"),
        name="cmp_select",
    )(slabs, kv_cmp, kv_cmp)


def _group_softmax_pv(q, k, v1, bias_s, s_s, p_s, o_s):
    nk = k.shape[0]
    tq = bias_s.shape[0]
    piece = 512 if nk % 512 == 0 else LANE
    pieces = [slice(c, c + piece) for c in range(0, nk, piece)]
    s_s[:, 0:nk] = _dot_t(q, k)
    for hh in range(C_GROUP):
        r = slice(hh * tq, (hh + 1) * tq)
        m = None
        for c in pieces:
            mc = jnp.max(s_s[r, c] + bias_s[:, c], axis=-1, keepdims=True)
            m = mc if m is None else jnp.maximum(m, mc)
        for c in pieces:
            p_s[r, c] = jnp.exp(s_s[r, c] + bias_s[:, c] - m).astype(_BF16)
    o = _dot(p_s[:, 0:nk], v1)
    o_s[...] = o[:, 0:HEAD_DIM] / o[:, HEAD_DIM:]


def _nsa_kernel(q_ref, ks_ref, vs_ref, kw_ref, vw_ref, sel_ref, oc_ref, gl_ref, cos_ref, sin_ref, o_ref,
                ksr, vs1, kwr, vw1, expand_s, qr, bias_s, s_s, p_s, oslc_s, owin_s):
    tq = q_ref.shape[2]
    seq = ks_ref.shape[2]
    qi = pl.program_id(2)
    chunk = 256
    shift = SLC_BLOCK.bit_length() - 1

    @pl.when(qi == 0)
    def _():
        def prep(c, _):
            r = pl.ds(pl.multiple_of(c * chunk, chunk), chunk)
            cos, sin = cos_ref[0, r, :], sin_ref[0, r, :]
            ones = jnp.ones((chunk, HEAD_DIM), _BF16)
            ksr[r, :] = _rope(ks_ref[0, 0, r, :], cos, sin).astype(_BF16)
            kwr[r, :] = _rope(kw_ref[0, 0, r, :], cos, sin).astype(_BF16)
            vs1[r, 0:HEAD_DIM] = vs_ref[0, 0, r, :].astype(_BF16)
            vs1[r, HEAD_DIM:] = ones
            vw1[r, 0:HEAD_DIM] = vw_ref[0, 0, r, :].astype(_BF16)
            vw1[r, HEAD_DIM:] = ones
            return 0

        lax.fori_loop(0, seq // chunk, prep, 0)
        blk = lax.broadcasted_iota(jnp.int32, (LANE, seq), 0)
        key = lax.broadcasted_iota(jnp.int32, (LANE, seq), 1)
        expand_s[...] = (blk == lax.shift_right_logical(key, shift)).astype(_BF16)

    q0 = pl.multiple_of(qi * tq, tq)
    cos_q, sin_q = cos_ref[0, pl.ds(q0, tq), :], sin_ref[0, pl.ds(q0, tq), :]
    for hh in range(C_GROUP):
        qr[hh * tq:(hh + 1) * tq, :] = (_rope(q_ref[0, hh], cos_q, sin_q) * HEAD_DIM ** -0.5).astype(_BF16)
    q = qr[...]

    n_extent = 4
    step = seq // n_extent
    for nt in range(1, n_extent + 1):
        @pl.when(q0 // step + 1 == nt)
        def _(nk=nt * step):
            for c in range(0, nk, step):
                t = q0 + lax.broadcasted_iota(jnp.int32, (tq, step), 0)
                key = c + lax.broadcasted_iota(jnp.int32, (tq, step), 1)
                ok = (_dot(sel_ref[0, 0], expand_s[:, c:c + step]) > 0.5) & (key <= t)
                bias_s[:, c:c + step] = jnp.where(ok, 0.0, -jnp.inf)
            _group_softmax_pv(q, ksr[0:nk, :], vs1[0:nk, :], bias_s, s_s, p_s, oslc_s)

    span = WIN_SIZE + tq
    k0 = pl.multiple_of(jnp.maximum(q0 - WIN_SIZE, 0), tq)
    t = q0 + lax.broadcasted_iota(jnp.int32, (tq, span), 0)
    key = k0 + lax.broadcasted_iota(jnp.int32, (tq, span), 1)
    bias_s[:, 0:span] = jnp.where((key <= t) & (key > t - WIN_SIZE), 0.0, -jnp.inf)
    _group_softmax_pv(q, kwr[pl.ds(k0, span), :], vw1[pl.ds(k0, span), :], bias_s, s_s, p_s, owin_s)

    gates = jax.nn.sigmoid(gl_ref[0, 0])
    for hh in range(C_GROUP):
        r = slice(hh * tq, (hh + 1) * tq)
        g_cmp, g_slc, g_win = (gates[:, 3 * hh + c:3 * hh + c + 1] for c in range(3))
        o = g_cmp * oc_ref[0, hh] + g_slc * oslc_s[r, :] + g_win * owin_s[r, :]
        o_ref[0, :, hh * HEAD_DIM:(hh + 1) * HEAD_DIM] = o.astype(o_ref.dtype)


def _nsa_attention(slabs, sel, o_cmp, cos, sin, tq=128):
    batch, _, seq, _ = slabs.shape
    kv = lambda off: pl.BlockSpec((1, 1, seq, LANE), lambda b, g, i: (b, off + g, 0, 0))
    table = pl.BlockSpec((1, seq, LANE), lambda b, g, i: (b, 0, 0))
    rows = C_GROUP * tq
    return pl.pallas_call(
        _nsa_kernel,
        grid=(batch, C_KV_HEADS, seq // tq),
        in_specs=[pl.BlockSpec((1, C_GROUP, tq, LANE), lambda b, g, i: (b, g, i, 0)),
                  kv(16), kv(18), kv(20), kv(22),
                  pl.BlockSpec((1, 1, tq, LANE), lambda b, g, i: (b, g, i, 0)),
                  pl.BlockSpec((1, C_GROUP, tq, LANE), lambda b, g, i: (b, g, i, 0)),
                  pl.BlockSpec((1, 1, tq, LANE), lambda b, g, i: (b, 24 + g, i, 0)),
                  table, table],
        out_specs=pl.BlockSpec((1, tq, C_GROUP * HEAD_DIM), lambda b, g, i: (b, i, g)),
        out_shape=jax.ShapeDtypeStruct((batch, seq, C_HEADS * HEAD_DIM), _BF16),
        scratch_shapes=[pltpu.VMEM((seq, HEAD_DIM), _BF16), pltpu.VMEM((seq, 2 * HEAD_DIM), _BF16)] * 2
        + [pltpu.VMEM((LANE, seq), _BF16), pltpu.VMEM((rows, HEAD_DIM), _BF16),
           pltpu.VMEM((tq, seq), _F32), pltpu.VMEM((rows, seq), _F32), pltpu.VMEM((rows, seq), _BF16)]
        + [pltpu.VMEM((rows, HEAD_DIM), _F32)] * 2,
        compiler_params=_params("arbitrary", "arbitrary", "arbitrary"),
        name="nsa_attention",
    )(slabs, slabs, slabs, slabs, slabs, sel, o_cmp, slabs, cos, sin)


def _conv_kernel(u_ref, c_ref, b_ref, w_ref, o_ref):
    seq = u_ref.shape[2]
    row = lax.broadcasted_iota(jnp.int32, (seq, LANE), 0)
    u = c_ref[0, 0] * u_ref[0, 0]
    taps = w_ref.shape[0]
    conv = _shift_rows(u, taps - 1, row) * w_ref[0:1, :]
    for j in range(1, taps):
        shifted = u if j == taps - 1 else _shift_rows(u, taps - 1 - j, row)
        conv = conv + shifted * w_ref[j:j + 1, :]
    o_ref[0] = (b_ref[0, 0] * conv).astype(o_ref.dtype)


def _short_conv(slabs, conv_w, first_slab):
    batch, _, seq, _ = slabs.shape
    nd = D_WIDTH // LANE
    part = lambda off: pl.BlockSpec((1, 1, seq, LANE), lambda b, j: (b, first_slab + off + j, 0, 0))
    return pl.pallas_call(
        _conv_kernel,
        grid=(batch, nd),
        in_specs=[part(0), part(nd), part(2 * nd),
                  pl.BlockSpec((conv_w.shape[0], LANE), lambda b, j: (0, j))],
        out_specs=pl.BlockSpec((1, seq, LANE), lambda b, j: (b, 0, j)),
        out_shape=jax.ShapeDtypeStruct((batch, seq, D_WIDTH), _BF16),
        compiler_params=_params("arbitrary", "arbitrary"),
        name="short_conv",
    )(slabs, slabs, slabs, conv_w)


def _pad_cols(w, total):
    return jnp.pad(w, ((0, 0), (0, total - w.shape[1])))


def _odd_in_weight(w):
    qkv = C_HEADS * HEAD_DIM + 6 * C_KV_HEADS * HEAD_DIM
    ngate = 3 * C_GROUP
    gates = [_pad_cols(w[:, qkv + g * ngate:qkv + (g + 1) * ngate], LANE) for g in range(C_KV_HEADS)]
    rest = w[:, qkv + C_KV_HEADS * ngate:]
    return _pad_cols(jnp.concatenate([w[:, :qkv]] + gates + [rest], axis=1), PROJ_SLABS * LANE)


def kernel(x, positions, norm_w, ffn_w_gate, ffn_w_up, ffn_w_down, ev_w_in, ev_w_out, pool_w, pool_scale,
           od_w_in, od_w_out, cmp_pe_k, cmp_w1_k, cmp_w2_k, cmp_pe_v, cmp_w1_v, cmp_w2_v, conv_w):
    batch, seq, d = x.shape
    depth = norm_w.shape[0]
    bf = lambda a: a.astype(_BF16)

    inv_freq = 1.0 / (ROPE_THETA ** (jnp.arange(0, HEAD_DIM, 2, dtype=_F32) / HEAD_DIM))
    ang = positions.astype(_F32)[..., None] * inv_freq
    cos = jnp.concatenate([jnp.cos(ang), jnp.cos(ang)], axis=-1)
    sin = jnp.concatenate([-jnp.sin(ang), jnp.sin(ang)], axis=-1)

    nw = norm_w.reshape(depth, 6, 1, d)
    h = x.reshape(batch * seq, d)

    def ffn(h, layer, which):
        a = _ffn_up(h, nw[layer, 4 * which], bf(ffn_w_gate[layer, which]), bf(ffn_w_up[layer, which]))
        return _ffn_down(a, bf(ffn_w_down[layer, which]), h, nw[layer, 4 * which + 1], HALF_STEP)

    for layer in range(depth):
        i = layer // 2
        h = ffn(h, layer, 0)
        if layer % 2 == 0:
            slabs = _norm_proj(h, nw[layer, 2], bf(ev_w_in[i]), batch, seq)
            o_a = _dilated_attention(slabs, cos, sin)
            o_b = _multiscale_pool(slabs, bf(pool_w[i]), pool_scale[i].reshape(1, -1))
            h = _mix_out(o_a.reshape(batch * seq, -1), o_b.reshape(batch * seq, -1), bf(ev_w_out[i]), h,
                         nw[layer, 3])
        else:
            slabs = _norm_proj(h, nw[layer, 2], bf(_odd_in_weight(od_w_in[i])), batch, seq)
            kv_cmp = _compress(slabs, jnp.stack([cmp_pe_k[i], cmp_pe_v[i]]),
                               bf(jnp.stack([cmp_w1_k[i], cmp_w1_v[i]])),
                               bf(jnp.stack([cmp_w2_k[i], cmp_w2_v[i]])), C_HEADS)
            o_cmp, sel = _cmp_select(slabs, kv_cmp)
            o_c = _nsa_attention(slabs, sel, o_cmp, cos, sin)
            y_d = _short_conv(slabs, conv_w[i], 26)
            h = _mix_out(o_c.reshape(batch * seq, -1), y_d.reshape(batch * seq, -1), bf(od_w_out[i]), h,
                         nw[layer, 3])
        h = ffn(h, layer, 1)
    return h.reshape(batch, seq, d)
```

```python
import functools

import jax
import jax.numpy as jnp
from jax import lax
from jax.experimental import pallas as pl
from jax.experimental.pallas import tpu as pltpu

HEAD_DIM = 128
ROPE_THETA = 10000.0
NORM_EPS = 1e-6
HALF_STEP = 0.5
LANE = 128

A_HEADS = 12
A_DILATIONS = (1, 4, 16)
A_BACK = 128
B_WINDOWS = (2, 4, 8, 16)
B_GROUPS = 4

C_HEADS = 12
C_KV_HEADS = 2
C_GROUP = C_HEADS // C_KV_HEADS
CMP_BLOCK = 32
CMP_STRIDE = 16
SLC_BLOCK = 64
SLC_TOPN = 8
WIN_SIZE = 512
FORCED_SCORE = 1e9
D_WIDTH = 512
PROJ_SLABS = 40

VMEM_LIMIT = 56 * 1024 * 1024

_BF16 = jnp.bfloat16
_F32 = jnp.float32


def _params(*sem):
    return pltpu.CompilerParams(dimension_semantics=sem, vmem_limit_bytes=VMEM_LIMIT)


def _rms(x, w):
    return x * lax.rsqrt(jnp.mean(x * x, axis=-1, keepdims=True) + NORM_EPS) * w


def _dot(a, b):
    return jnp.dot(a, b, preferred_element_type=_F32)


def _dot_t(a, b):
    return lax.dot_general(a, b, (((1,), (1,)), ((), ())), preferred_element_type=_F32)


def _rope(x, cos, sin):
    return x * cos + pltpu.roll(x, HEAD_DIM // 2, 1) * sin


def _shift_rows(x, k, row):
    return jnp.where(row >= k, pltpu.roll(x, k, 0), 0.0)


def _proj_kernel(h_ref, nw_ref, w_ref, o_ref, xn_ref):
    @pl.when(pl.program_id(1) == 0)
    def _():
        xn_ref[...] = _rms(h_ref[...], nw_ref[...]).astype(_BF16)

    acc = _dot(xn_ref[...], w_ref[...])
    for s in range(o_ref.shape[1]):
        o_ref[0, s] = acc[:, s * LANE:(s + 1) * LANE]


def _stacked(block, index_map, lead):
    return pl.BlockSpec((None,) * len(lead) + block, lambda *g: tuple(lead) + index_map(*g))


def _norm_proj(h, nw, w, lead, batch, seq, tm=1024, tn=1024):
    t, d = h.shape
    n = w.shape[-1]
    per_b = seq // tm
    return pl.pallas_call(
        _proj_kernel,
        grid=(t // tm, n // tn),
        in_specs=[pl.BlockSpec((tm, d), lambda i, j: (i, 0)),
                  pl.BlockSpec((1, d), lambda i, j: (0, 0)),
                  _stacked((d, tn), lambda i, j: (0, j), lead)],
        out_specs=pl.BlockSpec((1, tn // LANE, tm, LANE), lambda i, j: (i // per_b, j, i % per_b, 0)),
        out_shape=jax.ShapeDtypeStruct((batch, n // LANE, seq, LANE), _F32),
        scratch_shapes=[pltpu.VMEM((tm, d), _BF16)],
        compiler_params=_params("arbitrary", "arbitrary"),
        name="norm_proj",
    )(h, nw, w)


def _ffn_up_kernel(h_ref, nw_ref, wg_ref, wu_ref, o_ref, xn_ref):
    @pl.when(pl.program_id(1) == 0)
    def _():
        xn_ref[...] = _rms(h_ref[...], nw_ref[...]).astype(_BF16)

    xn = xn_ref[...]
    g = _dot(xn, wg_ref[...])
    u = _dot(xn, wu_ref[...])
    o_ref[...] = (jax.nn.silu(g) * u).astype(_BF16)


def _ffn_up(h, nw, wg, wu, lead, tm=1024, tn=512):
    t, d = h.shape
    f = wg.shape[-1]
    return pl.pallas_call(
        _ffn_up_kernel,
        grid=(t // tm, f // tn),
        in_specs=[pl.BlockSpec((tm, d), lambda i, j: (i, 0)),
                  pl.BlockSpec((1, d), lambda i, j: (0, 0)),
                  _stacked((d, tn), lambda i, j: (0, j), lead),
                  _stacked((d, tn), lambda i, j: (0, j), lead)],
        out_specs=pl.BlockSpec((tm, tn), lambda i, j: (i, j)),
        out_shape=jax.ShapeDtypeStruct((t, f), _BF16),
        scratch_shapes=[pltpu.VMEM((tm, d), _BF16)],
        compiler_params=_params("arbitrary", "arbitrary"),
        name="ffn_up",
    )(h, nw, wg, wu)


def _ffn_down_kernel(a_ref, w_ref, h_ref, nw_ref, o_ref, acc_ref, *, scale):
    k = pl.program_id(1)

    @pl.when(k == 0)
    def _():
        acc_ref[...] = jnp.zeros_like(acc_ref)

    acc_ref[...] += _dot(a_ref[...], w_ref[...])

    @pl.when(k == pl.num_programs(1) - 1)
    def _():
        o_ref[...] = h_ref[...] + scale * _rms(acc_ref[...], nw_ref[...])


def _ffn_down(a, w, lead, h, nw, scale, tm=512, tk=2816):
    t, f = a.shape
    d = w.shape[-1]
    return pl.pallas_call(
        functools.partial(_ffn_down_kernel, scale=scale),
        grid=(t // tm, f // tk),
        in_specs=[pl.BlockSpec((tm, tk), lambda i, k: (i, k)),
                  _stacked((tk, d), lambda i, k: (k, 0), lead),
                  pl.BlockSpec((tm, d), lambda i, k: (i, 0)),
                  pl.BlockSpec((1, d), lambda i, k: (0, 0))],
        out_specs=pl.BlockSpec((tm, d), lambda i, k: (i, 0)),
        out_shape=jax.ShapeDtypeStruct((t, d), _F32),
        scratch_shapes=[pltpu.VMEM((tm, d), _F32)],
        compiler_params=_params("arbitrary", "arbitrary"),
        name="ffn_down",
    )(a, w, h, nw)


def _mix_out_kernel(x1_ref, x2_ref, w_ref, h_ref, nw_ref, o_ref):
    c1 = x1_ref.shape[1]
    m = _dot(x1_ref[...], w_ref[0:c1, :]) + _dot(x2_ref[...], w_ref[c1:, :])
    o_ref[...] = h_ref[...] + _rms(m, nw_ref[...])


def _mix_out(x1, x2, w, lead, h, nw, tm=512):
    t, c1 = x1.shape
    c2 = x2.shape[1]
    d = w.shape[-1]
    return pl.pallas_call(
        _mix_out_kernel,
        grid=(t // tm,),
        in_specs=[pl.BlockSpec((tm, c1), lambda i: (i, 0)),
                  pl.BlockSpec((tm, c2), lambda i: (i, 0)),
                  _stacked((c1 + c2, d), lambda i: (0, 0), lead),
                  pl.BlockSpec((tm, d), lambda i: (i, 0)),
                  pl.BlockSpec((1, d), lambda i: (0, 0))],
        out_specs=pl.BlockSpec((tm, d), lambda i: (i, 0)),
        out_shape=jax.ShapeDtypeStruct((t, d), _F32),
        compiler_params=_params("arbitrary"),
        name="mix_out",
    )(x1, x2, w, h, nw)


def _rows(start, size, stride):
    return pl.ds(start, size) if stride == 1 else pl.ds(start, size, stride=stride)


def _dilated_kernel(q_ref, k_ref, v_ref, cos_ref, sin_ref, o_ref, qs, ks, pv_s, m_s, l_s):
    seq = q_ref.shape[2]
    blk = A_BACK
    scale = HEAD_DIM ** -0.5
    chunk = 256

    def prep_chunk(c, _):
        r = pl.ds(pl.multiple_of(c * chunk, chunk), chunk)
        cos, sin = cos_ref[0, r, :], sin_ref[0, r, :]
        qs[r, :] = _rope(q_ref[0, 0, r, :], cos, sin) * scale
        ks[r, :] = _rope(k_ref[0, 0, r, :], cos, sin)
        return 0

    lax.fori_loop(0, seq // chunk, prep_chunk, 0)

    qi = lax.broadcasted_iota(jnp.int32, (blk, blk), 0)
    kj = lax.broadcasted_iota(jnp.int32, (blk, blk), 1)
    cur_ok = kj <= qi
    prev_ok = kj >= qi
    ones = jnp.ones((blk, HEAD_DIM), _BF16)

    for g, dil in enumerate(A_DILATIONS):
        for res in range(dil):
            k_prev = v_prev = None
            for n in range(seq // (blk * dil)):
                rows = _rows(res + n * blk * dil, blk, dil)
                qb = qs[rows, :].astype(_BF16)
                k_cur = ks[rows, :].astype(_BF16)
                v_cur = jnp.concatenate([v_ref[0, 0, rows, :].astype(_BF16), ones], axis=1)
                s_cur = jnp.where(cur_ok, _dot_t(qb, k_cur), -jnp.inf)
                if n == 0:
                    m = jnp.max(s_cur, axis=-1, keepdims=True)
                    pva = _dot(jnp.exp(s_cur - m).astype(_BF16), v_cur)
                else:
                    s_prev = jnp.where(prev_ok, _dot_t(qb, k_prev), -jnp.inf)
                    m = jnp.max(jnp.maximum(s_cur, s_prev), axis=-1, keepdims=True)
                    pva = (_dot(jnp.exp(s_cur - m).astype(_BF16), v_cur)
                           + _dot(jnp.exp(s_prev - m).astype(_BF16), v_prev))
                pv_s[g, rows, :] = pva[:, 0:HEAD_DIM]
                l_s[g, rows, :] = pva[:, HEAD_DIM:]
                m_s[g, rows, :] = jnp.broadcast_to(m, (blk, HEAD_DIM))
                k_prev, v_prev = k_cur, v_cur

    def merge_chunk(c, _):
        r = pl.ds(pl.multiple_of(c * chunk, chunk), chunk)
        m_all = jnp.maximum(jnp.maximum(m_s[0, r, :], m_s[1, r, :]), m_s[2, r, :])
        num = jnp.zeros((chunk, HEAD_DIM), _F32)
        den = jnp.zeros((chunk, HEAD_DIM), _F32)
        for g in range(len(A_DILATIONS)):
            w = jnp.exp(m_s[g, r, :] - m_all)
            num = num + w * pv_s[g, r, :]
            den = den + w * l_s[g, r, :]
        o_ref[0, r, :] = (num / den).astype(o_ref.dtype)
        return 0

    lax.fori_loop(0, seq // chunk, merge_chunk, 0)


def _dilated_attention(slabs, cos, sin):
    batch, _, seq, _ = slabs.shape
    head = lambda off: pl.BlockSpec((1, 1, seq, LANE), lambda b, h: (b, off + h, 0, 0))
    table = pl.BlockSpec((1, seq, LANE), lambda b, h: (b, 0, 0))
    return pl.pallas_call(
        _dilated_kernel,
        grid=(batch, A_HEADS),
        in_specs=[head(0), head(A_HEADS), head(2 * A_HEADS), table, table],
        out_specs=pl.BlockSpec((1, seq, LANE), lambda b, h: (b, 0, h)),
        out_shape=jax.ShapeDtypeStruct((batch, seq, A_HEADS * HEAD_DIM), _BF16),
        scratch_shapes=[pltpu.VMEM((seq, HEAD_DIM), _F32), pltpu.VMEM((seq, HEAD_DIM), _F32),
                        pltpu.VMEM((3, seq, HEAD_DIM), _F32), pltpu.VMEM((3, seq, HEAD_DIM), _F32),
                        pltpu.VMEM((3, seq, HEAD_DIM), _F32)],
        compiler_params=_params("arbitrary", "arbitrary"),
        name="dilated_attention",
    )(slabs, slabs, slabs, cos, sin)


def _pool_kernel(u_ref, w_ref, sc_ref, o_ref):
    seq = u_ref.shape[2]
    row = lax.broadcasted_iota(jnp.int32, (seq, LANE), 0)
    for g, win in enumerate(B_WINDOWS):
        x = u_ref[0, g]
        s = x
        step = 1
        while step < win:
            s = s + _shift_rows(s, step, row)
            step *= 2
        cnt = jnp.minimum(row + 1, win).astype(_F32)
        pooled = s / cnt - x
        mixed = _dot(pooled.astype(_BF16), w_ref[g]) * sc_ref[:, g * LANE:(g + 1) * LANE]
        o_ref[0, :, g * LANE:(g + 1) * LANE] = mixed.astype(o_ref.dtype)


def _multiscale_pool(slabs, pool_w, pool_scale):
    batch, _, seq, _ = slabs.shape
    first = 3 * A_HEADS // B_GROUPS
    return pl.pallas_call(
        _pool_kernel,
        grid=(batch,),
        in_specs=[pl.BlockSpec((1, B_GROUPS, seq, LANE), lambda b: (b, first, 0, 0)),
                  pl.BlockSpec((B_GROUPS, LANE, LANE), lambda b: (0, 0, 0)),
                  pl.BlockSpec((1, B_GROUPS * LANE), lambda b: (0, 0))],
        out_specs=pl.BlockSpec((1, seq, B_GROUPS * LANE), lambda b: (b, 0, 0)),
        out_shape=jax.ShapeDtypeStruct((batch, seq, B_GROUPS * LANE), _BF16),
        compiler_params=_params("arbitrary"),
        name="multiscale_pool",
    )(slabs, pool_w, pool_scale)


def _compress_kernel(x_ref, pe_ref, w1_ref, w2_ref, o_ref):
    nrow = o_ref.shape[3]
    hidden = w1_ref.shape[2]
    a = jnp.zeros((nrow, hidden), _F32)
    b = jnp.zeros((nrow, hidden), _F32)
    for tok in range(CMP_STRIDE):
        x = x_ref[0, 0, pl.ds(tok, nrow, stride=CMP_STRIDE), :]
        lo, hi = tok, CMP_STRIDE + tok
        a = a + _dot((x + pe_ref[0, lo:lo + 1, :]).astype(_BF16), w1_ref[0, lo * HEAD_DIM:(lo + 1) * HEAD_DIM, :])
        b = b + _dot((x + pe_ref[0, hi:hi + 1, :]).astype(_BF16), w1_ref[0, hi * HEAD_DIM:(hi + 1) * HEAD_DIM, :])
    hid = a + pltpu.roll(b, nrow - 1, 0)
    o_ref[0, 0, 0] = _dot(jax.nn.gelu(hid).astype(_BF16), w2_ref[0])


def _compress(slabs, pe, w1, w2, first_slab):
    batch, _, seq, _ = slabs.shape
    nrow = seq // CMP_STRIDE
    return pl.pallas_call(
        _compress_kernel,
        grid=(2, batch, C_KV_HEADS),
        in_specs=[pl.BlockSpec((1, 1, seq, LANE), lambda kv, b, g: (b, first_slab + 2 * kv + g, 0, 0)),
                  pl.BlockSpec((1, CMP_BLOCK, HEAD_DIM), lambda kv, b, g: (kv, 0, 0)),
                  pl.BlockSpec((1, CMP_BLOCK * HEAD_DIM, w1.shape[2]), lambda kv, b, g: (kv, 0, 0)),
                  pl.BlockSpec((1, w2.shape[1], HEAD_DIM), lambda kv, b, g: (kv, 0, 0))],
        out_specs=pl.BlockSpec((1, 1, 1, nrow, HEAD_DIM), lambda kv, b, g: (kv, b, g, 0, 0)),
        out_shape=jax.ShapeDtypeStruct((2, batch, C_KV_HEADS, nrow, HEAD_DIM), _F32),
        compiler_params=_params("arbitrary", "arbitrary", "arbitrary"),
        name="compress",
    )(slabs, pe, w1, w2)


def _cmp_select_kernel(q_ref, kc_ref, vc_ref, o_ref, sel_ref, *, nslc):
    tq = q_ref.shape[2]
    ncmp = kc_ref.shape[3]
    scale = HEAD_DIM ** -0.5
    t = pl.program_id(2) * tq + lax.broadcasted_iota(jnp.int32, (tq, ncmp), 0)
    n = lax.broadcasted_iota(jnp.int32, (tq, ncmp), 1)
    cmask = n * CMP_STRIDE + (CMP_BLOCK - 1) <= t

    cj = lax.broadcasted_iota(jnp.int32, (LANE, ncmp), 0) * SLC_BLOCK
    cn = lax.broadcasted_iota(jnp.int32, (LANE, ncmp), 1) * CMP_STRIDE
    cover_t = ((cn < cj + SLC_BLOCK) & (cn + CMP_BLOCK > cj)).astype(_BF16)

    kc = kc_ref[0, 0, 0].astype(_BF16)
    vc = vc_ref[0, 0, 0].astype(_BF16)
    imp_t = jnp.zeros((LANE, tq), _F32)
    for hh in range(C_GROUP):
        sc = jnp.where(cmask, _dot_t(q_ref[0, hh].astype(_BF16), kc) * scale, -jnp.inf)
        m = jnp.max(sc, axis=-1, keepdims=True)
        m = jnp.where(jnp.isfinite(m), m, 0.0)
        p = jnp.exp(sc - m)
        den = jnp.sum(p, axis=-1, keepdims=True)
        p = (p / jnp.maximum(den, 1.0)).astype(_BF16)
        o_ref[0, hh] = _dot(p, vc)
        imp_t = imp_t + _dot_t(cover_t, p)

    j = lax.broadcasted_iota(jnp.int32, (nslc, tq), 0)
    tt = pl.program_id(2) * tq + lax.broadcasted_iota(jnp.int32, (nslc, tq), 1)
    cur = lax.shift_right_logical(tt, SLC_BLOCK.bit_length() - 1)
    visible = j <= cur
    forced = visible & ((j == 0) | (j >= cur - 1))
    score = jnp.where(forced, FORCED_SCORE, jnp.where(visible, imp_t[0:nslc, :], -FORCED_SCORE))
    jf = j.astype(_F32)
    sel_t = jnp.zeros((nslc, tq), _F32)
    for _ in range(min(SLC_TOPN, nslc)):
        best = jnp.max(score, axis=0, keepdims=True)
        pick = jnp.min(jnp.where(score == best, jf, float(nslc)), axis=0, keepdims=True)
        hit = jf == pick
        sel_t = jnp.where(hit, 1.0, sel_t)
        score = jnp.where(hit, -jnp.inf, score)
    sel_t = jnp.concatenate([sel_t, jnp.zeros((LANE - nslc, tq), _F32)], axis=0)
    sel_ref[0, 0] = sel_t.T.astype(sel_ref.dtype)


def _cmp_select(slabs, kv_cmp, tq=256):
    batch, _, seq, _ = slabs.shape
    cmp_spec = lambda which: pl.BlockSpec((1, 1, 1) + kv_cmp.shape[3:], lambda b, g, i: (which, b, g, 0, 0))
    return pl.pallas_call(
        functools.partial(_cmp_select_kernel, nslc=seq // SLC_BLOCK),
        grid=(batch, C_KV_HEADS, seq // tq),
        in_specs=[pl.BlockSpec((1, C_GROUP, tq, LANE), lambda b, g, i: (b, g, i, 0)),
                  cmp_spec(0), cmp_spec(1)],
        out_specs=[pl.BlockSpec((1, C_GROUP, tq, LANE), lambda b, g, i: (b, g, i, 0)),
                   pl.BlockSpec((1, 1, tq, LANE), lambda b, g, i: (b, g, i, 0))],
        out_shape=[jax.ShapeDtypeStruct((batch, C_HEADS, seq, HEAD_DIM), _F32),
                   jax.ShapeDtypeStruct((batch, C_KV_HEADS, seq, LANE), _BF16)],
        compiler_params=_params("arbitrary", "arbitrary", "arbitrary"),
        name="cmp_select",
    )(slabs, kv_cmp, kv_cmp)


def _group_softmax_pv(q, k, v1, bias_s, s_s, p_s, o_s):
    nk = k.shape[0]
    tq = bias_s.shape[0]
    piece = 512 if nk % 512 == 0 else LANE
    pieces = [slice(c, c + piece) for c in range(0, nk, piece)]
    s_s[:, 0:nk] = _dot_t(q, k)
    for hh in range(C_GROUP):
        r = slice(hh * tq, (hh + 1) * tq)
        m = None
        for c in pieces:
            mc = jnp.max(s_s[r, c] + bias_s[:, c], axis=-1, keepdims=True)
            m = mc if m is None else jnp.maximum(m, mc)
        for c in pieces:
            p_s[r, c] = jnp.exp(s_s[r, c] + bias_s[:, c] - m).astype(_BF16)
    o = _dot(p_s[:, 0:nk], v1)
    o_s[...] = o[:, 0:HEAD_DIM] / o[:, HEAD_DIM:]


def _nsa_kernel(q_ref, ks_ref, vs_ref, kw_ref, vw_ref, sel_ref, oc_ref, gl_ref, cos_ref, sin_ref, o_ref,
                ksr, vs1, kwr, vw1, expand_s, qr, bias_s, s_s, p_s, bias_w, s_w, p_w, oslc_s, owin_s):
    tq = q_ref.shape[2]
    seq = ks_ref.shape[2]
    qi = pl.program_id(2)
    chunk = 256
    shift = SLC_BLOCK.bit_length() - 1

    @pl.when(qi == 0)
    def _():
        def prep(c, _):
            r = pl.ds(pl.multiple_of(c * chunk, chunk), chunk)
            cos, sin = cos_ref[0, r, :], sin_ref[0, r, :]
            ones = jnp.ones((chunk, HEAD_DIM), _BF16)
            ksr[r, :] = _rope(ks_ref[0, 0, r, :], cos, sin).astype(_BF16)
            kwr[r, :] = _rope(kw_ref[0, 0, r, :], cos, sin).astype(_BF16)
            vs1[r, 0:HEAD_DIM] = vs_ref[0, 0, r, :].astype(_BF16)
            vs1[r, HEAD_DIM:] = ones
            vw1[r, 0:HEAD_DIM] = vw_ref[0, 0, r, :].astype(_BF16)
            vw1[r, HEAD_DIM:] = ones
            return 0

        lax.fori_loop(0, seq // chunk, prep, 0)
        blk = lax.broadcasted_iota(jnp.int32, (LANE, seq), 0)
        key = lax.broadcasted_iota(jnp.int32, (LANE, seq), 1)
        expand_s[...] = (blk == lax.shift_right_logical(key, shift)).astype(_BF16)

    q0 = pl.multiple_of(qi * tq, tq)
    cos_q, sin_q = cos_ref[0, pl.ds(q0, tq), :], sin_ref[0, pl.ds(q0, tq), :]
    for hh in range(C_GROUP):
        qr[hh * tq:(hh + 1) * tq, :] = (_rope(q_ref[0, hh], cos_q, sin_q) * HEAD_DIM ** -0.5).astype(_BF16)
    q = qr[...]

    def window_branch():
        span = WIN_SIZE + tq
        k0 = pl.multiple_of(jnp.maximum(q0 - WIN_SIZE, 0), tq)
        t = q0 + lax.broadcasted_iota(jnp.int32, (tq, span), 0)
        key = k0 + lax.broadcasted_iota(jnp.int32, (tq, span), 1)
        bias_w[...] = jnp.where((key <= t) & (key > t - WIN_SIZE), 0.0, -jnp.inf)
        _group_softmax_pv(q, kwr[pl.ds(k0, span), :], vw1[pl.ds(k0, span), :], bias_w, s_w, p_w, owin_s)

    n_extent = 4
    step = seq // n_extent
    for nt in range(1, n_extent + 1):
        @pl.when(q0 // step + 1 == nt)
        def _(nk=nt * step):
            for c in range(0, nk, step):
                t = q0 + lax.broadcasted_iota(jnp.int32, (tq, step), 0)
                key = c + lax.broadcasted_iota(jnp.int32, (tq, step), 1)
                ok = (_dot(sel_ref[0, 0], expand_s[:, c:c + step]) > 0.5) & (key <= t)
                bias_s[:, c:c + step] = jnp.where(ok, 0.0, -jnp.inf)
            _group_softmax_pv(q, ksr[0:nk, :], vs1[0:nk, :], bias_s, s_s, p_s, oslc_s)
            window_branch()

    gates = jax.nn.sigmoid(gl_ref[0, 0])
    for hh in range(C_GROUP):
        r = slice(hh * tq, (hh + 1) * tq)
        g_cmp, g_slc, g_win = (gates[:, 3 * hh + c:3 * hh + c + 1] for c in range(3))
        o = g_cmp * oc_ref[0, hh] + g_slc * oslc_s[r, :] + g_win * owin_s[r, :]
        o_ref[0, :, hh * HEAD_DIM:(hh + 1) * HEAD_DIM] = o.astype(o_ref.dtype)


def _nsa_attention(slabs, sel, o_cmp, cos, sin, tq=128):
    batch, _, seq, _ = slabs.shape
    kv = lambda off: pl.BlockSpec((1, 1, seq, LANE), lambda b, g, i: (b, off + g, 0, 0))
    table = pl.BlockSpec((1, seq, LANE), lambda b, g, i: (b, 0, 0))
    rows = C_GROUP * tq
    span = WIN_SIZE + tq
    return pl.pallas_call(
        _nsa_kernel,
        grid=(batch, C_KV_HEADS, seq // tq),
        in_specs=[pl.BlockSpec((1, C_GROUP, tq, LANE), lambda b, g, i: (b, g, i, 0)),
                  kv(16), kv(18), kv(20), kv(22),
                  pl.BlockSpec((1, 1, tq, LANE), lambda b, g, i: (b, g, i, 0)),
                  pl.BlockSpec((1, C_GROUP, tq, LANE), lambda b, g, i: (b, g, i, 0)),
                  pl.BlockSpec((1, 1, tq, LANE), lambda b, g, i: (b, 24 + g, i, 0)),
                  table, table],
        out_specs=pl.BlockSpec((1, tq, C_GROUP * HEAD_DIM), lambda b, g, i: (b, i, g)),
        out_shape=jax.ShapeDtypeStruct((batch, seq, C_HEADS * HEAD_DIM), _BF16),
        scratch_shapes=[pltpu.VMEM((seq, HEAD_DIM), _BF16), pltpu.VMEM((seq, 2 * HEAD_DIM), _BF16)] * 2
        + [pltpu.VMEM((LANE, seq), _BF16), pltpu.VMEM((rows, HEAD_DIM), _BF16),
           pltpu.VMEM((tq, seq), _F32), pltpu.VMEM((rows, seq), _F32), pltpu.VMEM((rows, seq), _BF16),
           pltpu.VMEM((tq, span), _F32), pltpu.VMEM((rows, span), _F32), pltpu.VMEM((rows, span), _BF16)]
        + [pltpu.VMEM((rows, HEAD_DIM), _F32)] * 2,
        compiler_params=_params("arbitrary", "arbitrary", "arbitrary"),
        name="nsa_attention",
    )(slabs, slabs, slabs, slabs, slabs, sel, o_cmp, slabs, cos, sin)


def _conv_kernel(u_ref, c_ref, b_ref, w_ref, o_ref):
    seq = u_ref.shape[2]
    row = lax.broadcasted_iota(jnp.int32, (seq, LANE), 0)
    u = c_ref[0, 0] * u_ref[0, 0]
    taps = w_ref.shape[0]
    conv = _shift_rows(u, taps - 1, row) * w_ref[0:1, :]
    for j in range(1, taps):
        shifted = u if j == taps - 1 else _shift_rows(u, taps - 1 - j, row)
        conv = conv + shifted * w_ref[j:j + 1, :]
    o_ref[0] = (b_ref[0, 0] * conv).astype(o_ref.dtype)


def _short_conv(slabs, conv_w, first_slab):
    batch, _, seq, _ = slabs.shape
    nd = D_WIDTH // LANE
    part = lambda off: pl.BlockSpec((1, 1, seq, LANE), lambda b, j: (b, first_slab + off + j, 0, 0))
    return pl.pallas_call(
        _conv_kernel,
        grid=(batch, nd),
        in_specs=[part(0), part(nd), part(2 * nd),
                  pl.BlockSpec((conv_w.shape[0], LANE), lambda b, j: (0, j))],
        out_specs=pl.BlockSpec((1, seq, LANE), lambda b, j: (b, 0, j)),
        out_shape=jax.ShapeDtypeStruct((batch, seq, D_WIDTH), _BF16),
        compiler_params=_params("arbitrary", "arbitrary"),
        name="short_conv",
    )(slabs, slabs, slabs, conv_w)


def _pad_cols(w, total):
    return jnp.pad(w, ((0, 0), (0, total - w.shape[1])))


def _odd_in_weight(w):
    qkv = C_HEADS * HEAD_DIM + 6 * C_KV_HEADS * HEAD_DIM
    ngate = 3 * C_GROUP
    gates = [_pad_cols(w[:, qkv + g * ngate:qkv + (g + 1) * ngate], LANE) for g in range(C_KV_HEADS)]
    rest = w[:, qkv + C_KV_HEADS * ngate:]
    return _pad_cols(jnp.concatenate([w[:, :qkv]] + gates + [rest], axis=1), PROJ_SLABS * LANE)


def kernel(x, positions, norm_w, ffn_w_gate, ffn_w_up, ffn_w_down, ev_w_in, ev_w_out, pool_w, pool_scale,
           od_w_in, od_w_out, cmp_pe_k, cmp_w1_k, cmp_w2_k, cmp_pe_v, cmp_w1_v, cmp_w2_v, conv_w):
    batch, seq, d = x.shape
    depth = norm_w.shape[0]
    bf = lambda a: a.astype(_BF16)

    inv_freq = 1.0 / (ROPE_THETA ** (jnp.arange(0, HEAD_DIM, 2, dtype=_F32) / HEAD_DIM))
    ang = positions.astype(_F32)[..., None] * inv_freq
    cos = jnp.concatenate([jnp.cos(ang), jnp.cos(ang)], axis=-1)
    sin = jnp.concatenate([-jnp.sin(ang), jnp.sin(ang)], axis=-1)

    nw = norm_w.reshape(depth, 6, 1, d)
    h = x.reshape(batch * seq, d)

    w_gate, w_up, w_down = bf(ffn_w_gate), bf(ffn_w_up), bf(ffn_w_down)
    w_ev_in, w_ev_out = bf(ev_w_in), bf(ev_w_out)
    w_od_in, w_od_out = bf(jax.vmap(_odd_in_weight)(od_w_in)), bf(od_w_out)

    def ffn(h, layer, which):
        a = _ffn_up(h, nw[layer, 4 * which], w_gate, w_up, (layer, which))
        return _ffn_down(a, w_down, (layer, which), h, nw[layer, 4 * which + 1], HALF_STEP)

    for layer in range(depth):
        i = layer // 2
        h = ffn(h, layer, 0)
        if layer % 2 == 0:
            slabs = _norm_proj(h, nw[layer, 2], w_ev_in, (i,), batch, seq)
            o_a = _dilated_attention(slabs, cos, sin)
            o_b = _multiscale_pool(slabs, bf(pool_w[i]), pool_scale[i].reshape(1, -1))
            h = _mix_out(o_a.reshape(batch * seq, -1), o_b.reshape(batch * seq, -1), w_ev_out, (i,), h,
                         nw[layer, 3])
        else:
            slabs = _norm_proj(h, nw[layer, 2], w_od_in, (i,), batch, seq)
            kv_cmp = _compress(slabs, jnp.stack([cmp_pe_k[i], cmp_pe_v[i]]),
                               bf(jnp.stack([cmp_w1_k[i], cmp_w1_v[i]])),
                               bf(jnp.stack([cmp_w2_k[i], cmp_w2_v[i]])), C_HEADS)
            o_cmp, sel = _cmp_select(slabs, kv_cmp)
            o_c = _nsa_attention(slabs, sel, o_cmp, cos, sin)
            y_d = _short_conv(slabs, conv_w[i], 26)
            h = _mix_out(o_c.reshape(batch * seq, -1), y_d.reshape(batch * seq, -1), w_od_out, (i,), h,
                         nw[layer, 3])
        h = ffn(h, layer, 1)
    return h.reshape(batch, seq, d)
```

```python
import functools

import jax
import jax.numpy as jnp
from jax import lax
from jax.experimental import pallas as pl
from jax.experimental.pallas import tpu as pltpu

HEAD_DIM = 128
ROPE_THETA = 10000.0
NORM_EPS = 1e-6
HALF_STEP = 0.5
LANE = 128

A_HEADS = 12
A_DILATIONS = (1, 4, 16)
A_BACK = 128
B_WINDOWS = (2, 4, 8, 16)
B_GROUPS = 4

C_HEADS = 12
C_KV_HEADS = 2
C_GROUP = C_HEADS // C_KV_HEADS
CMP_BLOCK = 32
CMP_STRIDE = 16
SLC_BLOCK = 64
SLC_TOPN = 8
WIN_SIZE = 512
FORCED_SCORE = 1e9
D_WIDTH = 512
PROJ_SLABS = 40
NORM_SUB_TILES = 4

VMEM_LIMIT = 56 * 1024 * 1024

_BF16 = jnp.bfloat16
_F32 = jnp.float32


def _params(*sem):
    return pltpu.CompilerParams(dimension_semantics=sem, vmem_limit_bytes=VMEM_LIMIT)


def _rms(x, w):
    return x * lax.rsqrt(jnp.mean(x * x, axis=-1, keepdims=True) + NORM_EPS) * w


def _dot(a, b):
    return jnp.dot(a, b, preferred_element_type=_F32)


def _dot_t(a, b):
    return lax.dot_general(a, b, (((1,), (1,)), ((), ())), preferred_element_type=_F32)


def _rope(x, cos, sin):
    return x * cos + pltpu.roll(x, HEAD_DIM // 2, 1) * sin


def _shift_rows(x, k, row):
    return jnp.where(row >= k, pltpu.roll(x, k, 0), 0.0)


def _proj_kernel(h_ref, nw_ref, w_ref, o_ref, xn_ref):
    def use(xn, r):
        acc = _dot(xn, w_ref[...])
        for s in range(o_ref.shape[1]):
            o_ref[0, s, r, :] = acc[:, s * LANE:(s + 1) * LANE]

    _normed_rows(h_ref, nw_ref, xn_ref, use)


def _stacked(block, index_map, lead):
    return pl.BlockSpec((None,) * len(lead) + block, lambda *g: tuple(lead) + index_map(*g))


def _norm_proj(h, nw, w, lead, batch, seq, tm=1024, tn=1024):
    t, d = h.shape
    n = w.shape[-1]
    per_b = seq // tm
    return pl.pallas_call(
        _proj_kernel,
        grid=(t // tm, n // tn),
        in_specs=[pl.BlockSpec((tm, d), lambda i, j: (i, 0)),
                  pl.BlockSpec((1, d), lambda i, j: (0, 0)),
                  _stacked((d, tn), lambda i, j: (0, j), lead)],
        out_specs=pl.BlockSpec((1, tn // LANE, tm, LANE), lambda i, j: (i // per_b, j, i % per_b, 0)),
        out_shape=jax.ShapeDtypeStruct((batch, n // LANE, seq, LANE), _F32),
        scratch_shapes=[pltpu.VMEM((tm, d), _BF16)],
        compiler_params=_params("arbitrary", "arbitrary"),
        name="norm_proj",
    )(h, nw, w)


def _normed_rows(h_ref, nw_ref, xn_ref, use):
    j = pl.program_id(1)

    @pl.when(j == 0)
    def _():
        sub = h_ref.shape[0] // NORM_SUB_TILES
        for s in range(NORM_SUB_TILES):
            r = slice(s * sub, (s + 1) * sub)
            xn = _rms(h_ref[r, :], nw_ref[...]).astype(_BF16)
            xn_ref[r, :] = xn
            use(xn, r)

    @pl.when(j > 0)
    def _():
        use(xn_ref[...], slice(None))


def _ffn_up_kernel(h_ref, nw_ref, wg_ref, wu_ref, o_ref, xn_ref):
    def use(xn, r):
        g = _dot(xn, wg_ref[...])
        u = _dot(xn, wu_ref[...])
        o_ref[r, :] = (jax.nn.silu(g) * u).astype(_BF16)

    _normed_rows(h_ref, nw_ref, xn_ref, use)


def _ffn_up(h, nw, wg, wu, lead, tm=1024, tn=512):
    t, d = h.shape
    f = wg.shape[-1]
    return pl.pallas_call(
        _ffn_up_kernel,
        grid=(t // tm, f // tn),
        in_specs=[pl.BlockSpec((tm, d), lambda i, j: (i, 0)),
                  pl.BlockSpec((1, d), lambda i, j: (0, 0)),
                  _stacked((d, tn), lambda i, j: (0, j), lead),
                  _stacked((d, tn), lambda i, j: (0, j), lead)],
        out_specs=pl.BlockSpec((tm, tn), lambda i, j: (i, j)),
        out_shape=jax.ShapeDtypeStruct((t, f), _BF16),
        scratch_shapes=[pltpu.VMEM((tm, d), _BF16)],
        compiler_params=_params("arbitrary", "arbitrary"),
        name="ffn_up",
    )(h, nw, wg, wu)


def _ffn_down_kernel(a_ref, w_ref, h_ref, nw_ref, o_ref, acc_ref, *, scale, n_sub):
    k = pl.program_id(1)
    last = pl.num_programs(1) - 1

    @pl.when(k == 0)
    def _():
        acc_ref[...] = _dot(a_ref[...], w_ref[...])

    @pl.when((k > 0) & (k < last))
    def _():
        acc_ref[...] += _dot(a_ref[...], w_ref[...])

    @pl.when(k == last)
    def _():
        sub = a_ref.shape[0] // n_sub
        for s in range(n_sub):
            r = slice(s * sub, (s + 1) * sub)
            f = acc_ref[r, :] + _dot(a_ref[r, :], w_ref[...])
            o_ref[r, :] = h_ref[r, :] + scale * _rms(f, nw_ref[...])


def _ffn_down(a, w, lead, h, nw, scale, tm=512, tk=2816, n_sub=2):
    t, f = a.shape
    d = w.shape[-1]
    assert f // tk >= 2, "the first and last contraction steps must be different grid steps"
    return pl.pallas_call(
        functools.partial(_ffn_down_kernel, scale=scale, n_sub=n_sub),
        grid=(t // tm, f // tk),
        in_specs=[pl.BlockSpec((tm, tk), lambda i, k: (i, k)),
                  _stacked((tk, d), lambda i, k: (k, 0), lead),
                  pl.BlockSpec((tm, d), lambda i, k: (i, 0)),
                  pl.BlockSpec((1, d), lambda i, k: (0, 0))],
        out_specs=pl.BlockSpec((tm, d), lambda i, k: (i, 0)),
        out_shape=jax.ShapeDtypeStruct((t, d), _F32),
        scratch_shapes=[pltpu.VMEM((tm, d), _F32)],
        compiler_params=_params("arbitrary", "arbitrary"),
        name="ffn_down",
    )(a, w, h, nw)


def _mix_out_kernel(x1_ref, x2_ref, w_ref, h_ref, nw_ref, o_ref):
    c1 = x1_ref.shape[1]
    m = _dot(x1_ref[...], w_ref[0:c1, :]) + _dot(x2_ref[...], w_ref[c1:, :])
    o_ref[...] = h_ref[...] + _rms(m, nw_ref[...])


def _mix_out(x1, x2, w, lead, h, nw, tm=512):
    t, c1 = x1.shape
    c2 = x2.shape[1]
    d = w.shape[-1]
    return pl.pallas_call(
        _mix_out_kernel,
        grid=(t // tm,),
        in_specs=[pl.BlockSpec((tm, c1), lambda i: (i, 0)),
                  pl.BlockSpec((tm, c2), lambda i: (i, 0)),
                  _stacked((c1 + c2, d), lambda i: (0, 0), lead),
                  pl.BlockSpec((tm, d), lambda i: (i, 0)),
                  pl.BlockSpec((1, d), lambda i: (0, 0))],
        out_specs=pl.BlockSpec((tm, d), lambda i: (i, 0)),
        out_shape=jax.ShapeDtypeStruct((t, d), _F32),
        compiler_params=_params("arbitrary"),
        name="mix_out",
    )(x1, x2, w, h, nw)


def _rows(start, size, stride):
    return pl.ds(start, size) if stride == 1 else pl.ds(start, size, stride=stride)


def _dilated_kernel(q_ref, k_ref, v_ref, cos_ref, sin_ref, o_ref, qs, ks, pv_s, m_s, l_s):
    seq = q_ref.shape[2]
    blk = A_BACK
    scale = HEAD_DIM ** -0.5
    chunk = 256

    for c in range(0, seq, chunk):
        r = slice(c, c + chunk)
        cos, sin = cos_ref[0, r, :], sin_ref[0, r, :]
        qs[r, :] = _rope(q_ref[0, 0, r, :], cos, sin) * scale
        ks[r, :] = _rope(k_ref[0, 0, r, :], cos, sin)

    qi = lax.broadcasted_iota(jnp.int32, (blk, blk), 0)
    kj = lax.broadcasted_iota(jnp.int32, (blk, blk), 1)
    cur_ok = kj <= qi
    prev_ok = kj >= qi
    ones = jnp.ones((blk, HEAD_DIM), _BF16)

    for g, dil in enumerate(A_DILATIONS):
        for res in range(dil):
            k_prev = v_prev = None
            for n in range(seq // (blk * dil)):
                rows = _rows(res + n * blk * dil, blk, dil)
                qb = qs[rows, :].astype(_BF16)
                k_cur = ks[rows, :].astype(_BF16)
                v_cur = jnp.concatenate([v_ref[0, 0, rows, :].astype(_BF16), ones], axis=1)
                s_cur = jnp.where(cur_ok, _dot_t(qb, k_cur), -jnp.inf)
                if n == 0:
                    m = jnp.max(s_cur, axis=-1, keepdims=True)
                    pva = _dot(jnp.exp(s_cur - m).astype(_BF16), v_cur)
                else:
                    s_prev = jnp.where(prev_ok, _dot_t(qb, k_prev), -jnp.inf)
                    m = jnp.max(jnp.maximum(s_cur, s_prev), axis=-1, keepdims=True)
                    pva = (_dot(jnp.exp(s_cur - m).astype(_BF16), v_cur)
                           + _dot(jnp.exp(s_prev - m).astype(_BF16), v_prev))
                pv_s[g, rows, :] = pva[:, 0:HEAD_DIM]
                l_s[g, rows, :] = pva[:, HEAD_DIM:]
                m_s[g, rows, :] = jnp.broadcast_to(m, (blk, HEAD_DIM))
                k_prev, v_prev = k_cur, v_cur

    for c in range(0, seq, chunk):
        r = slice(c, c + chunk)
        m_all = jnp.maximum(jnp.maximum(m_s[0, r, :], m_s[1, r, :]), m_s[2, r, :])
        num = jnp.zeros((chunk, HEAD_DIM), _F32)
        den = jnp.zeros((chunk, HEAD_DIM), _F32)
        for g in range(len(A_DILATIONS)):
            w = jnp.exp(m_s[g, r, :] - m_all)
            num = num + w * pv_s[g, r, :]
            den = den + w * l_s[g, r, :]
        o_ref[0, r, :] = (num / den).astype(o_ref.dtype)


def _dilated_attention(slabs, cos, sin):
    batch, _, seq, _ = slabs.shape
    head = lambda off: pl.BlockSpec((1, 1, seq, LANE), lambda b, h: (b, off + h, 0, 0))
    table = pl.BlockSpec((1, seq, LANE), lambda b, h: (b, 0, 0))
    return pl.pallas_call(
        _dilated_kernel,
        grid=(batch, A_HEADS),
        in_specs=[head(0), head(A_HEADS), head(2 * A_HEADS), table, table],
        out_specs=pl.BlockSpec((1, seq, LANE), lambda b, h: (b, 0, h)),
        out_shape=jax.ShapeDtypeStruct((batch, seq, A_HEADS * HEAD_DIM), _BF16),
        scratch_shapes=[pltpu.VMEM((seq, HEAD_DIM), _F32), pltpu.VMEM((seq, HEAD_DIM), _F32),
                        pltpu.VMEM((3, seq, HEAD_DIM), _F32), pltpu.VMEM((3, seq, HEAD_DIM), _F32),
                        pltpu.VMEM((3, seq, HEAD_DIM), _F32)],
        compiler_params=_params("arbitrary", "arbitrary"),
        name="dilated_attention",
    )(slabs, slabs, slabs, cos, sin)


def _pool_kernel(u_ref, w_ref, sc_ref, o_ref):
    seq = u_ref.shape[2]
    row = lax.broadcasted_iota(jnp.int32, (seq, LANE), 0)
    for g, win in enumerate(B_WINDOWS):
        x = u_ref[0, g]
        s = x
        step = 1
        while step < win:
            s = s + _shift_rows(s, step, row)
            step *= 2
        cnt = jnp.minimum(row + 1, win).astype(_F32)
        pooled = s / cnt - x
        mixed = _dot(pooled.astype(_BF16), w_ref[g]) * sc_ref[:, g * LANE:(g + 1) * LANE]
        o_ref[0, :, g * LANE:(g + 1) * LANE] = mixed.astype(o_ref.dtype)


def _multiscale_pool(slabs, pool_w, pool_scale):
    batch, _, seq, _ = slabs.shape
    first = 3 * A_HEADS // B_GROUPS
    return pl.pallas_call(
        _pool_kernel,
        grid=(batch,),
        in_specs=[pl.BlockSpec((1, B_GROUPS, seq, LANE), lambda b: (b, first, 0, 0)),
                  pl.BlockSpec((B_GROUPS, LANE, LANE), lambda b: (0, 0, 0)),
                  pl.BlockSpec((1, B_GROUPS * LANE), lambda b: (0, 0))],
        out_specs=pl.BlockSpec((1, seq, B_GROUPS * LANE), lambda b: (b, 0, 0)),
        out_shape=jax.ShapeDtypeStruct((batch, seq, B_GROUPS * LANE), _BF16),
        compiler_params=_params("arbitrary"),
        name="multiscale_pool",
    )(slabs, pool_w, pool_scale)


def _compress_kernel(x_ref, pe_ref, w1_ref, w2_ref, o_ref):
    nrow = o_ref.shape[3]
    hidden = w1_ref.shape[2]
    a = jnp.zeros((nrow, hidden), _F32)
    b = jnp.zeros((nrow, hidden), _F32)
    for tok in range(CMP_STRIDE):
        x = x_ref[0, 0, pl.ds(tok, nrow, stride=CMP_STRIDE), :]
        lo, hi = tok, CMP_STRIDE + tok
        a = a + _dot((x + pe_ref[0, lo:lo + 1, :]).astype(_BF16), w1_ref[0, lo * HEAD_DIM:(lo + 1) * HEAD_DIM, :])
        b = b + _dot((x + pe_ref[0, hi:hi + 1, :]).astype(_BF16), w1_ref[0, hi * HEAD_DIM:(hi + 1) * HEAD_DIM, :])
    hid = a + pltpu.roll(b, nrow - 1, 0)
    o_ref[0, 0, 0] = _dot(jax.nn.gelu(hid).astype(_BF16), w2_ref[0])


def _compress(slabs, pe, w1, w2, first_slab):
    batch, _, seq, _ = slabs.shape
    nrow = seq // CMP_STRIDE
    return pl.pallas_call(
        _compress_kernel,
        grid=(2, batch, C_KV_HEADS),
        in_specs=[pl.BlockSpec((1, 1, seq, LANE), lambda kv, b, g: (b, first_slab + 2 * kv + g, 0, 0)),
                  pl.BlockSpec((1, CMP_BLOCK, HEAD_DIM), lambda kv, b, g: (kv, 0, 0)),
                  pl.BlockSpec((1, CMP_BLOCK * HEAD_DIM, w1.shape[2]), lambda kv, b, g: (kv, 0, 0)),
                  pl.BlockSpec((1, w2.shape[1], HEAD_DIM), lambda kv, b, g: (kv, 0, 0))],
        out_specs=pl.BlockSpec((1, 1, 1, nrow, HEAD_DIM), lambda kv, b, g: (kv, b, g, 0, 0)),
        out_shape=jax.ShapeDtypeStruct((2, batch, C_KV_HEADS, nrow, HEAD_DIM), _F32),
        compiler_params=_params("arbitrary", "arbitrary", "arbitrary"),
        name="compress",
    )(slabs, pe, w1, w2)


def _cmp_select_kernel(q_ref, kc_ref, vc_ref, o_ref, sel_ref, *, nslc):
    tq = q_ref.shape[2]
    ncmp = kc_ref.shape[3]
    scale = HEAD_DIM ** -0.5
    t = pl.program_id(2) * tq + lax.broadcasted_iota(jnp.int32, (tq, ncmp), 0)
    n = lax.broadcasted_iota(jnp.int32, (tq, ncmp), 1)
    cmask = n * CMP_STRIDE + (CMP_BLOCK - 1) <= t

    cj = lax.broadcasted_iota(jnp.int32, (LANE, ncmp), 0) * SLC_BLOCK
    cn = lax.broadcasted_iota(jnp.int32, (LANE, ncmp), 1) * CMP_STRIDE
    cover_t = ((cn < cj + SLC_BLOCK) & (cn + CMP_BLOCK > cj)).astype(_BF16)

    kc = kc_ref[0, 0, 0].astype(_BF16)
    vc = vc_ref[0, 0, 0].astype(_BF16)
    imp_t = jnp.zeros((LANE, tq), _F32)
    for hh in range(C_GROUP):
        sc = jnp.where(cmask, _dot_t(q_ref[0, hh].astype(_BF16), kc) * scale, -jnp.inf)
        m = jnp.max(sc, axis=-1, keepdims=True)
        m = jnp.where(jnp.isfinite(m), m, 0.0)
        p = jnp.exp(sc - m)
        den = jnp.sum(p, axis=-1, keepdims=True)
        p = (p / jnp.maximum(den, 1.0)).astype(_BF16)
        o_ref[0, hh] = _dot(p, vc)
        imp_t = imp_t + _dot_t(cover_t, p)

    j = lax.broadcasted_iota(jnp.int32, (nslc, tq), 0)
    tt = pl.program_id(2) * tq + lax.broadcasted_iota(jnp.int32, (nslc, tq), 1)
    cur = lax.shift_right_logical(tt, SLC_BLOCK.bit_length() - 1)
    visible = j <= cur
    forced = visible & ((j == 0) | (j >= cur - 1))
    score = jnp.where(forced, FORCED_SCORE, jnp.where(visible, imp_t[0:nslc, :], -FORCED_SCORE))
    jf = j.astype(_F32)
    sel_t = jnp.zeros((nslc, tq), _F32)
    for _ in range(min(SLC_TOPN, nslc)):
        best = jnp.max(score, axis=0, keepdims=True)
        pick = jnp.min(jnp.where(score == best, jf, float(nslc)), axis=0, keepdims=True)
        hit = jf == pick
        sel_t = jnp.where(hit, 1.0, sel_t)
        score = jnp.where(hit, -jnp.inf, score)
    sel_t = jnp.concatenate([sel_t, jnp.zeros((LANE - nslc, tq), _F32)], axis=0)
    sel_ref[0, 0] = sel_t.T.astype(sel_ref.dtype)


def _cmp_select(slabs, kv_cmp, tq=256):
    batch, _, seq, _ = slabs.shape
    cmp_spec = lambda which: pl.BlockSpec((1, 1, 1) + kv_cmp.shape[3:], lambda b, g, i: (which, b, g, 0, 0))
    return pl.pallas_call(
        functools.partial(_cmp_select_kernel, nslc=seq // SLC_BLOCK),
        grid=(batch, C_KV_HEADS, seq // tq),
        in_specs=[pl.BlockSpec((1, C_GROUP, tq, LANE), lambda b, g, i: (b, g, i, 0)),
                  cmp_spec(0), cmp_spec(1)],
        out_specs=[pl.BlockSpec((1, C_GROUP, tq, LANE), lambda b, g, i: (b, g, i, 0)),
                   pl.BlockSpec((1, 1, tq, LANE), lambda b, g, i: (b, g, i, 0))],
        out_shape=[jax.ShapeDtypeStruct((batch, C_HEADS, seq, HEAD_DIM), _F32),
                   jax.ShapeDtypeStruct((batch, C_KV_HEADS, seq, LANE), _BF16)],
        compiler_params=_params("arbitrary", "arbitrary", "arbitrary"),
        name="cmp_select",
    )(slabs, kv_cmp, kv_cmp)


def _group_softmax_pv(q, k, v1, bias_s, s_s, p_s, o_s):
    nk = k.shape[0]
    tq = bias_s.shape[0]
    piece = 512 if nk % 512 == 0 else LANE
    pieces = [slice(c, c + piece) for c in range(0, nk, piece)]
    s_s[:, 0:nk] = _dot_t(q, k)
    for hh in range(C_GROUP):
        r = slice(hh * tq, (hh + 1) * tq)
        m = None
        for c in pieces:
            mc = jnp.max(s_s[r, c] + bias_s[:, c], axis=-1, keepdims=True)
            m = mc if m is None else jnp.maximum(m, mc)
        for c in pieces:
            p_s[r, c] = jnp.exp(s_s[r, c] + bias_s[:, c] - m).astype(_BF16)
    o = _dot(p_s[:, 0:nk], v1)
    o_s[...] = o[:, 0:HEAD_DIM] / o[:, HEAD_DIM:]


def _nsa_kernel(q_ref, ks_ref, vs_ref, kw_ref, vw_ref, sel_ref, oc_ref, gl_ref, cos_ref, sin_ref, o_ref,
                ksr, vs1, kwr, vw1, expand_s, qr, bias_s, s_s, p_s, bias_w, s_w, p_w, oslc_s, owin_s):
    tq = q_ref.shape[2]
    seq = ks_ref.shape[2]
    qi = pl.program_id(2)
    chunk = 256
    shift = SLC_BLOCK.bit_length() - 1

    @pl.when(qi == 0)
    def _():
        def prep(c, _):
            r = pl.ds(pl.multiple_of(c * chunk, chunk), chunk)
            cos, sin = cos_ref[0, r, :], sin_ref[0, r, :]
            ones = jnp.ones((chunk, HEAD_DIM), _BF16)
            ksr[r, :] = _rope(ks_ref[0, 0, r, :], cos, sin).astype(_BF16)
            kwr[r, :] = _rope(kw_ref[0, 0, r, :], cos, sin).astype(_BF16)
            vs1[r, 0:HEAD_DIM] = vs_ref[0, 0, r, :].astype(_BF16)
            vs1[r, HEAD_DIM:] = ones
            vw1[r, 0:HEAD_DIM] = vw_ref[0, 0, r, :].astype(_BF16)
            vw1[r, HEAD_DIM:] = ones
            return 0

        lax.fori_loop(0, seq // chunk, prep, 0)
        blk = lax.broadcasted_iota(jnp.int32, (LANE, seq), 0)
        key = lax.broadcasted_iota(jnp.int32, (LANE, seq), 1)
        expand_s[...] = (blk == lax.shift_right_logical(key, shift)).astype(_BF16)

    q0 = pl.multiple_of(qi * tq, tq)
    cos_q, sin_q = cos_ref[0, pl.ds(q0, tq), :], sin_ref[0, pl.ds(q0, tq), :]
    for hh in range(C_GROUP):
        qr[hh * tq:(hh + 1) * tq, :] = (_rope(q_ref[0, hh], cos_q, sin_q) * HEAD_DIM ** -0.5).astype(_BF16)
    q = qr[...]

    def window_branch():
        span = WIN_SIZE + tq
        k0 = pl.multiple_of(jnp.maximum(q0 - WIN_SIZE, 0), tq)
        t = q0 + lax.broadcasted_iota(jnp.int32, (tq, span), 0)
        key = k0 + lax.broadcasted_iota(jnp.int32, (tq, span), 1)
        bias_w[...] = jnp.where((key <= t) & (key > t - WIN_SIZE), 0.0, -jnp.inf)
        _group_softmax_pv(q, kwr[pl.ds(k0, span), :], vw1[pl.ds(k0, span), :], bias_w, s_w, p_w, owin_s)

    n_extent = 4
    step = seq // n_extent
    for nt in range(1, n_extent + 1):
        @pl.when(q0 // step + 1 == nt)
        def _(nk=nt * step):
            for c in range(0, nk, step):
                t = q0 + lax.broadcasted_iota(jnp.int32, (tq, step), 0)
                key = c + lax.broadcasted_iota(jnp.int32, (tq, step), 1)
                ok = (_dot(sel_ref[0, 0], expand_s[:, c:c + step]) > 0.5) & (key <= t)
                bias_s[:, c:c + step] = jnp.where(ok, 0.0, -jnp.inf)
            _group_softmax_pv(q, ksr[0:nk, :], vs1[0:nk, :], bias_s, s_s, p_s, oslc_s)
            window_branch()

    gates = jax.nn.sigmoid(gl_ref[0, 0])
    for hh in range(C_GROUP):
        r = slice(hh * tq, (hh + 1) * tq)
        g_cmp, g_slc, g_win = (gates[:, 3 * hh + c:3 * hh + c + 1] for c in range(3))
        o = g_cmp * oc_ref[0, hh] + g_slc * oslc_s[r, :] + g_win * owin_s[r, :]
        o_ref[0, :, hh * HEAD_DIM:(hh + 1) * HEAD_DIM] = o.astype(o_ref.dtype)


def _nsa_attention(slabs, sel, o_cmp, cos, sin, tq=128):
    batch, _, seq, _ = slabs.shape
    kv = lambda off: pl.BlockSpec((1, 1, seq, LANE), lambda b, g, i: (b, off + g, 0, 0))
    table = pl.BlockSpec((1, seq, LANE), lambda b, g, i: (b, 0, 0))
    rows = C_GROUP * tq
    span = WIN_SIZE + tq
    return pl.pallas_call(
        _nsa_kernel,
        grid=(batch, C_KV_HEADS, seq // tq),
        in_specs=[pl.BlockSpec((1, C_GROUP, tq, LANE), lambda b, g, i: (b, g, i, 0)),
                  kv(16), kv(18), kv(20), kv(22),
                  pl.BlockSpec((1, 1, tq, LANE), lambda b, g, i: (b, g, i, 0)),
                  pl.BlockSpec((1, C_GROUP, tq, LANE), lambda b, g, i: (b, g, i, 0)),
                  pl.BlockSpec((1, 1, tq, LANE), lambda b, g, i: (b, 24 + g, i, 0)),
                  table, table],
        out_specs=pl.BlockSpec((1, tq, C_GROUP * HEAD_DIM), lambda b, g, i: (b, i, g)),
        out_shape=jax.ShapeDtypeStruct((batch, seq, C_HEADS * HEAD_DIM), _BF16),
        scratch_shapes=[pltpu.VMEM((seq, HEAD_DIM), _BF16), pltpu.VMEM((seq, 2 * HEAD_DIM), _BF16)] * 2
        + [pltpu.VMEM((LANE, seq), _BF16), pltpu.VMEM((rows, HEAD_DIM), _BF16),
           pltpu.VMEM((tq, seq), _F32), pltpu.VMEM((rows, seq), _F32), pltpu.VMEM((rows, seq), _BF16),
           pltpu.VMEM((tq, span), _F32), pltpu.VMEM((rows, span), _F32), pltpu.VMEM((rows, span), _BF16)]
        + [pltpu.VMEM((rows, HEAD_DIM), _F32)] * 2,
        compiler_params=_params("arbitrary", "arbitrary", "arbitrary"),
        name="nsa_attention",
    )(slabs, slabs, slabs, slabs, slabs, sel, o_cmp, slabs, cos, sin)


def _conv_kernel(u_ref, c_ref, b_ref, w_ref, o_ref):
    seq = u_ref.shape[2]
    row = lax.broadcasted_iota(jnp.int32, (seq, LANE), 0)
    u = c_ref[0, 0] * u_ref[0, 0]
    taps = w_ref.shape[0]
    conv = _shift_rows(u, taps - 1, row) * w_ref[0:1, :]
    for j in range(1, taps):
        shifted = u if j == taps - 1 else _shift_rows(u, taps - 1 - j, row)
        conv = conv + shifted * w_ref[j:j + 1, :]
    o_ref[0] = (b_ref[0, 0] * conv).astype(o_ref.dtype)


def _short_conv(slabs, conv_w, first_slab):
    batch, _, seq, _ = slabs.shape
    nd = D_WIDTH // LANE
    part = lambda off: pl.BlockSpec((1, 1, seq, LANE), lambda b, j: (b, first_slab + off + j, 0, 0))
    return pl.pallas_call(
        _conv_kernel,
        grid=(batch, nd),
        in_specs=[part(0), part(nd), part(2 * nd),
                  pl.BlockSpec((conv_w.shape[0], LANE), lambda b, j: (0, j))],
        out_specs=pl.BlockSpec((1, seq, LANE), lambda b, j: (b, 0, j)),
        out_shape=jax.ShapeDtypeStruct((batch, seq, D_WIDTH), _BF16),
        compiler_params=_params("arbitrary", "arbitrary"),
        name="short_conv",
    )(slabs, slabs, slabs, conv_w)


def _pad_cols(w, total):
    return jnp.pad(w, ((0, 0), (0, total - w.shape[1])))


def _odd_in_weight(w):
    qkv = C_HEADS * HEAD_DIM + 6 * C_KV_HEADS * HEAD_DIM
    ngate = 3 * C_GROUP
    gates = [_pad_cols(w[:, qkv + g * ngate:qkv + (g + 1) * ngate], LANE) for g in range(C_KV_HEADS)]
    rest = w[:, qkv + C_KV_HEADS * ngate:]
    return _pad_cols(jnp.concatenate([w[:, :qkv]] + gates + [rest], axis=1), PROJ_SLABS * LANE)


def kernel(x, positions, norm_w, ffn_w_gate, ffn_w_up, ffn_w_down, ev_w_in, ev_w_out, pool_w, pool_scale,
           od_w_in, od_w_out, cmp_pe_k, cmp_w1_k, cmp_w2_k, cmp_pe_v, cmp_w1_v, cmp_w2_v, conv_w):
    batch, seq, d = x.shape
    depth = norm_w.shape[0]
    bf = lambda a: a.astype(_BF16)

    inv_freq = 1.0 / (ROPE_THETA ** (jnp.arange(0, HEAD_DIM, 2, dtype=_F32) / HEAD_DIM))
    ang = positions.astype(_F32)[..., None] * inv_freq
    cos = jnp.concatenate([jnp.cos(ang), jnp.cos(ang)], axis=-1)
    sin = jnp.concatenate([-jnp.sin(ang), jnp.sin(ang)], axis=-1)

    nw = norm_w.reshape(depth, 6, 1, d)
    h = x.reshape(batch * seq, d)

    w_gate, w_up, w_down = bf(ffn_w_gate), bf(ffn_w_up), bf(ffn_w_down)
    w_ev_in, w_ev_out = bf(ev_w_in), bf(ev_w_out)
    w_od_in, w_od_out = bf(jax.vmap(_odd_in_weight)(od_w_in)), bf(od_w_out)

    def ffn(h, layer, which):
        a = _ffn_up(h, nw[layer, 4 * which], w_gate, w_up, (layer, which))
        return _ffn_down(a, w_down, (layer, which), h, nw[layer, 4 * which + 1], HALF_STEP)

    for layer in range(depth):
        i = layer // 2
        h = ffn(h, layer, 0)
        if layer % 2 == 0:
            slabs = _norm_proj(h, nw[layer, 2], w_ev_in, (i,), batch, seq)
            o_a = _dilated_attention(slabs, cos, sin)
            o_b = _multiscale_pool(slabs, bf(pool_w[i]), pool_scale[i].reshape(1, -1))
            h = _mix_out(o_a.reshape(batch * seq, -1), o_b.reshape(batch * seq, -1), w_ev_out, (i,), h,
                         nw[layer, 3])
        else:
            slabs = _norm_proj(h, nw[layer, 2], w_od_in, (i,), batch, seq)
            kv_cmp = _compress(slabs, jnp.stack([cmp_pe_k[i], cmp_pe_v[i]]),
                               bf(jnp.stack([cmp_w1_k[i], cmp_w1_v[i]])),
                               bf(jnp.stack([cmp_w2_k[i], cmp_w2_v[i]])), C_HEADS)
            o_cmp, sel = _cmp_select(slabs, kv_cmp)
            o_c = _nsa_attention(slabs, sel, o_cmp, cos, sin)
            y_d = _short_conv(slabs, conv_w[i], 26)
            h = _mix_out(o_c.reshape(batch * seq, -1), y_d.reshape(batch * seq, -1), w_od_out, (i,), h,
                         nw[layer, 3])
        h = ffn(h, layer, 1)
    return h.reshape(batch, seq, d)
```

```python
import functools

import jax
import jax.numpy as jnp
from jax import lax
from jax.experimental import pallas as pl
from jax.experimental.pallas import tpu as pltpu

HEAD_DIM = 128
ROPE_THETA = 10000.0
NORM_EPS = 1e-6
HALF_STEP = 0.5
LANE = 128

A_HEADS = 12
A_DILATIONS = (1, 4, 16)
A_BACK = 128
B_WINDOWS = (2, 4, 8, 16)
B_GROUPS = 4

C_HEADS = 12
C_KV_HEADS = 2
C_GROUP = C_HEADS // C_KV_HEADS
CMP_BLOCK = 32
CMP_STRIDE = 16
SLC_BLOCK = 64
SLC_TOPN = 8
WIN_SIZE = 512
FORCED_SCORE = 1e9
D_WIDTH = 512
PROJ_SLABS = 40
NORM_SUB_TILES = 4

VMEM_LIMIT = 56 * 1024 * 1024

_BF16 = jnp.bfloat16
_F32 = jnp.float32


def _params(*sem):
    return pltpu.CompilerParams(dimension_semantics=sem, vmem_limit_bytes=VMEM_LIMIT)


def _rms(x, w):
    return x * lax.rsqrt(jnp.mean(x * x, axis=-1, keepdims=True) + NORM_EPS) * w


def _dot(a, b):
    return jnp.dot(a, b, preferred_element_type=_F32)


def _dot_t(a, b):
    return lax.dot_general(a, b, (((1,), (1,)), ((), ())), preferred_element_type=_F32)


def _rope(x, cos, sin):
    return x * cos + pltpu.roll(x, HEAD_DIM // 2, 1) * sin


def _shift_rows(x, k, row):
    return jnp.where(row >= k, pltpu.roll(x, k, 0), 0.0)


def _proj_kernel(h_ref, nw_ref, w_ref, o_ref, xn_ref):
    def use(xn, r):
        acc = _dot(xn, w_ref[...])
        for s in range(o_ref.shape[1]):
            o_ref[0, s, r, :] = acc[:, s * LANE:(s + 1) * LANE]

    _normed_rows(h_ref, nw_ref, xn_ref, use)


def _stacked(block, index_map, lead):
    return pl.BlockSpec((None,) * len(lead) + block, lambda *g: tuple(lead) + index_map(*g))


def _norm_proj(h, nw, w, lead, batch, seq, tm=1024, tn=1024):
    t, d = h.shape
    n = w.shape[-1]
    per_b = seq // tm
    return pl.pallas_call(
        _proj_kernel,
        grid=(t // tm, n // tn),
        in_specs=[pl.BlockSpec((tm, d), lambda i, j: (i, 0)),
                  pl.BlockSpec((1, d), lambda i, j: (0, 0)),
                  _stacked((d, tn), lambda i, j: (0, j), lead)],
        out_specs=pl.BlockSpec((1, tn // LANE, tm, LANE), lambda i, j: (i // per_b, j, i % per_b, 0)),
        out_shape=jax.ShapeDtypeStruct((batch, n // LANE, seq, LANE), _F32),
        scratch_shapes=[pltpu.VMEM((tm, d), _BF16)],
        compiler_params=_params("arbitrary", "arbitrary"),
        name="norm_proj",
    )(h, nw, w)


def _normed_rows(h_ref, nw_ref, xn_ref, use):
    j = pl.program_id(1)

    @pl.when(j == 0)
    def _():
        sub = h_ref.shape[0] // NORM_SUB_TILES
        for s in range(NORM_SUB_TILES):
            r = slice(s * sub, (s + 1) * sub)
            xn = _rms(h_ref[r, :], nw_ref[...]).astype(_BF16)
            xn_ref[r, :] = xn
            use(xn, r)

    @pl.when(j > 0)
    def _():
        use(xn_ref[...], slice(None))


def _ffn_up_kernel(h_ref, nw_ref, wg_ref, wu_ref, o_ref, xn_ref):
    def use(xn, r):
        g = _dot(xn, wg_ref[...].astype(_BF16))
        u = _dot(xn, wu_ref[...].astype(_BF16))
        o_ref[r, :] = (jax.nn.silu(g) * u).astype(_BF16)

    _normed_rows(h_ref, nw_ref, xn_ref, use)


def _ffn_up(h, nw, wg, wu, lead, tm=1024, tn=512):
    t, d = h.shape
    f = wg.shape[-1]
    return pl.pallas_call(
        _ffn_up_kernel,
        grid=(t // tm, f // tn),
        in_specs=[pl.BlockSpec((tm, d), lambda i, j: (i, 0)),
                  pl.BlockSpec((1, d), lambda i, j: (0, 0)),
                  _stacked((d, tn), lambda i, j: (0, j), lead),
                  _stacked((d, tn), lambda i, j: (0, j), lead)],
        out_specs=pl.BlockSpec((tm, tn), lambda i, j: (i, j)),
        out_shape=jax.ShapeDtypeStruct((t, f), _BF16),
        scratch_shapes=[pltpu.VMEM((tm, d), _BF16)],
        compiler_params=_params("arbitrary", "arbitrary"),
        name="ffn_up",
    )(h, nw, wg, wu)


def _ffn_down_kernel(a_ref, w_ref, h_ref, nw_ref, o_ref, acc_ref, *, scale, n_sub):
    k = pl.program_id(1)
    last = pl.num_programs(1) - 1

    @pl.when(k == 0)
    def _():
        acc_ref[...] = _dot(a_ref[...], w_ref[...])

    @pl.when((k > 0) & (k < last))
    def _():
        acc_ref[...] += _dot(a_ref[...], w_ref[...])

    @pl.when(k == last)
    def _():
        sub = a_ref.shape[0] // n_sub
        for s in range(n_sub):
            r = slice(s * sub, (s + 1) * sub)
            f = acc_ref[r, :] + _dot(a_ref[r, :], w_ref[...])
            o_ref[r, :] = h_ref[r, :] + scale * _rms(f, nw_ref[...])


def _ffn_down(a, w, lead, h, nw, scale, tm=512, tk=2816, n_sub=2):
    t, f = a.shape
    d = w.shape[-1]
    assert f // tk >= 2, "the first and last contraction steps must be different grid steps"
    return pl.pallas_call(
        functools.partial(_ffn_down_kernel, scale=scale, n_sub=n_sub),
        grid=(t // tm, f // tk),
        in_specs=[pl.BlockSpec((tm, tk), lambda i, k: (i, k)),
                  _stacked((tk, d), lambda i, k: (k, 0), lead),
                  pl.BlockSpec((tm, d), lambda i, k: (i, 0)),
                  pl.BlockSpec((1, d), lambda i, k: (0, 0))],
        out_specs=pl.BlockSpec((tm, d), lambda i, k: (i, 0)),
        out_shape=jax.ShapeDtypeStruct((t, d), _F32),
        scratch_shapes=[pltpu.VMEM((tm, d), _F32)],
        compiler_params=_params("arbitrary", "arbitrary"),
        name="ffn_down",
    )(a, w, h, nw)


def _mix_out_kernel(x1_ref, x2_ref, w_ref, h_ref, nw_ref, o_ref):
    c1 = x1_ref.shape[1]
    sub = x1_ref.shape[0] // 2
    for s in range(2):
        r = slice(s * sub, (s + 1) * sub)
        m = _dot(x1_ref[r, :], w_ref[0:c1, :]) + _dot(x2_ref[r, :], w_ref[c1:, :])
        o_ref[r, :] = h_ref[r, :] + _rms(m, nw_ref[...])


def _mix_out(x1, x2, w, lead, h, nw, tm=512):
    t, c1 = x1.shape
    c2 = x2.shape[1]
    d = w.shape[-1]
    return pl.pallas_call(
        _mix_out_kernel,
        grid=(t // tm,),
        in_specs=[pl.BlockSpec((tm, c1), lambda i: (i, 0)),
                  pl.BlockSpec((tm, c2), lambda i: (i, 0)),
                  _stacked((c1 + c2, d), lambda i: (0, 0), lead),
                  pl.BlockSpec((tm, d), lambda i: (i, 0)),
                  pl.BlockSpec((1, d), lambda i: (0, 0))],
        out_specs=pl.BlockSpec((tm, d), lambda i: (i, 0)),
        out_shape=jax.ShapeDtypeStruct((t, d), _F32),
        compiler_params=_params("arbitrary"),
        name="mix_out",
    )(x1, x2, w, h, nw)


def _rows(start, size, stride):
    return pl.ds(start, size) if stride == 1 else pl.ds(start, size, stride=stride)


def _dilated_kernel(q_ref, k_ref, v_ref, cos_ref, sin_ref, o_ref, qs, ks, pv_s, m_s, l_s):
    seq = q_ref.shape[2]
    blk = A_BACK
    scale = HEAD_DIM ** -0.5
    chunk = 256

    for c in range(0, seq, chunk):
        r = slice(c, c + chunk)
        cos, sin = cos_ref[0, r, :], sin_ref[0, r, :]
        qs[r, :] = _rope(q_ref[0, 0, r, :], cos, sin) * scale
        ks[r, :] = _rope(k_ref[0, 0, r, :], cos, sin)

    qi = lax.broadcasted_iota(jnp.int32, (blk, blk), 0)
    kj = lax.broadcasted_iota(jnp.int32, (blk, blk), 1)
    cur_ok = kj <= qi
    prev_ok = kj >= qi
    ones = jnp.ones((blk, HEAD_DIM), _BF16)

    for g, dil in enumerate(A_DILATIONS):
        for res in range(dil):
            k_prev = v_prev = None
            for n in range(seq // (blk * dil)):
                rows = _rows(res + n * blk * dil, blk, dil)
                qb = qs[rows, :].astype(_BF16)
                k_cur = ks[rows, :].astype(_BF16)
                v_cur = jnp.concatenate([v_ref[0, 0, rows, :].astype(_BF16), ones], axis=1)
                s_cur = jnp.where(cur_ok, _dot_t(qb, k_cur), -jnp.inf)
                if n == 0:
                    m = jnp.max(s_cur, axis=-1, keepdims=True)
                    pva = _dot(jnp.exp(s_cur - m).astype(_BF16), v_cur)
                else:
                    s_prev = jnp.where(prev_ok, _dot_t(qb, k_prev), -jnp.inf)
                    m = jnp.max(jnp.maximum(s_cur, s_prev), axis=-1, keepdims=True)
                    pva = (_dot(jnp.exp(s_cur - m).astype(_BF16), v_cur)
                           + _dot(jnp.exp(s_prev - m).astype(_BF16), v_prev))
                pv_s[g, rows, :] = pva[:, 0:HEAD_DIM]
                l_s[g, rows, :] = pva[:, HEAD_DIM:]
                m_s[g, rows, :] = jnp.broadcast_to(m, (blk, HEAD_DIM))
                k_prev, v_prev = k_cur, v_cur

    for c in range(0, seq, chunk):
        r = slice(c, c + chunk)
        m_all = jnp.maximum(jnp.maximum(m_s[0, r, :], m_s[1, r, :]), m_s[2, r, :])
        num = jnp.zeros((chunk, HEAD_DIM), _F32)
        den = jnp.zeros((chunk, HEAD_DIM), _F32)
        for g in range(len(A_DILATIONS)):
            w = jnp.exp(m_s[g, r, :] - m_all)
            num = num + w * pv_s[g, r, :]
            den = den + w * l_s[g, r, :]
        o_ref[0, r, :] = (num / den).astype(o_ref.dtype)


def _dilated_attention(slabs, cos, sin):
    batch, _, seq, _ = slabs.shape
    head = lambda off: pl.BlockSpec((1, 1, seq, LANE), lambda b, h: (b, off + h, 0, 0))
    table = pl.BlockSpec((1, seq, LANE), lambda b, h: (b, 0, 0))
    return pl.pallas_call(
        _dilated_kernel,
        grid=(batch, A_HEADS),
        in_specs=[head(0), head(A_HEADS), head(2 * A_HEADS), table, table],
        out_specs=pl.BlockSpec((1, seq, LANE), lambda b, h: (b, 0, h)),
        out_shape=jax.ShapeDtypeStruct((batch, seq, A_HEADS * HEAD_DIM), _BF16),
        scratch_shapes=[pltpu.VMEM((seq, HEAD_DIM), _F32), pltpu.VMEM((seq, HEAD_DIM), _F32),
                        pltpu.VMEM((3, seq, HEAD_DIM), _F32), pltpu.VMEM((3, seq, HEAD_DIM), _F32),
                        pltpu.VMEM((3, seq, HEAD_DIM), _F32)],
        compiler_params=_params("arbitrary", "arbitrary"),
        name="dilated_attention",
    )(slabs, slabs, slabs, cos, sin)


def _pool_kernel(u_ref, w_ref, sc_ref, o_ref):
    seq = u_ref.shape[2]
    row = lax.broadcasted_iota(jnp.int32, (seq, LANE), 0)
    for g, win in enumerate(B_WINDOWS):
        x = u_ref[0, g]
        s = x
        step = 1
        while step < win:
            s = s + _shift_rows(s, step, row)
            step *= 2
        cnt = jnp.minimum(row + 1, win).astype(_F32)
        pooled = s / cnt - x
        mixed = _dot(pooled.astype(_BF16), w_ref[g]) * sc_ref[:, g * LANE:(g + 1) * LANE]
        o_ref[0, :, g * LANE:(g + 1) * LANE] = mixed.astype(o_ref.dtype)


def _multiscale_pool(slabs, pool_w, pool_scale):
    batch, _, seq, _ = slabs.shape
    first = 3 * A_HEADS // B_GROUPS
    return pl.pallas_call(
        _pool_kernel,
        grid=(batch,),
        in_specs=[pl.BlockSpec((1, B_GROUPS, seq, LANE), lambda b: (b, first, 0, 0)),
                  pl.BlockSpec((B_GROUPS, LANE, LANE), lambda b: (0, 0, 0)),
                  pl.BlockSpec((1, B_GROUPS * LANE), lambda b: (0, 0))],
        out_specs=pl.BlockSpec((1, seq, B_GROUPS * LANE), lambda b: (b, 0, 0)),
        out_shape=jax.ShapeDtypeStruct((batch, seq, B_GROUPS * LANE), _BF16),
        compiler_params=_params("arbitrary"),
        name="multiscale_pool",
    )(slabs, pool_w, pool_scale)


def _compress_kernel(x_ref, pe_ref, w1_ref, w2_ref, o_ref):
    nrow = o_ref.shape[3]
    hidden = w1_ref.shape[2]
    a = jnp.zeros((nrow, hidden), _F32)
    b = jnp.zeros((nrow, hidden), _F32)
    for tok in range(CMP_STRIDE):
        x = x_ref[0, 0, pl.ds(tok, nrow, stride=CMP_STRIDE), :]
        lo, hi = tok, CMP_STRIDE + tok
        a = a + _dot((x + pe_ref[0, lo:lo + 1, :]).astype(_BF16), w1_ref[0, lo * HEAD_DIM:(lo + 1) * HEAD_DIM, :])
        b = b + _dot((x + pe_ref[0, hi:hi + 1, :]).astype(_BF16), w1_ref[0, hi * HEAD_DIM:(hi + 1) * HEAD_DIM, :])
    hid = a + pltpu.roll(b, nrow - 1, 0)
    o_ref[0, 0, 0] = _dot(jax.nn.gelu(hid).astype(_BF16), w2_ref[0])


def _compress(slabs, pe, w1, w2, first_slab):
    batch, _, seq, _ = slabs.shape
    nrow = seq // CMP_STRIDE
    return pl.pallas_call(
        _compress_kernel,
        grid=(2, batch, C_KV_HEADS),
        in_specs=[pl.BlockSpec((1, 1, seq, LANE), lambda kv, b, g: (b, first_slab + 2 * kv + g, 0, 0)),
                  pl.BlockSpec((1, CMP_BLOCK, HEAD_DIM), lambda kv, b, g: (kv, 0, 0)),
                  pl.BlockSpec((1, CMP_BLOCK * HEAD_DIM, w1.shape[2]), lambda kv, b, g: (kv, 0, 0)),
                  pl.BlockSpec((1, w2.shape[1], HEAD_DIM), lambda kv, b, g: (kv, 0, 0))],
        out_specs=pl.BlockSpec((1, 1, 1, nrow, HEAD_DIM), lambda kv, b, g: (kv, b, g, 0, 0)),
        out_shape=jax.ShapeDtypeStruct((2, batch, C_KV_HEADS, nrow, HEAD_DIM), _F32),
        compiler_params=_params("arbitrary", "arbitrary", "arbitrary"),
        name="compress",
    )(slabs, pe, w1, w2)


def _cmp_select_kernel(q_ref, kc_ref, vc_ref, o_ref, sel_ref, *, nslc):
    tq = q_ref.shape[2]
    ncmp = kc_ref.shape[3]
    scale = HEAD_DIM ** -0.5
    t = pl.program_id(2) * tq + lax.broadcasted_iota(jnp.int32, (tq, ncmp), 0)
    n = lax.broadcasted_iota(jnp.int32, (tq, ncmp), 1)
    cmask = n * CMP_STRIDE + (CMP_BLOCK - 1) <= t

    cj = lax.broadcasted_iota(jnp.int32, (LANE, ncmp), 0) * SLC_BLOCK
    cn = lax.broadcasted_iota(jnp.int32, (LANE, ncmp), 1) * CMP_STRIDE
    cover_t = ((cn < cj + SLC_BLOCK) & (cn + CMP_BLOCK > cj)).astype(_BF16)

    kc = kc_ref[0, 0, 0].astype(_BF16)
    vc = vc_ref[0, 0, 0].astype(_BF16)
    imp_t = jnp.zeros((LANE, tq), _F32)
    for hh in range(C_GROUP):
        sc = jnp.where(cmask, _dot_t(q_ref[0, hh].astype(_BF16), kc) * scale, -jnp.inf)
        m = jnp.max(sc, axis=-1, keepdims=True)
        m = jnp.where(jnp.isfinite(m), m, 0.0)
        p = jnp.exp(sc - m)
        den = jnp.sum(p, axis=-1, keepdims=True)
        p = (p / jnp.maximum(den, 1.0)).astype(_BF16)
        o_ref[0, hh] = _dot(p, vc)
        imp_t = imp_t + _dot_t(cover_t, p)

    j = lax.broadcasted_iota(jnp.int32, (nslc, tq), 0)
    tt = pl.program_id(2) * tq + lax.broadcasted_iota(jnp.int32, (nslc, tq), 1)
    cur = lax.shift_right_logical(tt, SLC_BLOCK.bit_length() - 1)
    visible = j <= cur
    forced = visible & ((j == 0) | (j >= cur - 1))
    score = jnp.where(forced, FORCED_SCORE, jnp.where(visible, imp_t[0:nslc, :], -FORCED_SCORE))
    jf = j.astype(_F32)
    sel_t = jnp.zeros((nslc, tq), _F32)
    for _ in range(min(SLC_TOPN, nslc)):
        best = jnp.max(score, axis=0, keepdims=True)
        pick = jnp.min(jnp.where(score == best, jf, float(nslc)), axis=0, keepdims=True)
        hit = jf == pick
        sel_t = jnp.where(hit, 1.0, sel_t)
        score = jnp.where(hit, -jnp.inf, score)
    sel_t = jnp.concatenate([sel_t, jnp.zeros((LANE - nslc, tq), _F32)], axis=0)
    sel_ref[0, 0] = sel_t.T.astype(sel_ref.dtype)


def _cmp_select(slabs, kv_cmp, tq=256):
    batch, _, seq, _ = slabs.shape
    cmp_spec = lambda which: pl.BlockSpec((1, 1, 1) + kv_cmp.shape[3:], lambda b, g, i: (which, b, g, 0, 0))
    return pl.pallas_call(
        functools.partial(_cmp_select_kernel, nslc=seq // SLC_BLOCK),
        grid=(batch, C_KV_HEADS, seq // tq),
        in_specs=[pl.BlockSpec((1, C_GROUP, tq, LANE), lambda b, g, i: (b, g, i, 0)),
                  cmp_spec(0), cmp_spec(1)],
        out_specs=[pl.BlockSpec((1, C_GROUP, tq, LANE), lambda b, g, i: (b, g, i, 0)),
                   pl.BlockSpec((1, 1, tq, LANE), lambda b, g, i: (b, g, i, 0))],
        out_shape=[jax.ShapeDtypeStruct((batch, C_HEADS, seq, HEAD_DIM), _F32),
                   jax.ShapeDtypeStruct((batch, C_KV_HEADS, seq, LANE), _BF16)],
        compiler_params=_params("arbitrary", "arbitrary", "arbitrary"),
        name="cmp_select",
    )(slabs, kv_cmp, kv_cmp)


def _group_softmax_pv(q, k, v1, bias_s, s_s, p_s, o_s):
    nk = k.shape[0]
    tq = bias_s.shape[0]
    piece = 512 if nk % 512 == 0 else LANE
    pieces = [slice(c, c + piece) for c in range(0, nk, piece)]
    s = _dot_t(q, k)
    for hh in range(C_GROUP):
        r = slice(hh * tq, (hh + 1) * tq)
        s_s[r, 0:nk] = s[r] + bias_s[:, 0:nk]
    for hh in range(C_GROUP):
        r = slice(hh * tq, (hh + 1) * tq)
        m = None
        for c in pieces:
            mc = jnp.max(s_s[r, c], axis=-1, keepdims=True)
            m = mc if m is None else jnp.maximum(m, mc)
        for c in pieces:
            p_s[r, c] = jnp.exp(s_s[r, c] - m).astype(_BF16)
    o = _dot(p_s[:, 0:nk], v1)
    o_s[...] = o[:, 0:HEAD_DIM] / o[:, HEAD_DIM:]


def _nsa_kernel(q_ref, ks_ref, vs_ref, kw_ref, vw_ref, sel_ref, oc_ref, gl_ref, cos_ref, sin_ref, o_ref,
                ksr, vs1, kwr, vw1, expand_s, qr, bias_s, s_s, p_s, bias_w, s_w, p_w, oslc_s, owin_s):
    n_sub = qr.shape[0]
    tq = q_ref.shape[2] // n_sub
    seq = ks_ref.shape[2]
    qi = pl.program_id(2)
    chunk = 256
    shift = SLC_BLOCK.bit_length() - 1

    @pl.when(qi == 0)
    def _():
        def prep(c, _):
            r = pl.ds(pl.multiple_of(c * chunk, chunk), chunk)
            cos, sin = cos_ref[0, r, :], sin_ref[0, r, :]
            ones = jnp.ones((chunk, HEAD_DIM), _BF16)
            ksr[r, :] = _rope(ks_ref[0, 0, r, :], cos, sin).astype(_BF16)
            kwr[r, :] = _rope(kw_ref[0, 0, r, :], cos, sin).astype(_BF16)
            vs1[r, 0:HEAD_DIM] = vs_ref[0, 0, r, :].astype(_BF16)
            vs1[r, HEAD_DIM:] = ones
            vw1[r, 0:HEAD_DIM] = vw_ref[0, 0, r, :].astype(_BF16)
            vw1[r, HEAD_DIM:] = ones
            return 0

        lax.fori_loop(0, seq // chunk, prep, 0)
        blk = lax.broadcasted_iota(jnp.int32, (LANE, seq), 0)
        key = lax.broadcasted_iota(jnp.int32, (LANE, seq), 1)
        expand_s[...] = (blk == lax.shift_right_logical(key, shift)).astype(_BF16)

    base = pl.multiple_of(qi * (n_sub * tq), n_sub * tq)
    sub_rows = [slice(u * tq, (u + 1) * tq) for u in range(n_sub)]
    for u in range(n_sub):
        cos_q = cos_ref[0, pl.ds(base + u * tq, tq), :]
        sin_q = sin_ref[0, pl.ds(base + u * tq, tq), :]
        for hh in range(C_GROUP):
            q_rot = _rope(q_ref[0, hh, sub_rows[u], :], cos_q, sin_q)
            qr[u, hh * tq:(hh + 1) * tq, :] = (q_rot * HEAD_DIM ** -0.5).astype(_BF16)

    def window_branch(u):
        q0 = base + u * tq
        span = WIN_SIZE + tq
        k0 = pl.multiple_of(jnp.maximum(q0 - WIN_SIZE, 0), tq)
        t = q0 + lax.broadcasted_iota(jnp.int32, (tq, span), 0)
        key = k0 + lax.broadcasted_iota(jnp.int32, (tq, span), 1)
        bias_w[u] = jnp.where((key <= t) & (key > t - WIN_SIZE), 0.0, -jnp.inf)
        _group_softmax_pv(qr[u], kwr[pl.ds(k0, span), :], vw1[pl.ds(k0, span), :],
                          bias_w.at[u], s_w.at[u], p_w.at[u], owin_s.at[u])

    n_extent = 4
    step = seq // n_extent
    for nt in range(1, n_extent + 1):
        @pl.when(base // step + 1 == nt)
        def _(nk=nt * step):
            for u in range(n_sub):
                for c in range(0, nk, step):
                    t = base + u * tq + lax.broadcasted_iota(jnp.int32, (tq, step), 0)
                    key = c + lax.broadcasted_iota(jnp.int32, (tq, step), 1)
                    ok = (_dot(sel_ref[0, 0, sub_rows[u], :], expand_s[:, c:c + step]) > 0.5) & (key <= t)
                    bias_s[u, :, c:c + step] = jnp.where(ok, 0.0, -jnp.inf)
                _group_softmax_pv(qr[u], ksr[0:nk, :], vs1[0:nk, :],
                                  bias_s.at[u], s_s.at[u], p_s.at[u], oslc_s.at[u])
                window_branch(u)

    gates = jax.nn.sigmoid(gl_ref[0, 0])
    for u in range(n_sub):
        for hh in range(C_GROUP):
            r = slice(hh * tq, (hh + 1) * tq)
            g_cmp, g_slc, g_win = (gates[sub_rows[u], 3 * hh + c:3 * hh + c + 1] for c in range(3))
            o = g_cmp * oc_ref[0, hh, sub_rows[u], :] + g_slc * oslc_s[u, r, :] + g_win * owin_s[u, r, :]
            o_ref[0, sub_rows[u], hh * HEAD_DIM:(hh + 1) * HEAD_DIM] = o.astype(o_ref.dtype)


def _nsa_attention(slabs, sel, o_cmp, cos, sin, tq=128, n_sub=2):
    batch, _, seq, _ = slabs.shape
    kv = lambda off: pl.BlockSpec((1, 1, seq, LANE), lambda b, g, i: (b, off + g, 0, 0))
    table = pl.BlockSpec((1, seq, LANE), lambda b, g, i: (b, 0, 0))
    rows = C_GROUP * tq
    span = WIN_SIZE + tq
    tile = n_sub * tq
    assert (seq // 4) % tile == 0, "the sub-tiles of a step must share one selected-branch extent"
    return pl.pallas_call(
        _nsa_kernel,
        grid=(batch, C_KV_HEADS, seq // tile),
        in_specs=[pl.BlockSpec((1, C_GROUP, tile, LANE), lambda b, g, i: (b, g, i, 0)),
                  kv(16), kv(18), kv(20), kv(22),
                  pl.BlockSpec((1, 1, tile, LANE), lambda b, g, i: (b, g, i, 0)),
                  pl.BlockSpec((1, C_GROUP, tile, LANE), lambda b, g, i: (b, g, i, 0)),
                  pl.BlockSpec((1, 1, tile, LANE), lambda b, g, i: (b, 24 + g, i, 0)),
                  table, table],
        out_specs=pl.BlockSpec((1, tile, C_GROUP * HEAD_DIM), lambda b, g, i: (b, i, g)),
        out_shape=jax.ShapeDtypeStruct((batch, seq, C_HEADS * HEAD_DIM), _BF16),
        scratch_shapes=[pltpu.VMEM((seq, HEAD_DIM), _BF16), pltpu.VMEM((seq, 2 * HEAD_DIM), _BF16)] * 2
        + [pltpu.VMEM((LANE, seq), _BF16), pltpu.VMEM((n_sub, rows, HEAD_DIM), _BF16),
           pltpu.VMEM((n_sub, tq, seq), _F32), pltpu.VMEM((n_sub, rows, seq), _F32),
           pltpu.VMEM((n_sub, rows, seq), _BF16),
           pltpu.VMEM((n_sub, tq, span), _F32), pltpu.VMEM((n_sub, rows, span), _F32),
           pltpu.VMEM((n_sub, rows, span), _BF16)]
        + [pltpu.VMEM((n_sub, rows, HEAD_DIM), _F32)] * 2,
        compiler_params=_params("arbitrary", "arbitrary", "arbitrary"),
        name="nsa_attention",
    )(slabs, slabs, slabs, slabs, slabs, sel, o_cmp, slabs, cos, sin)


def _conv_kernel(u_ref, c_ref, b_ref, w_ref, o_ref):
    seq = u_ref.shape[2]
    row = lax.broadcasted_iota(jnp.int32, (seq, LANE), 0)
    u = c_ref[0, 0] * u_ref[0, 0]
    taps = w_ref.shape[0]
    conv = _shift_rows(u, taps - 1, row) * w_ref[0:1, :]
    for j in range(1, taps):
        shifted = u if j == taps - 1 else _shift_rows(u, taps - 1 - j, row)
        conv = conv + shifted * w_ref[j:j + 1, :]
    o_ref[0] = (b_ref[0, 0] * conv).astype(o_ref.dtype)


def _short_conv(slabs, conv_w, first_slab):
    batch, _, seq, _ = slabs.shape
    nd = D_WIDTH // LANE
    part = lambda off: pl.BlockSpec((1, 1, seq, LANE), lambda b, j: (b, first_slab + off + j, 0, 0))
    return pl.pallas_call(
        _conv_kernel,
        grid=(batch, nd),
        in_specs=[part(0), part(nd), part(2 * nd),
                  pl.BlockSpec((conv_w.shape[0], LANE), lambda b, j: (0, j))],
        out_specs=pl.BlockSpec((1, seq, LANE), lambda b, j: (b, 0, j)),
        out_shape=jax.ShapeDtypeStruct((batch, seq, D_WIDTH), _BF16),
        compiler_params=_params("arbitrary", "arbitrary"),
        name="short_conv",
    )(slabs, slabs, slabs, conv_w)


def _pad_cols(w, total):
    return jnp.pad(w, ((0, 0), (0, total - w.shape[1])))


def _odd_in_weight(w):
    qkv = C_HEADS * HEAD_DIM + 6 * C_KV_HEADS * HEAD_DIM
    ngate = 3 * C_GROUP
    gates = [_pad_cols(w[:, qkv + g * ngate:qkv + (g + 1) * ngate], LANE) for g in range(C_KV_HEADS)]
    rest = w[:, qkv + C_KV_HEADS * ngate:]
    return _pad_cols(jnp.concatenate([w[:, :qkv]] + gates + [rest], axis=1), PROJ_SLABS * LANE)


def kernel(x, positions, norm_w, ffn_w_gate, ffn_w_up, ffn_w_down, ev_w_in, ev_w_out, pool_w, pool_scale,
           od_w_in, od_w_out, cmp_pe_k, cmp_w1_k, cmp_w2_k, cmp_pe_v, cmp_w1_v, cmp_w2_v, conv_w):
    batch, seq, d = x.shape
    depth = norm_w.shape[0]
    bf = lambda a: a.astype(_BF16)

    inv_freq = 1.0 / (ROPE_THETA ** (jnp.arange(0, HEAD_DIM, 2, dtype=_F32) / HEAD_DIM))
    ang = positions.astype(_F32)[..., None] * inv_freq
    cos = jnp.concatenate([jnp.cos(ang), jnp.cos(ang)], axis=-1)
    sin = jnp.concatenate([-jnp.sin(ang), jnp.sin(ang)], axis=-1)

    nw = norm_w.reshape(depth, 6, 1, d)
    h = x.reshape(batch * seq, d)

    w_gate, w_up, w_down = ffn_w_gate, ffn_w_up, bf(ffn_w_down)
    w_ev_in, w_ev_out = bf(ev_w_in), bf(ev_w_out)
    w_od_in, w_od_out = bf(jax.vmap(_odd_in_weight)(od_w_in)), bf(od_w_out)

    def ffn(h, layer, which):
        a = _ffn_up(h, nw[layer, 4 * which], w_gate, w_up, (layer, which))
        return _ffn_down(a, w_down, (layer, which), h, nw[layer, 4 * which + 1], HALF_STEP)

    for layer in range(depth):
        i = layer // 2
        h = ffn(h, layer, 0)
        if layer % 2 == 0:
            slabs = _norm_proj(h, nw[layer, 2], w_ev_in, (i,), batch, seq)
            o_a = _dilated_attention(slabs, cos, sin)
            o_b = _multiscale_pool(slabs, bf(pool_w[i]), pool_scale[i].reshape(1, -1))
            h = _mix_out(o_a.reshape(batch * seq, -1), o_b.reshape(batch * seq, -1), w_ev_out, (i,), h,
                         nw[layer, 3])
        else:
            slabs = _norm_proj(h, nw[layer, 2], w_od_in, (i,), batch, seq)
            kv_cmp = _compress(slabs, jnp.stack([cmp_pe_k[i], cmp_pe_v[i]]),
                               bf(jnp.stack([cmp_w1_k[i], cmp_w1_v[i]])),
                               bf(jnp.stack([cmp_w2_k[i], cmp_w2_v[i]])), C_HEADS)
            o_cmp, sel = _cmp_select(slabs, kv_cmp)
            o_c = _nsa_attention(slabs, sel, o_cmp, cos, sin)
            y_d = _short_conv(slabs, conv_w[i], 26)
            h = _mix_out(o_c.reshape(batch * seq, -1), y_d.reshape(batch * seq, -1), w_od_out, (i,), h,
                         nw[layer, 3])
        h = ffn(h, layer, 1)
    return h.reshape(batch, seq, d)
```

```python
import functools

import jax
import jax.numpy as jnp
from jax import lax
from jax.experimental import pallas as pl
from jax.experimental.pallas import tpu as pltpu

HEAD_DIM = 128
ROPE_THETA = 10000.0
NORM_EPS = 1e-6
LOG2_E = 1.4426950408889634
HALF_STEP = 0.5
LANE = 128

A_HEADS = 12
A_DILATIONS = (1, 4, 16)
A_BACK = 128
B_WINDOWS = (2, 4, 8, 16)
B_GROUPS = 4

C_HEADS = 12
C_KV_HEADS = 2
C_GROUP = C_HEADS // C_KV_HEADS
CMP_BLOCK = 32
CMP_STRIDE = 16
SLC_BLOCK = 64
SLC_TOPN = 8
WIN_SIZE = 512
FORCED_SCORE = 1e9
D_WIDTH = 512
PROJ_SLABS = 40
NORM_SUB_TILES = 4

VMEM_LIMIT = 56 * 1024 * 1024

_BF16 = jnp.bfloat16
_F32 = jnp.float32


def _params(*sem):
    return pltpu.CompilerParams(dimension_semantics=sem, vmem_limit_bytes=VMEM_LIMIT)


def _rms(x, w):
    return x * lax.rsqrt(jnp.mean(x * x, axis=-1, keepdims=True) + NORM_EPS) * w


def _dot(a, b):
    return jnp.dot(a, b, preferred_element_type=_F32)


def _dot_t(a, b):
    return lax.dot_general(a, b, (((1,), (1,)), ((), ())), preferred_element_type=_F32)


def _rope(x, cos, sin):
    return x * cos + pltpu.roll(x, HEAD_DIM // 2, 1) * sin


def _shift_rows(x, k, row):
    return jnp.where(row >= k, pltpu.roll(x, k, 0), 0.0)


def _proj_kernel(h_ref, nw_ref, w_ref, o_ref, xn_ref):
    def use(xn, r):
        acc = _dot(xn, w_ref[...])
        for s in range(o_ref.shape[1]):
            o_ref[0, s, r, :] = acc[:, s * LANE:(s + 1) * LANE]

    _normed_rows(h_ref, nw_ref, xn_ref, use)


def _stacked(block, index_map, lead):
    return pl.BlockSpec((None,) * len(lead) + block, lambda *g: tuple(lead) + index_map(*g))


def _norm_proj(h, nw, w, lead, batch, seq, tm=1024, tn=1024):
    t, d = h.shape
    n = w.shape[-1]
    per_b = seq // tm
    return pl.pallas_call(
        _proj_kernel,
        grid=(t // tm, n // tn),
        in_specs=[pl.BlockSpec((tm, d), lambda i, j: (i, 0)),
                  pl.BlockSpec((1, d), lambda i, j: (0, 0)),
                  _stacked((d, tn), lambda i, j: (0, j), lead)],
        out_specs=pl.BlockSpec((1, tn // LANE, tm, LANE), lambda i, j: (i // per_b, j, i % per_b, 0)),
        out_shape=jax.ShapeDtypeStruct((batch, n // LANE, seq, LANE), _F32),
        scratch_shapes=[pltpu.VMEM((tm, d), _BF16)],
        compiler_params=_params("arbitrary", "arbitrary"),
        name="norm_proj",
    )(h, nw, w)


def _normed_rows(h_ref, nw_ref, xn_ref, use):
    j = pl.program_id(1)

    @pl.when(j == 0)
    def _():
        sub = h_ref.shape[0] // NORM_SUB_TILES
        for s in range(NORM_SUB_TILES):
            r = slice(s * sub, (s + 1) * sub)
            xn = _rms(h_ref[r, :], nw_ref[...]).astype(_BF16)
            xn_ref[r, :] = xn
            use(xn, r)

    @pl.when(j > 0)
    def _():
        use(xn_ref[...], slice(None))


def _ffn_up_kernel(h_ref, nw_ref, wg_ref, wu_ref, o_ref, xn_ref):
    def use(xn, r):
        g = _dot(xn, wg_ref[...])
        u = _dot(xn, wu_ref[...])
        o_ref[r, :] = (jax.nn.silu(g) * u).astype(_BF16)

    _normed_rows(h_ref, nw_ref, xn_ref, use)


def _ffn_up(h, nw, wg, wu, lead, tm=1024, tn=512):
    t, d = h.shape
    f = wg.shape[-1]
    return pl.pallas_call(
        _ffn_up_kernel,
        grid=(t // tm, f // tn),
        in_specs=[pl.BlockSpec((tm, d), lambda i, j: (i, 0)),
                  pl.BlockSpec((1, d), lambda i, j: (0, 0)),
                  _stacked((d, tn), lambda i, j: (0, j), lead),
                  _stacked((d, tn), lambda i, j: (0, j), lead)],
        out_specs=pl.BlockSpec((tm, tn), lambda i, j: (i, j)),
        out_shape=jax.ShapeDtypeStruct((t, f), _BF16),
        scratch_shapes=[pltpu.VMEM((tm, d), _BF16)],
        compiler_params=_params("arbitrary", "arbitrary"),
        name="ffn_up",
    )(h, nw, wg, wu)


def _ffn_down_kernel(a_ref, w_ref, h_ref, nw_ref, o_ref, acc_ref, *, scale, n_sub):
    k = pl.program_id(1)
    last = pl.num_programs(1) - 1

    @pl.when(k == 0)
    def _():
        acc_ref[...] = _dot(a_ref[...], w_ref[...])

    @pl.when((k > 0) & (k < last))
    def _():
        acc_ref[...] += _dot(a_ref[...], w_ref[...])

    @pl.when(k == last)
    def _():
        sub = a_ref.shape[0] // n_sub
        for s in range(n_sub):
            r = slice(s * sub, (s + 1) * sub)
            f = acc_ref[r, :] + _dot(a_ref[r, :], w_ref[...])
            o_ref[r, :] = h_ref[r, :] + scale * _rms(f, nw_ref[...])


def _ffn_down(a, w, lead, h, nw, scale, tm=512, tk=2816, n_sub=2):
    t, f = a.shape
    d = w.shape[-1]
    assert f // tk >= 2, "the first and last contraction steps must be different grid steps"
    return pl.pallas_call(
        functools.partial(_ffn_down_kernel, scale=scale, n_sub=n_sub),
        grid=(t // tm, f // tk),
        in_specs=[pl.BlockSpec((tm, tk), lambda i, k: (i, k)),
                  _stacked((tk, d), lambda i, k: (k, 0), lead),
                  pl.BlockSpec((tm, d), lambda i, k: (i, 0)),
                  pl.BlockSpec((1, d), lambda i, k: (0, 0))],
        out_specs=pl.BlockSpec((tm, d), lambda i, k: (i, 0)),
        out_shape=jax.ShapeDtypeStruct((t, d), _F32),
        scratch_shapes=[pltpu.VMEM((tm, d), _F32)],
        compiler_params=_params("arbitrary", "arbitrary"),
        name="ffn_down",
    )(a, w, h, nw)


def _mix_out_kernel(x1_ref, x2_ref, w_ref, h_ref, nw_ref, o_ref):
    c1 = x1_ref.shape[1]
    sub = x1_ref.shape[0] // 2
    for s in range(2):
        r = slice(s * sub, (s + 1) * sub)
        m = _dot(x1_ref[r, :], w_ref[0:c1, :]) + _dot(x2_ref[r, :], w_ref[c1:, :])
        o_ref[r, :] = h_ref[r, :] + _rms(m, nw_ref[...])


def _mix_out(x1, x2, w, lead, h, nw, tm=512):
    t, c1 = x1.shape
    c2 = x2.shape[1]
    d = w.shape[-1]
    return pl.pallas_call(
        _mix_out_kernel,
        grid=(t // tm,),
        in_specs=[pl.BlockSpec((tm, c1), lambda i: (i, 0)),
                  pl.BlockSpec((tm, c2), lambda i: (i, 0)),
                  _stacked((c1 + c2, d), lambda i: (0, 0), lead),
                  pl.BlockSpec((tm, d), lambda i: (i, 0)),
                  pl.BlockSpec((1, d), lambda i: (0, 0))],
        out_specs=pl.BlockSpec((tm, d), lambda i: (i, 0)),
        out_shape=jax.ShapeDtypeStruct((t, d), _F32),
        compiler_params=_params("arbitrary"),
        name="mix_out",
    )(x1, x2, w, h, nw)


def _rows(start, size, stride):
    return pl.ds(start, size) if stride == 1 else pl.ds(start, size, stride=stride)


def _dilated_kernel(q_ref, k_ref, v_ref, cos_ref, sin_ref, o_ref, qs, ks, pv_s, m_s, l_s):
    seq = q_ref.shape[2]
    blk = A_BACK
    scale = HEAD_DIM ** -0.5 * LOG2_E
    chunk = 256

    for c in range(0, seq, chunk):
        r = slice(c, c + chunk)
        cos, sin = cos_ref[0, r, :], sin_ref[0, r, :]
        qs[r, :] = _rope(q_ref[0, 0, r, :], cos, sin) * scale
        ks[r, :] = _rope(k_ref[0, 0, r, :], cos, sin)

    qi = lax.broadcasted_iota(jnp.int32, (blk, blk), 0)
    kj = lax.broadcasted_iota(jnp.int32, (blk, blk), 1)
    cur_ok = kj <= qi
    both_ok = jnp.concatenate([kj >= qi, cur_ok], axis=1)
    ones = jnp.ones((blk, HEAD_DIM), _BF16)

    for g, dil in enumerate(A_DILATIONS):
        for res in range(dil):
            k_prev = v_prev = None
            for n in range(seq // (blk * dil)):
                rows = _rows(res + n * blk * dil, blk, dil)
                qb = qs[rows, :].astype(_BF16)
                k_cur = ks[rows, :].astype(_BF16)
                v_cur = jnp.concatenate([v_ref[0, 0, rows, :].astype(_BF16), ones], axis=1)
                if n == 0:
                    s = jnp.where(cur_ok, _dot_t(qb, k_cur), -jnp.inf)
                    v_all = v_cur
                else:
                    s = jnp.where(both_ok, _dot_t(qb, jnp.concatenate([k_prev, k_cur], axis=0)), -jnp.inf)
                    v_all = jnp.concatenate([v_prev, v_cur], axis=0)
                m = jnp.max(s, axis=-1, keepdims=True)
                pva = _dot(jnp.exp2(s - m).astype(_BF16), v_all)
                pv_s[g, rows, :] = pva[:, 0:HEAD_DIM]
                l_s[g, rows, :] = pva[:, HEAD_DIM:]
                m_s[g, rows, :] = jnp.broadcast_to(m, (blk, HEAD_DIM))
                k_prev, v_prev = k_cur, v_cur

    for c in range(0, seq, chunk):
        r = slice(c, c + chunk)
        m_all = jnp.maximum(jnp.maximum(m_s[0, r, :], m_s[1, r, :]), m_s[2, r, :])
        num = jnp.zeros((chunk, HEAD_DIM), _F32)
        den = jnp.zeros((chunk, HEAD_DIM), _F32)
        for g in range(len(A_DILATIONS)):
            w = jnp.exp2(m_s[g, r, :] - m_all)
            num = num + w * pv_s[g, r, :]
            den = den + w * l_s[g, r, :]
        o_ref[0, r, :] = (num / den).astype(o_ref.dtype)


def _dilated_attention(slabs, cos, sin):
    batch, _, seq, _ = slabs.shape
    head = lambda off: pl.BlockSpec((1, 1, seq, LANE), lambda b, h: (b, off + h, 0, 0))
    table = pl.BlockSpec((1, seq, LANE), lambda b, h: (b, 0, 0))
    return pl.pallas_call(
        _dilated_kernel,
        grid=(batch, A_HEADS),
        in_specs=[head(0), head(A_HEADS), head(2 * A_HEADS), table, table],
        out_specs=pl.BlockSpec((1, seq, LANE), lambda b, h: (b, 0, h)),
        out_shape=jax.ShapeDtypeStruct((batch, seq, A_HEADS * HEAD_DIM), _BF16),
        scratch_shapes=[pltpu.VMEM((seq, HEAD_DIM), _F32), pltpu.VMEM((seq, HEAD_DIM), _F32),
                        pltpu.VMEM((3, seq, HEAD_DIM), _F32), pltpu.VMEM((3, seq, HEAD_DIM), _F32),
                        pltpu.VMEM((3, seq, HEAD_DIM), _F32)],
        compiler_params=_params("arbitrary", "arbitrary"),
        name="dilated_attention",
    )(slabs, slabs, slabs, cos, sin)


def _pool_kernel(u_ref, w_ref, sc_ref, o_ref):
    seq = u_ref.shape[2]
    row = lax.broadcasted_iota(jnp.int32, (seq, LANE), 0)
    for g, win in enumerate(B_WINDOWS):
        x = u_ref[0, g]
        s = x
        step = 1
        while step < win:
            s = s + _shift_rows(s, step, row)
            step *= 2
        cnt = jnp.minimum(row + 1, win).astype(_F32)
        pooled = s / cnt - x
        mixed = _dot(pooled.astype(_BF16), w_ref[g]) * sc_ref[:, g * LANE:(g + 1) * LANE]
        o_ref[0, :, g * LANE:(g + 1) * LANE] = mixed.astype(o_ref.dtype)


def _multiscale_pool(slabs, pool_w, pool_scale):
    batch, _, seq, _ = slabs.shape
    first = 3 * A_HEADS // B_GROUPS
    return pl.pallas_call(
        _pool_kernel,
        grid=(batch,),
        in_specs=[pl.BlockSpec((1, B_GROUPS, seq, LANE), lambda b: (b, first, 0, 0)),
                  pl.BlockSpec((B_GROUPS, LANE, LANE), lambda b: (0, 0, 0)),
                  pl.BlockSpec((1, B_GROUPS * LANE), lambda b: (0, 0))],
        out_specs=pl.BlockSpec((1, seq, B_GROUPS * LANE), lambda b: (b, 0, 0)),
        out_shape=jax.ShapeDtypeStruct((batch, seq, B_GROUPS * LANE), _BF16),
        compiler_params=_params("arbitrary"),
        name="multiscale_pool",
    )(slabs, pool_w, pool_scale)


def _compress_kernel(x_ref, pe_ref, w1_ref, w2_ref, o_ref):
    nrow = o_ref.shape[3]
    hidden = w1_ref.shape[2]
    a = jnp.zeros((nrow, hidden), _F32)
    b = jnp.zeros((nrow, hidden), _F32)
    for tok in range(CMP_STRIDE):
        x = x_ref[0, 0, pl.ds(tok, nrow, stride=CMP_STRIDE), :]
        lo, hi = tok, CMP_STRIDE + tok
        a = a + _dot((x + pe_ref[0, lo:lo + 1, :]).astype(_BF16), w1_ref[0, lo * HEAD_DIM:(lo + 1) * HEAD_DIM, :])
        b = b + _dot((x + pe_ref[0, hi:hi + 1, :]).astype(_BF16), w1_ref[0, hi * HEAD_DIM:(hi + 1) * HEAD_DIM, :])
    hid = a + pltpu.roll(b, nrow - 1, 0)
    o_ref[0, 0, 0] = _dot(jax.nn.gelu(hid).astype(_BF16), w2_ref[0])


def _compress(slabs, pe, w1, w2, first_slab):
    batch, _, seq, _ = slabs.shape
    nrow = seq // CMP_STRIDE
    return pl.pallas_call(
        _compress_kernel,
        grid=(2, batch, C_KV_HEADS),
        in_specs=[pl.BlockSpec((1, 1, seq, LANE), lambda kv, b, g: (b, first_slab + 2 * kv + g, 0, 0)),
                  pl.BlockSpec((1, CMP_BLOCK, HEAD_DIM), lambda kv, b, g: (kv, 0, 0)),
                  pl.BlockSpec((1, CMP_BLOCK * HEAD_DIM, w1.shape[2]), lambda kv, b, g: (kv, 0, 0)),
                  pl.BlockSpec((1, w2.shape[1], HEAD_DIM), lambda kv, b, g: (kv, 0, 0))],
        out_specs=pl.BlockSpec((1, 1, 1, nrow, HEAD_DIM), lambda kv, b, g: (kv, b, g, 0, 0)),
        out_shape=jax.ShapeDtypeStruct((2, batch, C_KV_HEADS, nrow, HEAD_DIM), _F32),
        compiler_params=_params("arbitrary", "arbitrary", "arbitrary"),
        name="compress",
    )(slabs, pe, w1, w2)


def _cmp_select_kernel(q_ref, kc_ref, vc_ref, o_ref, sel_ref, *, nslc):
    tq = q_ref.shape[2]
    ncmp = kc_ref.shape[3]
    scale = HEAD_DIM ** -0.5
    t = pl.program_id(2) * tq + lax.broadcasted_iota(jnp.int32, (tq, ncmp), 0)
    n = lax.broadcasted_iota(jnp.int32, (tq, ncmp), 1)
    cmask = n * CMP_STRIDE + (CMP_BLOCK - 1) <= t

    cj = lax.broadcasted_iota(jnp.int32, (LANE, ncmp), 0) * SLC_BLOCK
    cn = lax.broadcasted_iota(jnp.int32, (LANE, ncmp), 1) * CMP_STRIDE
    cover_t = ((cn < cj + SLC_BLOCK) & (cn + CMP_BLOCK > cj)).astype(_BF16)

    kc = kc_ref[0, 0, 0].astype(_BF16)
    vc = vc_ref[0, 0, 0].astype(_BF16)
    imp_t = jnp.zeros((LANE, tq), _F32)
    for hh in range(C_GROUP):
        sc = jnp.where(cmask, _dot_t(q_ref[0, hh].astype(_BF16), kc) * scale, -jnp.inf)
        m = jnp.max(sc, axis=-1, keepdims=True)
        m = jnp.where(jnp.isfinite(m), m, 0.0)
        p = jnp.exp(sc - m)
        den = jnp.sum(p, axis=-1, keepdims=True)
        p = (p / jnp.maximum(den, 1.0)).astype(_BF16)
        o_ref[0, hh] = _dot(p, vc)
        imp_t = imp_t + _dot_t(cover_t, p)

    j = lax.broadcasted_iota(jnp.int32, (nslc, tq), 0)
    tt = pl.program_id(2) * tq + lax.broadcasted_iota(jnp.int32, (nslc, tq), 1)
    cur = lax.shift_right_logical(tt, SLC_BLOCK.bit_length() - 1)
    visible = j <= cur
    forced = visible & ((j == 0) | (j >= cur - 1))
    score = jnp.where(forced, FORCED_SCORE, jnp.where(visible, imp_t[0:nslc, :], -FORCED_SCORE))
    jf = j.astype(_F32)
    sel_t = jnp.zeros((nslc, tq), _F32)
    for _ in range(min(SLC_TOPN, nslc)):
        best = jnp.max(score, axis=0, keepdims=True)
        pick = jnp.min(jnp.where(score == best, jf, float(nslc)), axis=0, keepdims=True)
        hit = jf == pick
        sel_t = jnp.where(hit, 1.0, sel_t)
        score = jnp.where(hit, -jnp.inf, score)
    sel_t = jnp.concatenate([sel_t, jnp.zeros((LANE - nslc, tq), _F32)], axis=0)
    sel_ref[0, 0] = sel_t.T.astype(sel_ref.dtype)


def _cmp_select(slabs, kv_cmp, tq=256):
    batch, _, seq, _ = slabs.shape
    cmp_spec = lambda which: pl.BlockSpec((1, 1, 1) + kv_cmp.shape[3:], lambda b, g, i: (which, b, g, 0, 0))
    return pl.pallas_call(
        functools.partial(_cmp_select_kernel, nslc=seq // SLC_BLOCK),
        grid=(batch, C_KV_HEADS, seq // tq),
        in_specs=[pl.BlockSpec((1, C_GROUP, tq, LANE), lambda b, g, i: (b, g, i, 0)),
                  cmp_spec(0), cmp_spec(1)],
        out_specs=[pl.BlockSpec((1, C_GROUP, tq, LANE), lambda b, g, i: (b, g, i, 0)),
                   pl.BlockSpec((1, 1, tq, LANE), lambda b, g, i: (b, g, i, 0))],
        out_shape=[jax.ShapeDtypeStruct((batch, C_HEADS, seq, HEAD_DIM), _F32),
                   jax.ShapeDtypeStruct((batch, C_KV_HEADS, seq, LANE), _BF16)],
        compiler_params=_params("arbitrary", "arbitrary", "arbitrary"),
        name="cmp_select",
    )(slabs, kv_cmp, kv_cmp)


def _group_softmax_pv(q, k, v1, bias_s, s_s, p_s, o_s):
    nk = k.shape[0]
    tq = bias_s.shape[0]
    piece = 512 if nk % 512 == 0 else LANE
    pieces = [slice(c, c + piece) for c in range(0, nk, piece)]
    s = _dot_t(q, k)
    for hh in range(C_GROUP):
        r = slice(hh * tq, (hh + 1) * tq)
        s_s[r, 0:nk] = s[r] + bias_s[:, 0:nk]
    for hh in range(C_GROUP):
        r = slice(hh * tq, (hh + 1) * tq)
        m = None
        for c in pieces:
            mc = jnp.max(s_s[r, c], axis=-1, keepdims=True)
            m = mc if m is None else jnp.maximum(m, mc)
        for c in pieces:
            p_s[r, c] = jnp.exp2(s_s[r, c] - m).astype(_BF16)
    o = _dot(p_s[:, 0:nk], v1)
    o_s[...] = o[:, 0:HEAD_DIM] / o[:, HEAD_DIM:]


def _nsa_kernel(q_ref, ks_ref, vs_ref, kw_ref, vw_ref, sel_ref, oc_ref, gl_ref, cos_ref, sin_ref, o_ref,
                ksr, vs1, kwr, vw1, expand_s, qr, bias_s, s_s, p_s, bias_w, s_w, p_w, oslc_s, owin_s):
    n_sub = qr.shape[0]
    tq = q_ref.shape[2] // n_sub
    seq = ks_ref.shape[2]
    qi = pl.program_id(2)
    chunk = 256
    shift = SLC_BLOCK.bit_length() - 1

    @pl.when(qi == 0)
    def _():
        def prep(c, _):
            r = pl.ds(pl.multiple_of(c * chunk, chunk), chunk)
            cos, sin = cos_ref[0, r, :], sin_ref[0, r, :]
            ones = jnp.ones((chunk, HEAD_DIM), _BF16)
            ksr[r, :] = _rope(ks_ref[0, 0, r, :], cos, sin).astype(_BF16)
            kwr[r, :] = _rope(kw_ref[0, 0, r, :], cos, sin).astype(_BF16)
            vs1[r, 0:HEAD_DIM] = vs_ref[0, 0, r, :].astype(_BF16)
            vs1[r, HEAD_DIM:] = ones
            vw1[r, 0:HEAD_DIM] = vw_ref[0, 0, r, :].astype(_BF16)
            vw1[r, HEAD_DIM:] = ones
            return 0

        lax.fori_loop(0, seq // chunk, prep, 0)
        blk = lax.broadcasted_iota(jnp.int32, (LANE, seq), 0)
        key = lax.broadcasted_iota(jnp.int32, (LANE, seq), 1)
        expand_s[...] = (blk == lax.shift_right_logical(key, shift)).astype(_BF16)

    base = pl.multiple_of(qi * (n_sub * tq), n_sub * tq)
    sub_rows = [slice(u * tq, (u + 1) * tq) for u in range(n_sub)]
    for u in range(n_sub):
        cos_q = cos_ref[0, pl.ds(base + u * tq, tq), :]
        sin_q = sin_ref[0, pl.ds(base + u * tq, tq), :]
        for hh in range(C_GROUP):
            q_rot = _rope(q_ref[0, hh, sub_rows[u], :], cos_q, sin_q)
            qr[u, hh * tq:(hh + 1) * tq, :] = (q_rot * (HEAD_DIM ** -0.5 * LOG2_E)).astype(_BF16)

    def window_branch(u):
        q0 = base + u * tq
        span = WIN_SIZE + tq
        k0 = pl.multiple_of(jnp.maximum(q0 - WIN_SIZE, 0), tq)
        t = q0 + lax.broadcasted_iota(jnp.int32, (tq, span), 0)
        key = k0 + lax.broadcasted_iota(jnp.int32, (tq, span), 1)
        bias_w[u] = jnp.where((key <= t) & (key > t - WIN_SIZE), 0.0, -jnp.inf)
        _group_softmax_pv(qr[u], kwr[pl.ds(k0, span), :], vw1[pl.ds(k0, span), :],
                          bias_w.at[u], s_w.at[u], p_w.at[u], owin_s.at[u])

    n_extent = 4
    step = seq // n_extent
    for nt in range(1, n_extent + 1):
        @pl.when(base // step + 1 == nt)
        def _(nk=nt * step):
            for u in range(n_sub):
                for c in range(0, nk, step):
                    t = base + u * tq + lax.broadcasted_iota(jnp.int32, (tq, step), 0)
                    key = c + lax.broadcasted_iota(jnp.int32, (tq, step), 1)
                    ok = (_dot(sel_ref[0, 0, sub_rows[u], :], expand_s[:, c:c + step]) > 0.5) & (key <= t)
                    bias_s[u, :, c:c + step] = jnp.where(ok, 0.0, -jnp.inf)
                _group_softmax_pv(qr[u], ksr[0:nk, :], vs1[0:nk, :],
                                  bias_s.at[u], s_s.at[u], p_s.at[u], oslc_s.at[u])
                window_branch(u)

    gates = jax.nn.sigmoid(gl_ref[0, 0])
    for u in range(n_sub):
        for hh in range(C_GROUP):
            r = slice(hh * tq, (hh + 1) * tq)
            g_cmp, g_slc, g_win = (gates[sub_rows[u], 3 * hh + c:3 * hh + c + 1] for c in range(3))
            o = g_cmp * oc_ref[0, hh, sub_rows[u], :] + g_slc * oslc_s[u, r, :] + g_win * owin_s[u, r, :]
            o_ref[0, sub_rows[u], hh * HEAD_DIM:(hh + 1) * HEAD_DIM] = o.astype(o_ref.dtype)


def _nsa_attention(slabs, sel, o_cmp, cos, sin, tq=128, n_sub=2):
    batch, _, seq, _ = slabs.shape
    kv = lambda off: pl.BlockSpec((1, 1, seq, LANE), lambda b, g, i: (b, off + g, 0, 0))
    table = pl.BlockSpec((1, seq, LANE), lambda b, g, i: (b, 0, 0))
    rows = C_GROUP * tq
    span = WIN_SIZE + tq
    tile = n_sub * tq
    assert (seq // 4) % tile == 0, "the sub-tiles of a step must share one selected-branch extent"
    return pl.pallas_call(
        _nsa_kernel,
        grid=(batch, C_KV_HEADS, seq // tile),
        in_specs=[pl.BlockSpec((1, C_GROUP, tile, LANE), lambda b, g, i: (b, g, i, 0)),
                  kv(16), kv(18), kv(20), kv(22),
                  pl.BlockSpec((1, 1, tile, LANE), lambda b, g, i: (b, g, i, 0)),
                  pl.BlockSpec((1, C_GROUP, tile, LANE), lambda b, g, i: (b, g, i, 0)),
                  pl.BlockSpec((1, 1, tile, LANE), lambda b, g, i: (b, 24 + g, i, 0)),
                  table, table],
        out_specs=pl.BlockSpec((1, tile, C_GROUP * HEAD_DIM), lambda b, g, i: (b, i, g)),
        out_shape=jax.ShapeDtypeStruct((batch, seq, C_HEADS * HEAD_DIM), _BF16),
        scratch_shapes=[pltpu.VMEM((seq, HEAD_DIM), _BF16), pltpu.VMEM((seq, 2 * HEAD_DIM), _BF16)] * 2
        + [pltpu.VMEM((LANE, seq), _BF16), pltpu.VMEM((n_sub, rows, HEAD_DIM), _BF16),
           pltpu.VMEM((n_sub, tq, seq), _F32), pltpu.VMEM((n_sub, rows, seq), _F32),
           pltpu.VMEM((n_sub, rows, seq), _BF16),
           pltpu.VMEM((n_sub, tq, span), _F32), pltpu.VMEM((n_sub, rows, span), _F32),
           pltpu.VMEM((n_sub, rows, span), _BF16)]
        + [pltpu.VMEM((n_sub, rows, HEAD_DIM), _F32)] * 2,
        compiler_params=_params("arbitrary", "arbitrary", "arbitrary"),
        name="nsa_attention",
    )(slabs, slabs, slabs, slabs, slabs, sel, o_cmp, slabs, cos, sin)


def _conv_kernel(u_ref, c_ref, b_ref, w_ref, o_ref):
    seq = u_ref.shape[2]
    row = lax.broadcasted_iota(jnp.int32, (seq, LANE), 0)
    u = c_ref[0, 0] * u_ref[0, 0]
    taps = w_ref.shape[0]
    conv = _shift_rows(u, taps - 1, row) * w_ref[0:1, :]
    for j in range(1, taps):
        shifted = u if j == taps - 1 else _shift_rows(u, taps - 1 - j, row)
        conv = conv + shifted * w_ref[j:j + 1, :]
    o_ref[0] = (b_ref[0, 0] * conv).astype(o_ref.dtype)


def _short_conv(slabs, conv_w, first_slab):
    batch, _, seq, _ = slabs.shape
    nd = D_WIDTH // LANE
    part = lambda off: pl.BlockSpec((1, 1, seq, LANE), lambda b, j: (b, first_slab + off + j, 0, 0))
    return pl.pallas_call(
        _conv_kernel,
        grid=(batch, nd),
        in_specs=[part(0), part(nd), part(2 * nd),
                  pl.BlockSpec((conv_w.shape[0], LANE), lambda b, j: (0, j))],
        out_specs=pl.BlockSpec((1, seq, LANE), lambda b, j: (b, 0, j)),
        out_shape=jax.ShapeDtypeStruct((batch, seq, D_WIDTH), _BF16),
        compiler_params=_params("arbitrary", "arbitrary"),
        name="short_conv",
    )(slabs, slabs, slabs, conv_w)


def _pad_cols(w, total):
    return jnp.pad(w, ((0, 0), (0, total - w.shape[1])))


def _odd_in_weight(w):
    qkv = C_HEADS * HEAD_DIM + 6 * C_KV_HEADS * HEAD_DIM
    ngate = 3 * C_GROUP
    gates = [_pad_cols(w[:, qkv + g * ngate:qkv + (g + 1) * ngate], LANE) for g in range(C_KV_HEADS)]
    rest = w[:, qkv + C_KV_HEADS * ngate:]
    return _pad_cols(jnp.concatenate([w[:, :qkv]] + gates + [rest], axis=1), PROJ_SLABS * LANE)


def kernel(x, positions, norm_w, ffn_w_gate, ffn_w_up, ffn_w_down, ev_w_in, ev_w_out, pool_w, pool_scale,
           od_w_in, od_w_out, cmp_pe_k, cmp_w1_k, cmp_w2_k, cmp_pe_v, cmp_w1_v, cmp_w2_v, conv_w):
    batch, seq, d = x.shape
    depth = norm_w.shape[0]
    bf = lambda a: a.astype(_BF16)

    inv_freq = 1.0 / (ROPE_THETA ** (jnp.arange(0, HEAD_DIM, 2, dtype=_F32) / HEAD_DIM))
    ang = positions.astype(_F32)[..., None] * inv_freq
    cos = jnp.concatenate([jnp.cos(ang), jnp.cos(ang)], axis=-1)
    sin = jnp.concatenate([-jnp.sin(ang), jnp.sin(ang)], axis=-1)

    nw = norm_w.reshape(depth, 6, 1, d)
    h = x.reshape(batch * seq, d)

    w_gate, w_up, w_down = bf(ffn_w_gate), bf(ffn_w_up), bf(ffn_w_down)
    w_ev_in, w_ev_out = bf(ev_w_in), bf(ev_w_out)
    w_od_in, w_od_out = bf(jax.vmap(_odd_in_weight)(od_w_in)), bf(od_w_out)

    def ffn(h, layer, which):
        a = _ffn_up(h, nw[layer, 4 * which], w_gate, w_up, (layer, which))
        return _ffn_down(a, w_down, (layer, which), h, nw[layer, 4 * which + 1], HALF_STEP)

    for layer in range(depth):
        i = layer // 2
        h = ffn(h, layer, 0)
        if layer % 2 == 0:
            slabs = _norm_proj(h, nw[layer, 2], w_ev_in, (i,), batch, seq)
            o_a = _dilated_attention(slabs, cos, sin)
            o_b = _multiscale_pool(slabs, bf(pool_w[i]), pool_scale[i].reshape(1, -1))
            h = _mix_out(o_a.reshape(batch * seq, -1), o_b.reshape(batch * seq, -1), w_ev_out, (i,), h,
                         nw[layer, 3])
        else:
            slabs = _norm_proj(h, nw[layer, 2], w_od_in, (i,), batch, seq)
            kv_cmp = _compress(slabs, jnp.stack([cmp_pe_k[i], cmp_pe_v[i]]),
                               bf(jnp.stack([cmp_w1_k[i], cmp_w1_v[i]])),
                               bf(jnp.stack([cmp_w2_k[i], cmp_w2_v[i]])), C_HEADS)
            o_cmp, sel = _cmp_select(slabs, kv_cmp)
            o_c = _nsa_attention(slabs, sel, o_cmp, cos, sin)
            y_d = _short_conv(slabs, conv_w[i], 26)
            h = _mix_out(o_c.reshape(batch * seq, -1), y_d.reshape(batch * seq, -1), w_od_out, (i,), h,
                         nw[layer, 3])
        h = ffn(h, layer, 1)
    return h.reshape(batch, seq, d)
```

```python
import functools

import jax
import jax.numpy as jnp
from jax import lax
from jax.experimental import pallas as pl
from jax.experimental.pallas import tpu as pltpu

HEAD_DIM = 128
ROPE_THETA = 10000.0
NORM_EPS = 1e-6
LOG2_E = 1.4426950408889634
HALF_STEP = 0.5
LANE = 128

A_HEADS = 12
A_DILATIONS = (1, 4, 16)
A_BACK = 128
B_WINDOWS = (2, 4, 8, 16)
B_GROUPS = 4

C_HEADS = 12
C_KV_HEADS = 2
C_GROUP = C_HEADS // C_KV_HEADS
CMP_BLOCK = 32
CMP_STRIDE = 16
SLC_BLOCK = 64
SLC_TOPN = 8
WIN_SIZE = 512
FORCED_SCORE = 1e9
D_WIDTH = 512
PROJ_SLABS = 40
NORM_SUB_TILES = 4

VMEM_LIMIT = 56 * 1024 * 1024

_BF16 = jnp.bfloat16
_F32 = jnp.float32


def _params(*sem):
    return pltpu.CompilerParams(dimension_semantics=sem, vmem_limit_bytes=VMEM_LIMIT)


def _rms(x, w):
    return x * lax.rsqrt(jnp.mean(x * x, axis=-1, keepdims=True) + NORM_EPS) * w


def _dot(a, b):
    return jnp.dot(a, b, preferred_element_type=_F32)


def _dot_t(a, b):
    return lax.dot_general(a, b, (((1,), (1,)), ((), ())), preferred_element_type=_F32)


def _rope(x, cos, sin):
    return x * cos + pltpu.roll(x, HEAD_DIM // 2, 1) * sin


def _shift_rows(x, k, row):
    return jnp.where(row >= k, pltpu.roll(x, k, 0), 0.0)


def _proj_kernel(h_ref, nw_ref, w_ref, o_ref, xn_ref):
    def use(xn, r):
        acc = _dot(xn, w_ref[...])
        for s in range(o_ref.shape[1]):
            o_ref[0, s, r, :] = acc[:, s * LANE:(s + 1) * LANE]

    _normed_rows(h_ref, nw_ref, xn_ref, use)


def _stacked(block, index_map, lead):
    return pl.BlockSpec((None,) * len(lead) + block, lambda *g: tuple(lead) + index_map(*g))


def _norm_proj(h, nw, w, lead, batch, seq, tm=1024, tn=1024):
    t, d = h.shape
    n = w.shape[-1]
    per_b = seq // tm
    return pl.pallas_call(
        _proj_kernel,
        grid=(t // tm, n // tn),
        in_specs=[pl.BlockSpec((tm, d), lambda i, j: (i, 0)),
                  pl.BlockSpec((1, d), lambda i, j: (0, 0)),
                  _stacked((d, tn), lambda i, j: (0, j), lead)],
        out_specs=pl.BlockSpec((1, tn // LANE, tm, LANE), lambda i, j: (i // per_b, j, i % per_b, 0)),
        out_shape=jax.ShapeDtypeStruct((batch, n // LANE, seq, LANE), _F32),
        scratch_shapes=[pltpu.VMEM((tm, d), _BF16)],
        compiler_params=_params("arbitrary", "arbitrary"),
        name="norm_proj",
    )(h, nw, w)


def _normed_rows(h_ref, nw_ref, xn_ref, use):
    j = pl.program_id(1)

    @pl.when(j == 0)
    def _():
        sub = h_ref.shape[0] // NORM_SUB_TILES
        for s in range(NORM_SUB_TILES):
            r = slice(s * sub, (s + 1) * sub)
            xn = _rms(h_ref[r, :], nw_ref[...]).astype(_BF16)
            xn_ref[r, :] = xn
            use(xn, r)

    @pl.when(j > 0)
    def _():
        use(xn_ref[...], slice(None))


def _ffn_up_kernel(h_ref, nw_ref, wg_ref, wu_ref, o_ref, xn_ref):
    def use(xn, r):
        g = _dot(xn, wg_ref[...])
        u = _dot(xn, wu_ref[...])
        o_ref[r, :] = (jax.nn.silu(g) * u).astype(_BF16)

    _normed_rows(h_ref, nw_ref, xn_ref, use)


def _ffn_up(h, nw, wg, wu, lead, tm=1024, tn=512):
    t, d = h.shape
    f = wg.shape[-1]
    return pl.pallas_call(
        _ffn_up_kernel,
        grid=(t // tm, f // tn),
        in_specs=[pl.BlockSpec((tm, d), lambda i, j: (i, 0)),
                  pl.BlockSpec((1, d), lambda i, j: (0, 0)),
                  _stacked((d, tn), lambda i, j: (0, j), lead),
                  _stacked((d, tn), lambda i, j: (0, j), lead)],
        out_specs=pl.BlockSpec((tm, tn), lambda i, j: (i, j)),
        out_shape=jax.ShapeDtypeStruct((t, f), _BF16),
        scratch_shapes=[pltpu.VMEM((tm, d), _BF16)],
        compiler_params=_params("arbitrary", "arbitrary"),
        name="ffn_up",
    )(h, nw, wg, wu)


def _ffn_down_kernel(a_ref, w_ref, h_ref, nw_ref, o_ref, acc_ref, *, scale, n_sub):
    k = pl.program_id(1)
    last = pl.num_programs(1) - 1

    @pl.when(k == 0)
    def _():
        acc_ref[...] = _dot(a_ref[...], w_ref[...])

    @pl.when((k > 0) & (k < last))
    def _():
        acc_ref[...] += _dot(a_ref[...], w_ref[...])

    @pl.when(k == last)
    def _():
        sub = a_ref.shape[0] // n_sub
        for s in range(n_sub):
            r = slice(s * sub, (s + 1) * sub)
            f = acc_ref[r, :] + _dot(a_ref[r, :], w_ref[...])
            o_ref[r, :] = h_ref[r, :] + scale * _rms(f, nw_ref[...])


def _ffn_down(a, w, lead, h, nw, scale, tm=512, tk=2816, n_sub=2):
    t, f = a.shape
    d = w.shape[-1]
    assert f // tk >= 2, "the first and last contraction steps must be different grid steps"
    return pl.pallas_call(
        functools.partial(_ffn_down_kernel, scale=scale, n_sub=n_sub),
        grid=(t // tm, f // tk),
        in_specs=[pl.BlockSpec((tm, tk), lambda i, k: (i, k)),
                  _stacked((tk, d), lambda i, k: (k, 0), lead),
                  pl.BlockSpec((tm, d), lambda i, k: (i, 0)),
                  pl.BlockSpec((1, d), lambda i, k: (0, 0))],
        out_specs=pl.BlockSpec((tm, d), lambda i, k: (i, 0)),
        out_shape=jax.ShapeDtypeStruct((t, d), _F32),
        scratch_shapes=[pltpu.VMEM((tm, d), _F32)],
        compiler_params=_params("arbitrary", "arbitrary"),
        name="ffn_down",
    )(a, w, h, nw)


def _mix_out_kernel(x1_ref, x2_ref, w_ref, h_ref, nw_ref, o_ref):
    c1 = x1_ref.shape[1]
    sub = x1_ref.shape[0] // 2
    for s in range(2):
        r = slice(s * sub, (s + 1) * sub)
        m = _dot(x1_ref[r, :], w_ref[0:c1, :]) + _dot(x2_ref[r, :], w_ref[c1:, :])
        o_ref[r, :] = h_ref[r, :] + _rms(m, nw_ref[...])


def _mix_out(x1, x2, w, lead, h, nw, tm=512):
    t, c1 = x1.shape
    c2 = x2.shape[1]
    d = w.shape[-1]
    return pl.pallas_call(
        _mix_out_kernel,
        grid=(t // tm,),
        in_specs=[pl.BlockSpec((tm, c1), lambda i: (i, 0)),
                  pl.BlockSpec((tm, c2), lambda i: (i, 0)),
                  _stacked((c1 + c2, d), lambda i: (0, 0), lead),
                  pl.BlockSpec((tm, d), lambda i: (i, 0)),
                  pl.BlockSpec((1, d), lambda i: (0, 0))],
        out_specs=pl.BlockSpec((tm, d), lambda i: (i, 0)),
        out_shape=jax.ShapeDtypeStruct((t, d), _F32),
        compiler_params=_params("arbitrary"),
        name="mix_out",
    )(x1, x2, w, h, nw)


def _rows(start, size, stride):
    return pl.ds(start, size) if stride == 1 else pl.ds(start, size, stride=stride)


def _dilated_kernel(q_ref, k_ref, v_ref, cos_ref, sin_ref, o_ref, qs, ks, s_s, p_s, pv_s, m_s, l_s):
    seq = q_ref.shape[2]
    blk = A_BACK
    scale = HEAD_DIM ** -0.5 * LOG2_E
    chunk = 256

    for c in range(0, seq, chunk):
        r = slice(c, c + chunk)
        cos, sin = cos_ref[0, r, :], sin_ref[0, r, :]
        qs[r, :] = _rope(q_ref[0, 0, r, :], cos, sin) * scale
        ks[r, :] = _rope(k_ref[0, 0, r, :], cos, sin)

    qi = lax.broadcasted_iota(jnp.int32, (blk, blk), 0)
    kj = lax.broadcasted_iota(jnp.int32, (blk, blk), 1)
    cur_ok = kj <= qi
    both_ok = jnp.concatenate([kj >= qi, cur_ok], axis=1)
    ones = jnp.ones((blk, HEAD_DIM), _BF16)

    blocks = []
    for g, dil in enumerate(A_DILATIONS):
        for res in range(dil):
            prev = None
            for n in range(seq // (blk * dil)):
                rows = _rows(res + n * blk * dil, blk, dil)
                blocks.append((g, rows, prev))
                prev = rows

    def keys_bf16(rows):
        return ks[rows, :].astype(_BF16)

    def values1_bf16(rows):
        return jnp.concatenate([v_ref[0, 0, rows, :].astype(_BF16), ones], axis=1)

    for b, (g, rows, prev) in enumerate(blocks):
        qb = qs[rows, :].astype(_BF16)
        if prev is None:
            s_s[b, :, 0:blk] = jnp.where(cur_ok, _dot_t(qb, keys_bf16(rows)), -jnp.inf)
        else:
            k_all = jnp.concatenate([keys_bf16(prev), keys_bf16(rows)], axis=0)
            s_s[b] = jnp.where(both_ok, _dot_t(qb, k_all), -jnp.inf)

    for b, (g, rows, prev) in enumerate(blocks):
        width = blk if prev is None else 2 * blk
        s = s_s[b, :, 0:width]
        m = jnp.max(s, axis=-1, keepdims=True)
        p_s[b, :, 0:width] = jnp.exp2(s - m).astype(_BF16)
        m_s[g, rows, :] = jnp.broadcast_to(m, (blk, HEAD_DIM))

    for b, (g, rows, prev) in enumerate(blocks):
        if prev is None:
            pva = _dot(p_s[b, :, 0:blk], values1_bf16(rows))
        else:
            pva = _dot(p_s[b], jnp.concatenate([values1_bf16(prev), values1_bf16(rows)], axis=0))
        pv_s[g, rows, :] = pva[:, 0:HEAD_DIM]
        l_s[g, rows, :] = pva[:, HEAD_DIM:]

    for c in range(0, seq, chunk):
        r = slice(c, c + chunk)
        m_all = jnp.maximum(jnp.maximum(m_s[0, r, :], m_s[1, r, :]), m_s[2, r, :])
        num = jnp.zeros((chunk, HEAD_DIM), _F32)
        den = jnp.zeros((chunk, HEAD_DIM), _F32)
        for g in range(len(A_DILATIONS)):
            w = jnp.exp2(m_s[g, r, :] - m_all)
            num = num + w * pv_s[g, r, :]
            den = den + w * l_s[g, r, :]
        o_ref[0, r, :] = (num / den).astype(o_ref.dtype)


def _dilated_attention(slabs, cos, sin):
    batch, _, seq, _ = slabs.shape
    head = lambda off: pl.BlockSpec((1, 1, seq, LANE), lambda b, h: (b, off + h, 0, 0))
    table = pl.BlockSpec((1, seq, LANE), lambda b, h: (b, 0, 0))
    nblocks = len(A_DILATIONS) * seq // A_BACK
    return pl.pallas_call(
        _dilated_kernel,
        grid=(batch, A_HEADS),
        in_specs=[head(0), head(A_HEADS), head(2 * A_HEADS), table, table],
        out_specs=pl.BlockSpec((1, seq, LANE), lambda b, h: (b, 0, h)),
        out_shape=jax.ShapeDtypeStruct((batch, seq, A_HEADS * HEAD_DIM), _BF16),
        scratch_shapes=[pltpu.VMEM((seq, HEAD_DIM), _F32), pltpu.VMEM((seq, HEAD_DIM), _F32),
                        pltpu.VMEM((nblocks, A_BACK, 2 * A_BACK), _F32),
                        pltpu.VMEM((nblocks, A_BACK, 2 * A_BACK), _BF16),
                        pltpu.VMEM((3, seq, HEAD_DIM), _F32), pltpu.VMEM((3, seq, HEAD_DIM), _F32),
                        pltpu.VMEM((3, seq, HEAD_DIM), _F32)],
        compiler_params=_params("arbitrary", "arbitrary"),
        name="dilated_attention",
    )(slabs, slabs, slabs, cos, sin)


def _pool_kernel(u_ref, w_ref, sc_ref, o_ref):
    seq = u_ref.shape[2]
    row = lax.broadcasted_iota(jnp.int32, (seq, LANE), 0)
    for g, win in enumerate(B_WINDOWS):
        x = u_ref[0, g]
        s = x
        step = 1
        while step < win:
            s = s + _shift_rows(s, step, row)
            step *= 2
        cnt = jnp.minimum(row + 1, win).astype(_F32)
        pooled = s / cnt - x
        mixed = _dot(pooled.astype(_BF16), w_ref[g]) * sc_ref[:, g * LANE:(g + 1) * LANE]
        o_ref[0, :, g * LANE:(g + 1) * LANE] = mixed.astype(o_ref.dtype)


def _multiscale_pool(slabs, pool_w, pool_scale):
    batch, _, seq, _ = slabs.shape
    first = 3 * A_HEADS // B_GROUPS
    return pl.pallas_call(
        _pool_kernel,
        grid=(batch,),
        in_specs=[pl.BlockSpec((1, B_GROUPS, seq, LANE), lambda b: (b, first, 0, 0)),
                  pl.BlockSpec((B_GROUPS, LANE, LANE), lambda b: (0, 0, 0)),
                  pl.BlockSpec((1, B_GROUPS * LANE), lambda b: (0, 0))],
        out_specs=pl.BlockSpec((1, seq, B_GROUPS * LANE), lambda b: (b, 0, 0)),
        out_shape=jax.ShapeDtypeStruct((batch, seq, B_GROUPS * LANE), _BF16),
        compiler_params=_params("arbitrary"),
        name="multiscale_pool",
    )(slabs, pool_w, pool_scale)


def _compress_kernel(x_ref, pe_ref, w1_ref, w2_ref, o_ref):
    nrow = o_ref.shape[3]
    hidden = w1_ref.shape[2]
    a = jnp.zeros((nrow, hidden), _F32)
    b = jnp.zeros((nrow, hidden), _F32)
    for tok in range(CMP_STRIDE):
        x = x_ref[0, 0, pl.ds(tok, nrow, stride=CMP_STRIDE), :]
        lo, hi = tok, CMP_STRIDE + tok
        a = a + _dot((x + pe_ref[0, lo:lo + 1, :]).astype(_BF16), w1_ref[0, lo * HEAD_DIM:(lo + 1) * HEAD_DIM, :])
        b = b + _dot((x + pe_ref[0, hi:hi + 1, :]).astype(_BF16), w1_ref[0, hi * HEAD_DIM:(hi + 1) * HEAD_DIM, :])
    hid = a + pltpu.roll(b, nrow - 1, 0)
    o_ref[0, 0, 0] = _dot(jax.nn.gelu(hid).astype(_BF16), w2_ref[0])


def _compress(slabs, pe, w1, w2, first_slab):
    batch, _, seq, _ = slabs.shape
    nrow = seq // CMP_STRIDE
    return pl.pallas_call(
        _compress_kernel,
        grid=(2, batch, C_KV_HEADS),
        in_specs=[pl.BlockSpec((1, 1, seq, LANE), lambda kv, b, g: (b, first_slab + 2 * kv + g, 0, 0)),
                  pl.BlockSpec((1, CMP_BLOCK, HEAD_DIM), lambda kv, b, g: (kv, 0, 0)),
                  pl.BlockSpec((1, CMP_BLOCK * HEAD_DIM, w1.shape[2]), lambda kv, b, g: (kv, 0, 0)),
                  pl.BlockSpec((1, w2.shape[1], HEAD_DIM), lambda kv, b, g: (kv, 0, 0))],
        out_specs=pl.BlockSpec((1, 1, 1, nrow, HEAD_DIM), lambda kv, b, g: (kv, b, g, 0, 0)),
        out_shape=jax.ShapeDtypeStruct((2, batch, C_KV_HEADS, nrow, HEAD_DIM), _F32),
        compiler_params=_params("arbitrary", "arbitrary", "arbitrary"),
        name="compress",
    )(slabs, pe, w1, w2)


def _cmp_select_kernel(q_ref, kc_ref, vc_ref, o_ref, sel_ref, *, nslc, n_sub):
    tq = q_ref.shape[2] // n_sub
    ncmp = kc_ref.shape[3]
    scale = HEAD_DIM ** -0.5
    n = lax.broadcasted_iota(jnp.int32, (tq, ncmp), 1)

    cj = lax.broadcasted_iota(jnp.int32, (LANE, ncmp), 0) * SLC_BLOCK
    cn = lax.broadcasted_iota(jnp.int32, (LANE, ncmp), 1) * CMP_STRIDE
    cover_t = ((cn < cj + SLC_BLOCK) & (cn + CMP_BLOCK > cj)).astype(_BF16)
    j = lax.broadcasted_iota(jnp.int32, (nslc, tq), 0)
    jf = j.astype(_F32)

    kc = kc_ref[0, 0, 0].astype(_BF16)
    vc = vc_ref[0, 0, 0].astype(_BF16)
    for u in range(n_sub):
        rows = slice(u * tq, (u + 1) * tq)
        t0 = (pl.program_id(2) * n_sub + u) * tq
        t = t0 + lax.broadcasted_iota(jnp.int32, (tq, ncmp), 0)
        cmask = n * CMP_STRIDE + (CMP_BLOCK - 1) <= t
        imp_t = jnp.zeros((LANE, tq), _F32)
        for hh in range(C_GROUP):
            sc = jnp.where(cmask, _dot_t(q_ref[0, hh, rows, :].astype(_BF16), kc) * scale, -jnp.inf)
            m = jnp.max(sc, axis=-1, keepdims=True)
            m = jnp.where(jnp.isfinite(m), m, 0.0)
            p = jnp.exp(sc - m)
            den = jnp.sum(p, axis=-1, keepdims=True)
            p = (p / jnp.maximum(den, 1.0)).astype(_BF16)
            o_ref[0, hh, rows, :] = _dot(p, vc)
            imp_t = imp_t + _dot_t(cover_t, p)

        tt = t0 + lax.broadcasted_iota(jnp.int32, (nslc, tq), 1)
        cur = lax.shift_right_logical(tt, SLC_BLOCK.bit_length() - 1)
        visible = j <= cur
        forced = visible & ((j == 0) | (j >= cur - 1))
        score = jnp.where(forced, FORCED_SCORE, jnp.where(visible, imp_t[0:nslc, :], -FORCED_SCORE))
        sel_t = jnp.zeros((nslc, tq), _F32)
        for _ in range(min(SLC_TOPN, nslc)):
            best = jnp.max(score, axis=0, keepdims=True)
            pick = jnp.min(jnp.where(score == best, jf, float(nslc)), axis=0, keepdims=True)
            hit = jf == pick
            sel_t = jnp.where(hit, 1.0, sel_t)
            score = jnp.where(hit, -jnp.inf, score)
        sel_t = jnp.concatenate([sel_t, jnp.zeros((LANE - nslc, tq), _F32)], axis=0)
        sel_ref[0, 0, rows, :] = sel_t.T.astype(sel_ref.dtype)


def _cmp_select(slabs, kv_cmp, tq=512, n_sub=2):
    batch, _, seq, _ = slabs.shape
    cmp_spec = lambda which: pl.BlockSpec((1, 1, 1) + kv_cmp.shape[3:], lambda b, g, i: (which, b, g, 0, 0))
    return pl.pallas_call(
        functools.partial(_cmp_select_kernel, nslc=seq // SLC_BLOCK, n_sub=n_sub),
        grid=(batch, C_KV_HEADS, seq // tq),
        in_specs=[pl.BlockSpec((1, C_GROUP, tq, LANE), lambda b, g, i: (b, g, i, 0)),
                  cmp_spec(0), cmp_spec(1)],
        out_specs=[pl.BlockSpec((1, C_GROUP, tq, LANE), lambda b, g, i: (b, g, i, 0)),
                   pl.BlockSpec((1, 1, tq, LANE), lambda b, g, i: (b, g, i, 0))],
        out_shape=[jax.ShapeDtypeStruct((batch, C_HEADS, seq, HEAD_DIM), _F32),
                   jax.ShapeDtypeStruct((batch, C_KV_HEADS, seq, LANE), _BF16)],
        compiler_params=_params("arbitrary", "arbitrary", "arbitrary"),
        name="cmp_select",
    )(slabs, kv_cmp, kv_cmp)


def _group_softmax_pv(q, k, v1, bias_s, s_s, p_s, o_s):
    nk = k.shape[0]
    tq = bias_s.shape[0]
    piece = 512 if nk % 512 == 0 else LANE
    pieces = [slice(c, c + piece) for c in range(0, nk, piece)]
    s = _dot_t(q, k)
    for hh in range(C_GROUP):
        r = slice(hh * tq, (hh + 1) * tq)
        s_s[r, 0:nk] = s[r] + bias_s[:, 0:nk]
    for hh in range(C_GROUP):
        r = slice(hh * tq, (hh + 1) * tq)
        m = None
        for c in pieces:
            mc = jnp.max(s_s[r, c], axis=-1, keepdims=True)
            m = mc if m is None else jnp.maximum(m, mc)
        for c in pieces:
            p_s[r, c] = jnp.exp2(s_s[r, c] - m).astype(_BF16)
    o = _dot(p_s[:, 0:nk], v1)
    o_s[...] = o[:, 0:HEAD_DIM] / o[:, HEAD_DIM:]


def _nsa_kernel(q_ref, ks_ref, vs_ref, kw_ref, vw_ref, sel_ref, oc_ref, gl_ref, cos_ref, sin_ref, o_ref,
                ksr, vs1, kwr, vw1, expand_s, qr, bias_s, s_s, p_s, bias_w, s_w, p_w, oslc_s, owin_s):
    n_sub = qr.shape[0]
    tq = q_ref.shape[2] // n_sub
    seq = ks_ref.shape[2]
    qi = pl.program_id(2)
    chunk = 256
    shift = SLC_BLOCK.bit_length() - 1

    @pl.when(qi == 0)
    def _():
        def prep(c, _):
            r = pl.ds(pl.multiple_of(c * chunk, chunk), chunk)
            cos, sin = cos_ref[0, r, :], sin_ref[0, r, :]
            ones = jnp.ones((chunk, HEAD_DIM), _BF16)
            ksr[r, :] = _rope(ks_ref[0, 0, r, :], cos, sin).astype(_BF16)
            kwr[r, :] = _rope(kw_ref[0, 0, r, :], cos, sin).astype(_BF16)
            vs1[r, 0:HEAD_DIM] = vs_ref[0, 0, r, :].astype(_BF16)
            vs1[r, HEAD_DIM:] = ones
            vw1[r, 0:HEAD_DIM] = vw_ref[0, 0, r, :].astype(_BF16)
            vw1[r, HEAD_DIM:] = ones
            return 0

        lax.fori_loop(0, seq // chunk, prep, 0)
        blk = lax.broadcasted_iota(jnp.int32, (LANE, seq), 0)
        key = lax.broadcasted_iota(jnp.int32, (LANE, seq), 1)
        expand_s[...] = (blk == lax.shift_right_logical(key, shift)).astype(_BF16)

    base = pl.multiple_of(qi * (n_sub * tq), n_sub * tq)
    sub_rows = [slice(u * tq, (u + 1) * tq) for u in range(n_sub)]
    for u in range(n_sub):
        cos_q = cos_ref[0, pl.ds(base + u * tq, tq), :]
        sin_q = sin_ref[0, pl.ds(base + u * tq, tq), :]
        for hh in range(C_GROUP):
            q_rot = _rope(q_ref[0, hh, sub_rows[u], :], cos_q, sin_q)
            qr[u, hh * tq:(hh + 1) * tq, :] = (q_rot * (HEAD_DIM ** -0.5 * LOG2_E)).astype(_BF16)

    def window_branch(u):
        q0 = base + u * tq
        span = WIN_SIZE + tq
        k0 = pl.multiple_of(jnp.maximum(q0 - WIN_SIZE, 0), tq)
        t = q0 + lax.broadcasted_iota(jnp.int32, (tq, span), 0)
        key = k0 + lax.broadcasted_iota(jnp.int32, (tq, span), 1)
        bias_w[u] = jnp.where((key <= t) & (key > t - WIN_SIZE), 0.0, -jnp.inf)
        _group_softmax_pv(qr[u], kwr[pl.ds(k0, span), :], vw1[pl.ds(k0, span), :],
                          bias_w.at[u], s_w.at[u], p_w.at[u], owin_s.at[u])

    n_extent = 4
    step = seq // n_extent
    for nt in range(1, n_extent + 1):
        @pl.when(base // step + 1 == nt)
        def _(nk=nt * step):
            for u in range(n_sub):
                for c in range(0, nk, step):
                    t = base + u * tq + lax.broadcasted_iota(jnp.int32, (tq, step), 0)
                    key = c + lax.broadcasted_iota(jnp.int32, (tq, step), 1)
                    ok = (_dot(sel_ref[0, 0, sub_rows[u], :], expand_s[:, c:c + step]) > 0.5) & (key <= t)
                    bias_s[u, :, c:c + step] = jnp.where(ok, 0.0, -jnp.inf)
                _group_softmax_pv(qr[u], ksr[0:nk, :], vs1[0:nk, :],
                                  bias_s.at[u], s_s.at[u], p_s.at[u], oslc_s.at[u])
                window_branch(u)

    gates = jax.nn.sigmoid(gl_ref[0, 0])
    for u in range(n_sub):
        for hh in range(C_GROUP):
            r = slice(hh * tq, (hh + 1) * tq)
            g_cmp, g_slc, g_win = (gates[sub_rows[u], 3 * hh + c:3 * hh + c + 1] for c in range(3))
            o = g_cmp * oc_ref[0, hh, sub_rows[u], :] + g_slc * oslc_s[u, r, :] + g_win * owin_s[u, r, :]
            o_ref[0, sub_rows[u], hh * HEAD_DIM:(hh + 1) * HEAD_DIM] = o.astype(o_ref.dtype)


def _nsa_attention(slabs, sel, o_cmp, cos, sin, tq=128, n_sub=2):
    batch, _, seq, _ = slabs.shape
    kv = lambda off: pl.BlockSpec((1, 1, seq, LANE), lambda b, g, i: (b, off + g, 0, 0))
    table = pl.BlockSpec((1, seq, LANE), lambda b, g, i: (b, 0, 0))
    rows = C_GROUP * tq
    span = WIN_SIZE + tq
    tile = n_sub * tq
    assert (seq // 4) % tile == 0, "the sub-tiles of a step must share one selected-branch extent"
    return pl.pallas_call(
        _nsa_kernel,
        grid=(batch, C_KV_HEADS, seq // tile),
        in_specs=[pl.BlockSpec((1, C_GROUP, tile, LANE), lambda b, g, i: (b, g, i, 0)),
                  kv(16), kv(18), kv(20), kv(22),
                  pl.BlockSpec((1, 1, tile, LANE), lambda b, g, i: (b, g, i, 0)),
                  pl.BlockSpec((1, C_GROUP, tile, LANE), lambda b, g, i: (b, g, i, 0)),
                  pl.BlockSpec((1, 1, tile, LANE), lambda b, g, i: (b, 24 + g, i, 0)),
                  table, table],
        out_specs=pl.BlockSpec((1, tile, C_GROUP * HEAD_DIM), lambda b, g, i: (b, i, g)),
        out_shape=jax.ShapeDtypeStruct((batch, seq, C_HEADS * HEAD_DIM), _BF16),
        scratch_shapes=[pltpu.VMEM((seq, HEAD_DIM), _BF16), pltpu.VMEM((seq, 2 * HEAD_DIM), _BF16)] * 2
        + [pltpu.VMEM((LANE, seq), _BF16), pltpu.VMEM((n_sub, rows, HEAD_DIM), _BF16),
           pltpu.VMEM((n_sub, tq, seq), _F32), pltpu.VMEM((n_sub, rows, seq), _F32),
           pltpu.VMEM((n_sub, rows, seq), _BF16),
           pltpu.VMEM((n_sub, tq, span), _F32), pltpu.VMEM((n_sub, rows, span), _F32),
           pltpu.VMEM((n_sub, rows, span), _BF16)]
        + [pltpu.VMEM((n_sub, rows, HEAD_DIM), _F32)] * 2,
        compiler_params=_params("arbitrary", "arbitrary", "arbitrary"),
        name="nsa_attention",
    )(slabs, slabs, slabs, slabs, slabs, sel, o_cmp, slabs, cos, sin)


def _conv_kernel(u_ref, c_ref, b_ref, w_ref, o_ref):
    seq = u_ref.shape[2]
    row = lax.broadcasted_iota(jnp.int32, (seq, LANE), 0)
    u = c_ref[0, 0] * u_ref[0, 0]
    taps = w_ref.shape[0]
    conv = _shift_rows(u, taps - 1, row) * w_ref[0:1, :]
    for j in range(1, taps):
        shifted = u if j == taps - 1 else _shift_rows(u, taps - 1 - j, row)
        conv = conv + shifted * w_ref[j:j + 1, :]
    o_ref[0] = (b_ref[0, 0] * conv).astype(o_ref.dtype)


def _short_conv(slabs, conv_w, first_slab):
    batch, _, seq, _ = slabs.shape
    nd = D_WIDTH // LANE
    part = lambda off: pl.BlockSpec((1, 1, seq, LANE), lambda b, j: (b, first_slab + off + j, 0, 0))
    return pl.pallas_call(
        _conv_kernel,
        grid=(batch, nd),
        in_specs=[part(0), part(nd), part(2 * nd),
                  pl.BlockSpec((conv_w.shape[0], LANE), lambda b, j: (0, j))],
        out_specs=pl.BlockSpec((1, seq, LANE), lambda b, j: (b, 0, j)),
        out_shape=jax.ShapeDtypeStruct((batch, seq, D_WIDTH), _BF16),
        compiler_params=_params("arbitrary", "arbitrary"),
        name="short_conv",
    )(slabs, slabs, slabs, conv_w)


def _pad_cols(w, total):
    return jnp.pad(w, ((0, 0), (0, total - w.shape[1])))


def _odd_in_weight(w):
    qkv = C_HEADS * HEAD_DIM + 6 * C_KV_HEADS * HEAD_DIM
    ngate = 3 * C_GROUP
    gates = [_pad_cols(w[:, qkv + g * ngate:qkv + (g + 1) * ngate], LANE) for g in range(C_KV_HEADS)]
    rest = w[:, qkv + C_KV_HEADS * ngate:]
    return _pad_cols(jnp.concatenate([w[:, :qkv]] + gates + [rest], axis=1), PROJ_SLABS * LANE)


def kernel(x, positions, norm_w, ffn_w_gate, ffn_w_up, ffn_w_down, ev_w_in, ev_w_out, pool_w, pool_scale,
           od_w_in, od_w_out, cmp_pe_k, cmp_w1_k, cmp_w2_k, cmp_pe_v, cmp_w1_v, cmp_w2_v, conv_w):
    batch, seq, d = x.shape
    depth = norm_w.shape[0]
    bf = lambda a: a.astype(_BF16)

    inv_freq = 1.0 / (ROPE_THETA ** (jnp.arange(0, HEAD_DIM, 2, dtype=_F32) / HEAD_DIM))
    ang = positions.astype(_F32)[..., None] * inv_freq
    cos = jnp.concatenate([jnp.cos(ang), jnp.cos(ang)], axis=-1)
    sin = jnp.concatenate([-jnp.sin(ang), jnp.sin(ang)], axis=-1)

    nw = norm_w.reshape(depth, 6, 1, d)
    h = x.reshape(batch * seq, d)

    w_gate, w_up, w_down = bf(ffn_w_gate), bf(ffn_w_up), bf(ffn_w_down)
    w_ev_in, w_ev_out = bf(ev_w_in), bf(ev_w_out)
    w_od_in, w_od_out = bf(jax.vmap(_odd_in_weight)(od_w_in)), bf(od_w_out)

    def ffn(h, layer, which):
        a = _ffn_up(h, nw[layer, 4 * which], w_gate, w_up, (layer, which))
        return _ffn_down(a, w_down, (layer, which), h, nw[layer, 4 * which + 1], HALF_STEP)

    for layer in range(depth):
        i = layer // 2
        h = ffn(h, layer, 0)
        if layer % 2 == 0:
            slabs = _norm_proj(h, nw[layer, 2], w_ev_in, (i,), batch, seq)
            o_a = _dilated_attention(slabs, cos, sin)
            o_b = _multiscale_pool(slabs, bf(pool_w[i]), pool_scale[i].reshape(1, -1))
            h = _mix_out(o_a.reshape(batch * seq, -1), o_b.reshape(batch * seq, -1), w_ev_out, (i,), h,
                         nw[layer, 3])
        else:
            slabs = _norm_proj(h, nw[layer, 2], w_od_in, (i,), batch, seq)
            kv_cmp = _compress(slabs, jnp.stack([cmp_pe_k[i], cmp_pe_v[i]]),
                               bf(jnp.stack([cmp_w1_k[i], cmp_w1_v[i]])),
                               bf(jnp.stack([cmp_w2_k[i], cmp_w2_v[i]])), C_HEADS)
            o_cmp, sel = _cmp_select(slabs, kv_cmp)
            o_c = _nsa_attention(slabs, sel, o_cmp, cos, sin)
            y_d = _short_conv(slabs, conv_w[i], 26)
            h = _mix_out(o_c.reshape(batch * seq, -1), y_d.reshape(batch * seq, -1), w_od_out, (i,), h,
                         nw[layer, 3])
        h = ffn(h, layer, 1)
    return h.reshape(batch, seq, d)
```

```python
import functools

import jax
import jax.numpy as jnp
from jax import lax
from jax.experimental import pallas as pl
from jax.experimental.pallas import tpu as pltpu

HEAD_DIM = 128
ROPE_THETA = 10000.0
NORM_EPS = 1e-6
LOG2_E = 1.4426950408889634
HALF_STEP = 0.5
LANE = 128

A_HEADS = 12
A_DILATIONS = (1, 4, 16)
A_BACK = 128
B_WINDOWS = (2, 4, 8, 16)
B_GROUPS = 4

C_HEADS = 12
C_KV_HEADS = 2
C_GROUP = C_HEADS // C_KV_HEADS
CMP_BLOCK = 32
CMP_STRIDE = 16
SLC_BLOCK = 64
SLC_TOPN = 8
WIN_SIZE = 512
FORCED_SCORE = 1e9
D_WIDTH = 512
PROJ_SLABS = 40
NORM_SUB_TILES = 4

VMEM_LIMIT = 56 * 1024 * 1024

_BF16 = jnp.bfloat16
_F32 = jnp.float32


def _params(*sem):
    return pltpu.CompilerParams(dimension_semantics=sem, vmem_limit_bytes=VMEM_LIMIT)


def _rms(x, w):
    return x * lax.rsqrt(jnp.mean(x * x, axis=-1, keepdims=True) + NORM_EPS) * w


def _dot(a, b):
    return jnp.dot(a, b, preferred_element_type=_F32)


def _dot_t(a, b):
    return lax.dot_general(a, b, (((1,), (1,)), ((), ())), preferred_element_type=_F32)


def _rope(x, cos, sin):
    return x * cos + pltpu.roll(x, HEAD_DIM // 2, 1) * sin


def _shift_rows(x, k, row):
    return jnp.where(row >= k, pltpu.roll(x, k, 0), 0.0)


def _proj_kernel(h_ref, nw_ref, w_ref, o_ref, xn_ref):
    def use(xn, r):
        acc = _dot(xn, w_ref[...])
        for s in range(o_ref.shape[1]):
            o_ref[0, s, r, :] = acc[:, s * LANE:(s + 1) * LANE]

    _normed_rows(h_ref, nw_ref, xn_ref, use)


def _stacked(block, index_map, lead):
    return pl.BlockSpec((None,) * len(lead) + block, lambda *g: tuple(lead) + index_map(*g))


def _norm_proj(h, nw, w, lead, batch, seq, tm=1024, tn=1024):
    t, d = h.shape
    n = w.shape[-1]
    per_b = seq // tm
    return pl.pallas_call(
        _proj_kernel,
        grid=(t // tm, n // tn),
        in_specs=[pl.BlockSpec((tm, d), lambda i, j: (i, 0)),
                  pl.BlockSpec((1, d), lambda i, j: (0, 0)),
                  _stacked((d, tn), lambda i, j: (0, j), lead)],
        out_specs=pl.BlockSpec((1, tn // LANE, tm, LANE), lambda i, j: (i // per_b, j, i % per_b, 0)),
        out_shape=jax.ShapeDtypeStruct((batch, n // LANE, seq, LANE), _F32),
        scratch_shapes=[pltpu.VMEM((tm, d), _BF16)],
        compiler_params=_params("arbitrary", "arbitrary"),
        name="norm_proj",
    )(h, nw, w)


def _normed_rows(h_ref, nw_ref, xn_ref, use):
    j = pl.program_id(1)

    @pl.when(j == 0)
    def _():
        sub = h_ref.shape[0] // NORM_SUB_TILES
        for s in range(NORM_SUB_TILES):
            r = slice(s * sub, (s + 1) * sub)
            xn = _rms(h_ref[r, :], nw_ref[...]).astype(_BF16)
            xn_ref[r, :] = xn
            use(xn, r)

    @pl.when(j > 0)
    def _():
        use(xn_ref[...], slice(None))


def _ffn_up_kernel(h_ref, nw_ref, wg_ref, wu_ref, o_ref, xn_ref):
    def use(xn, r):
        g = _dot(xn, wg_ref[...])
        u = _dot(xn, wu_ref[...])
        o_ref[r, :] = (jax.nn.silu(g) * u).astype(_BF16)

    _normed_rows(h_ref, nw_ref, xn_ref, use)


def _ffn_up(h, nw, wg, wu, lead, tm=1024, tn=512):
    t, d = h.shape
    f = wg.shape[-1]
    return pl.pallas_call(
        _ffn_up_kernel,
        grid=(t // tm, f // tn),
        in_specs=[pl.BlockSpec((tm, d), lambda i, j: (i, 0)),
                  pl.BlockSpec((1, d), lambda i, j: (0, 0)),
                  _stacked((d, tn), lambda i, j: (0, j), lead),
                  _stacked((d, tn), lambda i, j: (0, j), lead)],
        out_specs=pl.BlockSpec((tm, tn), lambda i, j: (i, j)),
        out_shape=jax.ShapeDtypeStruct((t, f), _BF16),
        scratch_shapes=[pltpu.VMEM((tm, d), _BF16)],
        compiler_params=_params("arbitrary", "arbitrary"),
        name="ffn_up",
    )(h, nw, wg, wu)


def _ffn_down_kernel(a_ref, w_ref, h_ref, nw_ref, o_ref, *, scale, n_sub):
    sub = a_ref.shape[0] // n_sub
    for s in range(n_sub):
        r = slice(s * sub, (s + 1) * sub)
        f = _dot(a_ref[r, :], w_ref[...])
        o_ref[r, :] = h_ref[r, :] + scale * _rms(f, nw_ref[...])


def _ffn_down(a, w, lead, h, nw, scale, tm=512, n_sub=2):
    t, f = a.shape
    d = w.shape[-1]
    w_spec = pl.BlockSpec((None,) * len(lead) + (f, d), lambda i: tuple(lead) + (0, 0),
                          pipeline_mode=pl.Buffered(1))
    return pl.pallas_call(
        functools.partial(_ffn_down_kernel, scale=scale, n_sub=n_sub),
        grid=(t // tm,),
        in_specs=[pl.BlockSpec((tm, f), lambda i: (i, 0)),
                  w_spec,
                  pl.BlockSpec((tm, d), lambda i: (i, 0)),
                  pl.BlockSpec((1, d), lambda i: (0, 0))],
        out_specs=pl.BlockSpec((tm, d), lambda i: (i, 0)),
        out_shape=jax.ShapeDtypeStruct((t, d), _F32),
        compiler_params=_params("arbitrary"),
        name="ffn_down",
    )(a, w, h, nw)


def _mix_out_kernel(x1_ref, x2_ref, w_ref, h_ref, nw_ref, o_ref):
    c1 = x1_ref.shape[1]
    sub = x1_ref.shape[0] // 2
    for s in range(2):
        r = slice(s * sub, (s + 1) * sub)
        m = _dot(x1_ref[r, :], w_ref[0:c1, :]) + _dot(x2_ref[r, :], w_ref[c1:, :])
        o_ref[r, :] = h_ref[r, :] + _rms(m, nw_ref[...])


def _mix_out(x1, x2, w, lead, h, nw, tm=512):
    t, c1 = x1.shape
    c2 = x2.shape[1]
    d = w.shape[-1]
    return pl.pallas_call(
        _mix_out_kernel,
        grid=(t // tm,),
        in_specs=[pl.BlockSpec((tm, c1), lambda i: (i, 0)),
                  pl.BlockSpec((tm, c2), lambda i: (i, 0)),
                  _stacked((c1 + c2, d), lambda i: (0, 0), lead),
                  pl.BlockSpec((tm, d), lambda i: (i, 0)),
                  pl.BlockSpec((1, d), lambda i: (0, 0))],
        out_specs=pl.BlockSpec((tm, d), lambda i: (i, 0)),
        out_shape=jax.ShapeDtypeStruct((t, d), _F32),
        compiler_params=_params("arbitrary"),
        name="mix_out",
    )(x1, x2, w, h, nw)


def _rows(start, size, stride):
    return pl.ds(start, size) if stride == 1 else pl.ds(start, size, stride=stride)


def _dilated_kernel(q_ref, k_ref, v_ref, cos_ref, sin_ref, o_ref, qs, ks, s_s, p_s, pv_s, m_s, l_s):
    seq = q_ref.shape[2]
    blk = A_BACK
    scale = HEAD_DIM ** -0.5 * LOG2_E
    chunk = 256

    for c in range(0, seq, chunk):
        r = slice(c, c + chunk)
        cos, sin = cos_ref[0, r, :], sin_ref[0, r, :]
        qs[r, :] = _rope(q_ref[0, 0, r, :], cos, sin) * scale
        ks[r, :] = _rope(k_ref[0, 0, r, :], cos, sin)

    qi = lax.broadcasted_iota(jnp.int32, (blk, blk), 0)
    kj = lax.broadcasted_iota(jnp.int32, (blk, blk), 1)
    cur_ok = kj <= qi
    both_ok = jnp.concatenate([kj >= qi, cur_ok], axis=1)
    ones = jnp.ones((blk, HEAD_DIM), _BF16)

    blocks = []
    for g, dil in enumerate(A_DILATIONS):
        for res in range(dil):
            prev = None
            for n in range(seq // (blk * dil)):
                rows = _rows(res + n * blk * dil, blk, dil)
                blocks.append((g, rows, prev))
                prev = rows

    def keys_bf16(rows):
        return ks[rows, :].astype(_BF16)

    def values1_bf16(rows):
        return jnp.concatenate([v_ref[0, 0, rows, :].astype(_BF16), ones], axis=1)

    for b, (g, rows, prev) in enumerate(blocks):
        qb = qs[rows, :].astype(_BF16)
        if prev is None:
            s_s[b, :, 0:blk] = jnp.where(cur_ok, _dot_t(qb, keys_bf16(rows)), -jnp.inf)
        else:
            k_all = jnp.concatenate([keys_bf16(prev), keys_bf16(rows)], axis=0)
            s_s[b] = jnp.where(both_ok, _dot_t(qb, k_all), -jnp.inf)

    for b, (g, rows, prev) in enumerate(blocks):
        width = blk if prev is None else 2 * blk
        s = s_s[b, :, 0:width]
        m = jnp.max(s, axis=-1, keepdims=True)
        p_s[b, :, 0:width] = jnp.exp2(s - m).astype(_BF16)
        m_s[g, rows, :] = jnp.broadcast_to(m, (blk, HEAD_DIM))

    for b, (g, rows, prev) in enumerate(blocks):
        if prev is None:
            pva = _dot(p_s[b, :, 0:blk], values1_bf16(rows))
        else:
            pva = _dot(p_s[b], jnp.concatenate([values1_bf16(prev), values1_bf16(rows)], axis=0))
        pv_s[g, rows, :] = pva[:, 0:HEAD_DIM]
        l_s[g, rows, :] = pva[:, HEAD_DIM:]

    for c in range(0, seq, chunk):
        r = slice(c, c + chunk)
        m_all = jnp.maximum(jnp.maximum(m_s[0, r, :], m_s[1, r, :]), m_s[2, r, :])
        num = jnp.zeros((chunk, HEAD_DIM), _F32)
        den = jnp.zeros((chunk, HEAD_DIM), _F32)
        for g in range(len(A_DILATIONS)):
            w = jnp.exp2(m_s[g, r, :] - m_all)
            num = num + w * pv_s[g, r, :]
            den = den + w * l_s[g, r, :]
        o_ref[0, r, :] = (num / den).astype(o_ref.dtype)


def _dilated_attention(slabs, cos, sin):
    batch, _, seq, _ = slabs.shape
    head = lambda off: pl.BlockSpec((1, 1, seq, LANE), lambda b, h: (b, off + h, 0, 0))
    table = pl.BlockSpec((1, seq, LANE), lambda b, h: (b, 0, 0))
    nblocks = len(A_DILATIONS) * seq // A_BACK
    return pl.pallas_call(
        _dilated_kernel,
        grid=(batch, A_HEADS),
        in_specs=[head(0), head(A_HEADS), head(2 * A_HEADS), table, table],
        out_specs=pl.BlockSpec((1, seq, LANE), lambda b, h: (b, 0, h)),
        out_shape=jax.ShapeDtypeStruct((batch, seq, A_HEADS * HEAD_DIM), _BF16),
        scratch_shapes=[pltpu.VMEM((seq, HEAD_DIM), _F32), pltpu.VMEM((seq, HEAD_DIM), _F32),
                        pltpu.VMEM((nblocks, A_BACK, 2 * A_BACK), _F32),
                        pltpu.VMEM((nblocks, A_BACK, 2 * A_BACK), _BF16),
                        pltpu.VMEM((3, seq, HEAD_DIM), _F32), pltpu.VMEM((3, seq, HEAD_DIM), _F32),
                        pltpu.VMEM((3, seq, HEAD_DIM), _F32)],
        compiler_params=_params("arbitrary", "arbitrary"),
        name="dilated_attention",
    )(slabs, slabs, slabs, cos, sin)


def _pool_kernel(u_ref, w_ref, sc_ref, o_ref):
    seq = u_ref.shape[2]
    row = lax.broadcasted_iota(jnp.int32, (seq, LANE), 0)
    for g, win in enumerate(B_WINDOWS):
        x = u_ref[0, g]
        s = x
        step = 1
        while step < win:
            s = s + _shift_rows(s, step, row)
            step *= 2
        cnt = jnp.minimum(row + 1, win).astype(_F32)
        pooled = s / cnt - x
        mixed = _dot(pooled.astype(_BF16), w_ref[g]) * sc_ref[:, g * LANE:(g + 1) * LANE]
        o_ref[0, :, g * LANE:(g + 1) * LANE] = mixed.astype(o_ref.dtype)


def _multiscale_pool(slabs, pool_w, pool_scale):
    batch, _, seq, _ = slabs.shape
    first = 3 * A_HEADS // B_GROUPS
    return pl.pallas_call(
        _pool_kernel,
        grid=(batch,),
        in_specs=[pl.BlockSpec((1, B_GROUPS, seq, LANE), lambda b: (b, first, 0, 0)),
                  pl.BlockSpec((B_GROUPS, LANE, LANE), lambda b: (0, 0, 0)),
                  pl.BlockSpec((1, B_GROUPS * LANE), lambda b: (0, 0))],
        out_specs=pl.BlockSpec((1, seq, B_GROUPS * LANE), lambda b: (b, 0, 0)),
        out_shape=jax.ShapeDtypeStruct((batch, seq, B_GROUPS * LANE), _BF16),
        compiler_params=_params("arbitrary"),
        name="multiscale_pool",
    )(slabs, pool_w, pool_scale)


def _compress_kernel(x_ref, pe_ref, w1_ref, w2_ref, o_ref):
    nrow = o_ref.shape[3]
    hidden = w1_ref.shape[2]
    a = jnp.zeros((nrow, hidden), _F32)
    b = jnp.zeros((nrow, hidden), _F32)
    for tok in range(CMP_STRIDE):
        x = x_ref[0, 0, pl.ds(tok, nrow, stride=CMP_STRIDE), :]
        lo, hi = tok, CMP_STRIDE + tok
        a = a + _dot((x + pe_ref[0, lo:lo + 1, :]).astype(_BF16), w1_ref[0, lo * HEAD_DIM:(lo + 1) * HEAD_DIM, :])
        b = b + _dot((x + pe_ref[0, hi:hi + 1, :]).astype(_BF16), w1_ref[0, hi * HEAD_DIM:(hi + 1) * HEAD_DIM, :])
    hid = a + pltpu.roll(b, nrow - 1, 0)
    o_ref[0, 0, 0] = _dot(jax.nn.gelu(hid).astype(_BF16), w2_ref[0])


def _compress(slabs, pe, w1, w2, first_slab):
    batch, _, seq, _ = slabs.shape
    nrow = seq // CMP_STRIDE
    return pl.pallas_call(
        _compress_kernel,
        grid=(2, batch, C_KV_HEADS),
        in_specs=[pl.BlockSpec((1, 1, seq, LANE), lambda kv, b, g: (b, first_slab + 2 * kv + g, 0, 0)),
                  pl.BlockSpec((1, CMP_BLOCK, HEAD_DIM), lambda kv, b, g: (kv, 0, 0)),
                  pl.BlockSpec((1, CMP_BLOCK * HEAD_DIM, w1.shape[2]), lambda kv, b, g: (kv, 0, 0)),
                  pl.BlockSpec((1, w2.shape[1], HEAD_DIM), lambda kv, b, g: (kv, 0, 0))],
        out_specs=pl.BlockSpec((1, 1, 1, nrow, HEAD_DIM), lambda kv, b, g: (kv, b, g, 0, 0)),
        out_shape=jax.ShapeDtypeStruct((2, batch, C_KV_HEADS, nrow, HEAD_DIM), _F32),
        compiler_params=_params("arbitrary", "arbitrary", "arbitrary"),
        name="compress",
    )(slabs, pe, w1, w2)


def _cmp_select_kernel(q_ref, kc_ref, vc_ref, o_ref, sel_ref, *, nslc, n_sub):
    tq = q_ref.shape[2] // n_sub
    ncmp = kc_ref.shape[3]
    scale = HEAD_DIM ** -0.5
    n = lax.broadcasted_iota(jnp.int32, (tq, ncmp), 1)

    cj = lax.broadcasted_iota(jnp.int32, (LANE, ncmp), 0) * SLC_BLOCK
    cn = lax.broadcasted_iota(jnp.int32, (LANE, ncmp), 1) * CMP_STRIDE
    cover_t = ((cn < cj + SLC_BLOCK) & (cn + CMP_BLOCK > cj)).astype(_BF16)
    j = lax.broadcasted_iota(jnp.int32, (nslc, tq), 0)
    jf = j.astype(_F32)

    kc = kc_ref[0, 0, 0].astype(_BF16)
    vc = vc_ref[0, 0, 0].astype(_BF16)
    for u in range(n_sub):
        rows = slice(u * tq, (u + 1) * tq)
        t0 = (pl.program_id(2) * n_sub + u) * tq
        t = t0 + lax.broadcasted_iota(jnp.int32, (tq, ncmp), 0)
        cmask = n * CMP_STRIDE + (CMP_BLOCK - 1) <= t
        imp_t = jnp.zeros((LANE, tq), _F32)
        for hh in range(C_GROUP):
            sc = jnp.where(cmask, _dot_t(q_ref[0, hh, rows, :].astype(_BF16), kc) * scale, -jnp.inf)
            m = jnp.max(sc, axis=-1, keepdims=True)
            m = jnp.where(jnp.isfinite(m), m, 0.0)
            p = jnp.exp(sc - m)
            den = jnp.sum(p, axis=-1, keepdims=True)
            p = (p / jnp.maximum(den, 1.0)).astype(_BF16)
            o_ref[0, hh, rows, :] = _dot(p, vc)
            imp_t = imp_t + _dot_t(cover_t, p)

        tt = t0 + lax.broadcasted_iota(jnp.int32, (nslc, tq), 1)
        cur = lax.shift_right_logical(tt, SLC_BLOCK.bit_length() - 1)
        visible = j <= cur
        forced = visible & ((j == 0) | (j >= cur - 1))
        score = jnp.where(forced, FORCED_SCORE, jnp.where(visible, imp_t[0:nslc, :], -FORCED_SCORE))
        sel_t = jnp.zeros((nslc, tq), _F32)
        for _ in range(min(SLC_TOPN, nslc)):
            best = jnp.max(score, axis=0, keepdims=True)
            pick = jnp.min(jnp.where(score == best, jf, float(nslc)), axis=0, keepdims=True)
            hit = jf == pick
            sel_t = jnp.where(hit, 1.0, sel_t)
            score = jnp.where(hit, -jnp.inf, score)
        sel_t = jnp.concatenate([sel_t, jnp.zeros((LANE - nslc, tq), _F32)], axis=0)
        sel_ref[0, 0, rows, :] = sel_t.T.astype(sel_ref.dtype)


def _cmp_select(slabs, kv_cmp, tq=512, n_sub=2):
    batch, _, seq, _ = slabs.shape
    cmp_spec = lambda which: pl.BlockSpec((1, 1, 1) + kv_cmp.shape[3:], lambda b, g, i: (which, b, g, 0, 0))
    return pl.pallas_call(
        functools.partial(_cmp_select_kernel, nslc=seq // SLC_BLOCK, n_sub=n_sub),
        grid=(batch, C_KV_HEADS, seq // tq),
        in_specs=[pl.BlockSpec((1, C_GROUP, tq, LANE), lambda b, g, i: (b, g, i, 0)),
                  cmp_spec(0), cmp_spec(1)],
        out_specs=[pl.BlockSpec((1, C_GROUP, tq, LANE), lambda b, g, i: (b, g, i, 0)),
                   pl.BlockSpec((1, 1, tq, LANE), lambda b, g, i: (b, g, i, 0))],
        out_shape=[jax.ShapeDtypeStruct((batch, C_HEADS, seq, HEAD_DIM), _F32),
                   jax.ShapeDtypeStruct((batch, C_KV_HEADS, seq, LANE), _BF16)],
        compiler_params=_params("arbitrary", "arbitrary", "arbitrary"),
        name="cmp_select",
    )(slabs, kv_cmp, kv_cmp)


def _group_softmax_pv(q, k, v1, bias_s, s_s, p_s, o_s):
    nk = k.shape[0]
    tq = bias_s.shape[0]
    piece = 512 if nk % 512 == 0 else LANE
    pieces = [slice(c, c + piece) for c in range(0, nk, piece)]
    s = _dot_t(q, k)
    for hh in range(C_GROUP):
        r = slice(hh * tq, (hh + 1) * tq)
        s_s[r, 0:nk] = s[r] + bias_s[:, 0:nk]
    for hh in range(C_GROUP):
        r = slice(hh * tq, (hh + 1) * tq)
        m = None
        for c in pieces:
            mc = jnp.max(s_s[r, c], axis=-1, keepdims=True)
            m = mc if m is None else jnp.maximum(m, mc)
        for c in pieces:
            p_s[r, c] = jnp.exp2(s_s[r, c] - m).astype(_BF16)
    o = _dot(p_s[:, 0:nk], v1)
    o_s[...] = o[:, 0:HEAD_DIM] / o[:, HEAD_DIM:]


def _nsa_kernel(q_ref, ks_ref, vs_ref, kw_ref, vw_ref, sel_ref, oc_ref, gl_ref, cos_ref, sin_ref, o_ref,
                ksr, vs1, kwr, vw1, expand_s, qr, bias_s, s_s, p_s, bias_w, s_w, p_w, oslc_s, owin_s):
    n_sub = qr.shape[0]
    tq = q_ref.shape[2] // n_sub
    seq = ks_ref.shape[2]
    qi = pl.program_id(2)
    chunk = 256
    shift = SLC_BLOCK.bit_length() - 1

    @pl.when(qi == 0)
    def _():
        def prep(c, _):
            r = pl.ds(pl.multiple_of(c * chunk, chunk), chunk)
            cos, sin = cos_ref[0, r, :], sin_ref[0, r, :]
            ones = jnp.ones((chunk, HEAD_DIM), _BF16)
            ksr[r, :] = _rope(ks_ref[0, 0, r, :], cos, sin).astype(_BF16)
            kwr[r, :] = _rope(kw_ref[0, 0, r, :], cos, sin).astype(_BF16)
            vs1[r, 0:HEAD_DIM] = vs_ref[0, 0, r, :].astype(_BF16)
            vs1[r, HEAD_DIM:] = ones
            vw1[r, 0:HEAD_DIM] = vw_ref[0, 0, r, :].astype(_BF16)
            vw1[r, HEAD_DIM:] = ones
            return 0

        lax.fori_loop(0, seq // chunk, prep, 0)
        blk = lax.broadcasted_iota(jnp.int32, (LANE, seq), 0)
        key = lax.broadcasted_iota(jnp.int32, (LANE, seq), 1)
        expand_s[...] = (blk == lax.shift_right_logical(key, shift)).astype(_BF16)

    base = pl.multiple_of(qi * (n_sub * tq), n_sub * tq)
    sub_rows = [slice(u * tq, (u + 1) * tq) for u in range(n_sub)]
    for u in range(n_sub):
        cos_q = cos_ref[0, pl.ds(base + u * tq, tq), :]
        sin_q = sin_ref[0, pl.ds(base + u * tq, tq), :]
        for hh in range(C_GROUP):
            q_rot = _rope(q_ref[0, hh, sub_rows[u], :], cos_q, sin_q)
            qr[u, hh * tq:(hh + 1) * tq, :] = (q_rot * (HEAD_DIM ** -0.5 * LOG2_E)).astype(_BF16)

    def window_branch(u):
        q0 = base + u * tq
        span = WIN_SIZE + tq
        k0 = pl.multiple_of(jnp.maximum(q0 - WIN_SIZE, 0), tq)
        t = q0 + lax.broadcasted_iota(jnp.int32, (tq, span), 0)
        key = k0 + lax.broadcasted_iota(jnp.int32, (tq, span), 1)
        bias_w[u] = jnp.where((key <= t) & (key > t - WIN_SIZE), 0.0, -jnp.inf)
        _group_softmax_pv(qr[u], kwr[pl.ds(k0, span), :], vw1[pl.ds(k0, span), :],
                          bias_w.at[u], s_w.at[u], p_w.at[u], owin_s.at[u])

    n_extent = 4
    step = seq // n_extent
    for nt in range(1, n_extent + 1):
        @pl.when(base // step + 1 == nt)
        def _(nk=nt * step):
            for u in range(n_sub):
                for c in range(0, nk, step):
                    t = base + u * tq + lax.broadcasted_iota(jnp.int32, (tq, step), 0)
                    key = c + lax.broadcasted_iota(jnp.int32, (tq, step), 1)
                    ok = (_dot(sel_ref[0, 0, sub_rows[u], :], expand_s[:, c:c + step]) > 0.5) & (key <= t)
                    bias_s[u, :, c:c + step] = jnp.where(ok, 0.0, -jnp.inf)
                _group_softmax_pv(qr[u], ksr[0:nk, :], vs1[0:nk, :],
                                  bias_s.at[u], s_s.at[u], p_s.at[u], oslc_s.at[u])
                window_branch(u)

    gates = jax.nn.sigmoid(gl_ref[0, 0])
    for u in range(n_sub):
        for hh in range(C_GROUP):
            r = slice(hh * tq, (hh + 1) * tq)
            g_cmp, g_slc, g_win = (gates[sub_rows[u], 3 * hh + c:3 * hh + c + 1] for c in range(3))
            o = g_cmp * oc_ref[0, hh, sub_rows[u], :] + g_slc * oslc_s[u, r, :] + g_win * owin_s[u, r, :]
            o_ref[0, sub_rows[u], hh * HEAD_DIM:(hh + 1) * HEAD_DIM] = o.astype(o_ref.dtype)


def _nsa_attention(slabs, sel, o_cmp, cos, sin, tq=128, n_sub=2):
    batch, _, seq, _ = slabs.shape
    kv = lambda off: pl.BlockSpec((1, 1, seq, LANE), lambda b, g, i: (b, off + g, 0, 0))
    table = pl.BlockSpec((1, seq, LANE), lambda b, g, i: (b, 0, 0))
    rows = C_GROUP * tq
    span = WIN_SIZE + tq
    tile = n_sub * tq
    assert (seq // 4) % tile == 0, "the sub-tiles of a step must share one selected-branch extent"
    return pl.pallas_call(
        _nsa_kernel,
        grid=(batch, C_KV_HEADS, seq // tile),
        in_specs=[pl.BlockSpec((1, C_GROUP, tile, LANE), lambda b, g, i: (b, g, i, 0)),
                  kv(16), kv(18), kv(20), kv(22),
                  pl.BlockSpec((1, 1, tile, LANE), lambda b, g, i: (b, g, i, 0)),
                  pl.BlockSpec((1, C_GROUP, tile, LANE), lambda b, g, i: (b, g, i, 0)),
                  pl.BlockSpec((1, 1, tile, LANE), lambda b, g, i: (b, 24 + g, i, 0)),
                  table, table],
        out_specs=pl.BlockSpec((1, tile, C_GROUP * HEAD_DIM), lambda b, g, i: (b, i, g)),
        out_shape=jax.ShapeDtypeStruct((batch, seq, C_HEADS * HEAD_DIM), _BF16),
        scratch_shapes=[pltpu.VMEM((seq, HEAD_DIM), _BF16), pltpu.VMEM((seq, 2 * HEAD_DIM), _BF16)] * 2
        + [pltpu.VMEM((LANE, seq), _BF16), pltpu.VMEM((n_sub, rows, HEAD_DIM), _BF16),
           pltpu.VMEM((n_sub, tq, seq), _F32), pltpu.VMEM((n_sub, rows, seq), _F32),
           pltpu.VMEM((n_sub, rows, seq), _BF16),
           pltpu.VMEM((n_sub, tq, span), _F32), pltpu.VMEM((n_sub, rows, span), _F32),
           pltpu.VMEM((n_sub, rows, span), _BF16)]
        + [pltpu.VMEM((n_sub, rows, HEAD_DIM), _F32)] * 2,
        compiler_params=_params("arbitrary", "arbitrary", "arbitrary"),
        name="nsa_attention",
    )(slabs, slabs, slabs, slabs, slabs, sel, o_cmp, slabs, cos, sin)


def _conv_kernel(u_ref, c_ref, b_ref, w_ref, o_ref):
    seq = u_ref.shape[2]
    row = lax.broadcasted_iota(jnp.int32, (seq, LANE), 0)
    u = c_ref[0, 0] * u_ref[0, 0]
    taps = w_ref.shape[0]
    conv = _shift_rows(u, taps - 1, row) * w_ref[0:1, :]
    for j in range(1, taps):
        shifted = u if j == taps - 1 else _shift_rows(u, taps - 1 - j, row)
        conv = conv + shifted * w_ref[j:j + 1, :]
    o_ref[0] = (b_ref[0, 0] * conv).astype(o_ref.dtype)


def _short_conv(slabs, conv_w, first_slab):
    batch, _, seq, _ = slabs.shape
    nd = D_WIDTH // LANE
    part = lambda off: pl.BlockSpec((1, 1, seq, LANE), lambda b, j: (b, first_slab + off + j, 0, 0))
    return pl.pallas_call(
        _conv_kernel,
        grid=(batch, nd),
        in_specs=[part(0), part(nd), part(2 * nd),
                  pl.BlockSpec((conv_w.shape[0], LANE), lambda b, j: (0, j))],
        out_specs=pl.BlockSpec((1, seq, LANE), lambda b, j: (b, 0, j)),
        out_shape=jax.ShapeDtypeStruct((batch, seq, D_WIDTH), _BF16),
        compiler_params=_params("arbitrary", "arbitrary"),
        name="short_conv",
    )(slabs, slabs, slabs, conv_w)


def _pad_cols(w, total):
    return jnp.pad(w, ((0, 0), (0, total - w.shape[1])))


def _odd_in_weight(w):
    qkv = C_HEADS * HEAD_DIM + 6 * C_KV_HEADS * HEAD_DIM
    ngate = 3 * C_GROUP
    gates = [_pad_cols(w[:, qkv + g * ngate:qkv + (g + 1) * ngate], LANE) for g in range(C_KV_HEADS)]
    rest = w[:, qkv + C_KV_HEADS * ngate:]
    return _pad_cols(jnp.concatenate([w[:, :qkv]] + gates + [rest], axis=1), PROJ_SLABS * LANE)


def kernel(x, positions, norm_w, ffn_w_gate, ffn_w_up, ffn_w_down, ev_w_in, ev_w_out, pool_w, pool_scale,
           od_w_in, od_w_out, cmp_pe_k, cmp_w1_k, cmp_w2_k, cmp_pe_v, cmp_w1_v, cmp_w2_v, conv_w):
    batch, seq, d = x.shape
    depth = norm_w.shape[0]
    bf = lambda a: a.astype(_BF16)

    inv_freq = 1.0 / (ROPE_THETA ** (jnp.arange(0, HEAD_DIM, 2, dtype=_F32) / HEAD_DIM))
    ang = positions.astype(_F32)[..., None] * inv_freq
    cos = jnp.concatenate([jnp.cos(ang), jnp.cos(ang)], axis=-1)
    sin = jnp.concatenate([-jnp.sin(ang), jnp.sin(ang)], axis=-1)

    nw = norm_w.reshape(depth, 6, 1, d)
    h = x.reshape(batch * seq, d)

    w_gate, w_up, w_down = bf(ffn_w_gate), bf(ffn_w_up), bf(ffn_w_down)
    w_ev_in, w_ev_out = bf(ev_w_in), bf(ev_w_out)
    w_od_in, w_od_out = bf(jax.vmap(_odd_in_weight)(od_w_in)), bf(od_w_out)

    def ffn(h, layer, which):
        a = _ffn_up(h, nw[layer, 4 * which], w_gate, w_up, (layer, which))
        return _ffn_down(a, w_down, (layer, which), h, nw[layer, 4 * which + 1], HALF_STEP)

    for layer in range(depth):
        i = layer // 2
        h = ffn(h, layer, 0)
        if layer % 2 == 0:
            slabs = _norm_proj(h, nw[layer, 2], w_ev_in, (i,), batch, seq)
            o_a = _dilated_attention(slabs, cos, sin)
            o_b = _multiscale_pool(slabs, bf(pool_w[i]), pool_scale[i].reshape(1, -1))
            h = _mix_out(o_a.reshape(batch * seq, -1), o_b.reshape(batch * seq, -1), w_ev_out, (i,), h,
                         nw[layer, 3])
        else:
            slabs = _norm_proj(h, nw[layer, 2], w_od_in, (i,), batch, seq)
            kv_cmp = _compress(slabs, jnp.stack([cmp_pe_k[i], cmp_pe_v[i]]),
                               bf(jnp.stack([cmp_w1_k[i], cmp_w1_v[i]])),
                               bf(jnp.stack([cmp_w2_k[i], cmp_w2_v[i]])), C_HEADS)
            o_cmp, sel = _cmp_select(slabs, kv_cmp)
            o_c = _nsa_attention(slabs, sel, o_cmp, cos, sin)
            y_d = _short_conv(slabs, conv_w[i], 26)
            h = _mix_out(o_c.reshape(batch * seq, -1), y_d.reshape(batch * seq, -1), w_od_out, (i,), h,
                         nw[layer, 3])
        h = ffn(h, layer, 1)
    return h.reshape(batch, seq, d)
```

```python
import functools

import jax
import jax.numpy as jnp
from jax import lax
from jax.experimental import pallas as pl
from jax.experimental.pallas import tpu as pltpu

HEAD_DIM = 128
ROPE_THETA = 10000.0
NORM_EPS = 1e-6
LOG2_E = 1.4426950408889634
HALF_STEP = 0.5
LANE = 128

A_HEADS = 12
A_DILATIONS = (1, 4, 16)
A_BACK = 128
B_WINDOWS = (2, 4, 8, 16)
B_GROUPS = 4

C_HEADS = 12
C_KV_HEADS = 2
C_GROUP = C_HEADS // C_KV_HEADS
CMP_BLOCK = 32
CMP_STRIDE = 16
SLC_BLOCK = 64
SLC_TOPN = 8
WIN_SIZE = 512
FORCED_SCORE = 1e9
D_WIDTH = 512
PROJ_SLABS = 40
NORM_SUB_TILES = 4

VMEM_LIMIT = 56 * 1024 * 1024

_BF16 = jnp.bfloat16
_F32 = jnp.float32


def _params(*sem):
    return pltpu.CompilerParams(dimension_semantics=sem, vmem_limit_bytes=VMEM_LIMIT)


def _rms(x, w):
    return x * lax.rsqrt(jnp.mean(x * x, axis=-1, keepdims=True) + NORM_EPS) * w


def _dot(a, b):
    return jnp.dot(a, b, preferred_element_type=_F32)


def _dot_t(a, b):
    return lax.dot_general(a, b, (((1,), (1,)), ((), ())), preferred_element_type=_F32)


def _rope(x, cos, sin):
    return x * cos + pltpu.roll(x, HEAD_DIM // 2, 1) * sin


def _shift_rows(x, k, row):
    return jnp.where(row >= k, pltpu.roll(x, k, 0), 0.0)


def _proj_kernel(h_ref, nw_ref, w_ref, o_ref, xn_ref):
    tn = o_ref.shape[1] * LANE
    cols = pl.ds(pl.multiple_of(pl.program_id(1) * tn, tn), tn)

    def use(xn, r):
        acc = _dot(xn, w_ref[:, cols])
        for s in range(o_ref.shape[1]):
            o_ref[0, s, r, :] = acc[:, s * LANE:(s + 1) * LANE]

    _normed_rows(h_ref, nw_ref, xn_ref, use)


def _stacked(block, index_map, lead, **kwargs):
    return pl.BlockSpec((None,) * len(lead) + block, lambda *g: tuple(lead) + index_map(*g), **kwargs)


def _norm_proj(h, nw, w, lead, batch, seq, tm=1024, tn=1024):
    t, d = h.shape
    n = w.shape[-1]
    per_b = seq // tm
    return pl.pallas_call(
        _proj_kernel,
        grid=(t // tm, n // tn),
        in_specs=[pl.BlockSpec((tm, d), lambda i, j: (i, 0)),
                  pl.BlockSpec((1, d), lambda i, j: (0, 0)),
                  _stacked((d, n), lambda i, j: (0, 0), lead, pipeline_mode=pl.Buffered(1))],
        out_specs=pl.BlockSpec((1, tn // LANE, tm, LANE), lambda i, j: (i // per_b, j, i % per_b, 0)),
        out_shape=jax.ShapeDtypeStruct((batch, n // LANE, seq, LANE), _F32),
        scratch_shapes=[pltpu.VMEM((tm, d), _BF16)],
        compiler_params=_params("arbitrary", "arbitrary"),
        name="norm_proj",
    )(h, nw, w)


def _normed_rows(h_ref, nw_ref, xn_ref, use):
    j = pl.program_id(1)

    @pl.when(j == 0)
    def _():
        sub = h_ref.shape[0] // NORM_SUB_TILES
        for s in range(NORM_SUB_TILES):
            r = slice(s * sub, (s + 1) * sub)
            xn = _rms(h_ref[r, :], nw_ref[...]).astype(_BF16)
            xn_ref[r, :] = xn
            use(xn, r)

    @pl.when(j > 0)
    def _():
        use(xn_ref[...], slice(None))


def _ffn_up_kernel(h_ref, nw_ref, wg_ref, wu_ref, o_ref, xn_ref):
    def use(xn, r):
        g = _dot(xn, wg_ref[...])
        u = _dot(xn, wu_ref[...])
        o_ref[r, :] = (jax.nn.silu(g) * u).astype(_BF16)

    _normed_rows(h_ref, nw_ref, xn_ref, use)


def _ffn_up(h, nw, wg, wu, lead, tm=1024, tn=512):
    t, d = h.shape
    f = wg.shape[-1]
    return pl.pallas_call(
        _ffn_up_kernel,
        grid=(t // tm, f // tn),
        in_specs=[pl.BlockSpec((tm, d), lambda i, j: (i, 0)),
                  pl.BlockSpec((1, d), lambda i, j: (0, 0)),
                  _stacked((d, tn), lambda i, j: (0, j), lead),
                  _stacked((d, tn), lambda i, j: (0, j), lead)],
        out_specs=pl.BlockSpec((tm, tn), lambda i, j: (i, j)),
        out_shape=jax.ShapeDtypeStruct((t, f), _BF16),
        scratch_shapes=[pltpu.VMEM((tm, d), _BF16)],
        compiler_params=_params("arbitrary", "arbitrary"),
        name="ffn_up",
    )(h, nw, wg, wu)


def _ffn_down_kernel(a_ref, w_ref, h_ref, nw_ref, o_ref, *, scale, n_sub):
    sub = a_ref.shape[0] // n_sub
    for s in range(n_sub):
        r = slice(s * sub, (s + 1) * sub)
        f = _dot(a_ref[r, :], w_ref[...])
        o_ref[r, :] = h_ref[r, :] + scale * _rms(f, nw_ref[...])


def _ffn_down(a, w, lead, h, nw, scale, tm=512, n_sub=2):
    t, f = a.shape
    d = w.shape[-1]
    w_spec = _stacked((f, d), lambda i: (0, 0), lead, pipeline_mode=pl.Buffered(1))
    return pl.pallas_call(
        functools.partial(_ffn_down_kernel, scale=scale, n_sub=n_sub),
        grid=(t // tm,),
        in_specs=[pl.BlockSpec((tm, f), lambda i: (i, 0)),
                  w_spec,
                  pl.BlockSpec((tm, d), lambda i: (i, 0)),
                  pl.BlockSpec((1, d), lambda i: (0, 0))],
        out_specs=pl.BlockSpec((tm, d), lambda i: (i, 0)),
        out_shape=jax.ShapeDtypeStruct((t, d), _F32),
        compiler_params=_params("arbitrary"),
        name="ffn_down",
    )(a, w, h, nw)


def _mix_out_kernel(x1_ref, x2_ref, w_ref, h_ref, nw_ref, o_ref):
    c1 = x1_ref.shape[1]
    sub = x1_ref.shape[0] // 2
    for s in range(2):
        r = slice(s * sub, (s + 1) * sub)
        m = _dot(x1_ref[r, :], w_ref[0:c1, :]) + _dot(x2_ref[r, :], w_ref[c1:, :])
        o_ref[r, :] = h_ref[r, :] + _rms(m, nw_ref[...])


def _mix_out(x1, x2, w, lead, h, nw, tm=512):
    t, c1 = x1.shape
    c2 = x2.shape[1]
    d = w.shape[-1]
    return pl.pallas_call(
        _mix_out_kernel,
        grid=(t // tm,),
        in_specs=[pl.BlockSpec((tm, c1), lambda i: (i, 0)),
                  pl.BlockSpec((tm, c2), lambda i: (i, 0)),
                  _stacked((c1 + c2, d), lambda i: (0, 0), lead),
                  pl.BlockSpec((tm, d), lambda i: (i, 0)),
                  pl.BlockSpec((1, d), lambda i: (0, 0))],
        out_specs=pl.BlockSpec((tm, d), lambda i: (i, 0)),
        out_shape=jax.ShapeDtypeStruct((t, d), _F32),
        compiler_params=_params("arbitrary"),
        name="mix_out",
    )(x1, x2, w, h, nw)


def _rows(start, size, stride):
    return pl.ds(start, size) if stride == 1 else pl.ds(start, size, stride=stride)


def _dilated_kernel(q_ref, k_ref, v_ref, cos_ref, sin_ref, o_ref, qs, ks, s_s, p_s, pv_s, m_s, l_s):
    seq = q_ref.shape[2]
    blk = A_BACK
    scale = HEAD_DIM ** -0.5 * LOG2_E
    chunk = 256

    for c in range(0, seq, chunk):
        r = slice(c, c + chunk)
        cos, sin = cos_ref[0, r, :], sin_ref[0, r, :]
        qs[r, :] = _rope(q_ref[0, 0, r, :], cos, sin) * scale
        ks[r, :] = _rope(k_ref[0, 0, r, :], cos, sin)

    qi = lax.broadcasted_iota(jnp.int32, (blk, blk), 0)
    kj = lax.broadcasted_iota(jnp.int32, (blk, blk), 1)
    cur_ok = kj <= qi
    both_ok = jnp.concatenate([kj >= qi, cur_ok], axis=1)
    ones = jnp.ones((blk, HEAD_DIM), _BF16)

    blocks = []
    for g, dil in enumerate(A_DILATIONS):
        for res in range(dil):
            prev = None
            for n in range(seq // (blk * dil)):
                rows = _rows(res + n * blk * dil, blk, dil)
                blocks.append((g, rows, prev))
                prev = rows

    def keys_bf16(rows):
        return ks[rows, :].astype(_BF16)

    def values1_bf16(rows):
        return jnp.concatenate([v_ref[0, 0, rows, :].astype(_BF16), ones], axis=1)

    for b, (g, rows, prev) in enumerate(blocks):
        qb = qs[rows, :].astype(_BF16)
        if prev is None:
            s_s[b, :, 0:blk] = jnp.where(cur_ok, _dot_t(qb, keys_bf16(rows)), -jnp.inf)
        else:
            k_all = jnp.concatenate([keys_bf16(prev), keys_bf16(rows)], axis=0)
            s_s[b] = jnp.where(both_ok, _dot_t(qb, k_all), -jnp.inf)

    for b, (g, rows, prev) in enumerate(blocks):
        width = blk if prev is None else 2 * blk
        s = s_s[b, :, 0:width]
        m = jnp.max(s, axis=-1, keepdims=True)
        p_s[b, :, 0:width] = jnp.exp2(s - m).astype(_BF16)
        m_s[g, rows, :] = jnp.broadcast_to(m, (blk, HEAD_DIM))

    for b, (g, rows, prev) in enumerate(blocks):
        if prev is None:
            pva = _dot(p_s[b, :, 0:blk], values1_bf16(rows))
        else:
            pva = _dot(p_s[b], jnp.concatenate([values1_bf16(prev), values1_bf16(rows)], axis=0))
        pv_s[g, rows, :] = pva[:, 0:HEAD_DIM]
        l_s[g, rows, :] = pva[:, HEAD_DIM:]

    for c in range(0, seq, chunk):
        r = slice(c, c + chunk)
        m_all = jnp.maximum(jnp.maximum(m_s[0, r, :], m_s[1, r, :]), m_s[2, r, :])
        num = jnp.zeros((chunk, HEAD_DIM), _F32)
        den = jnp.zeros((chunk, HEAD_DIM), _F32)
        for g in range(len(A_DILATIONS)):
            w = jnp.exp2(m_s[g, r, :] - m_all)
            num = num + w * pv_s[g, r, :]
            den = den + w * l_s[g, r, :]
        o_ref[0, r, :] = (num / den).astype(o_ref.dtype)


def _dilated_attention(slabs, cos, sin):
    batch, _, seq, _ = slabs.shape
    head = lambda off: pl.BlockSpec((1, 1, seq, LANE), lambda b, h: (b, off + h, 0, 0))
    table = pl.BlockSpec((1, seq, LANE), lambda b, h: (b, 0, 0))
    nblocks = len(A_DILATIONS) * seq // A_BACK
    return pl.pallas_call(
        _dilated_kernel,
        grid=(batch, A_HEADS),
        in_specs=[head(0), head(A_HEADS), head(2 * A_HEADS), table, table],
        out_specs=pl.BlockSpec((1, seq, LANE), lambda b, h: (b, 0, h)),
        out_shape=jax.ShapeDtypeStruct((batch, seq, A_HEADS * HEAD_DIM), _BF16),
        scratch_shapes=[pltpu.VMEM((seq, HEAD_DIM), _F32), pltpu.VMEM((seq, HEAD_DIM), _F32),
                        pltpu.VMEM((nblocks, A_BACK, 2 * A_BACK), _F32),
                        pltpu.VMEM((nblocks, A_BACK, 2 * A_BACK), _BF16),
                        pltpu.VMEM((3, seq, HEAD_DIM), _F32), pltpu.VMEM((3, seq, HEAD_DIM), _F32),
                        pltpu.VMEM((3, seq, HEAD_DIM), _F32)],
        compiler_params=_params("arbitrary", "arbitrary"),
        name="dilated_attention",
    )(slabs, slabs, slabs, cos, sin)


def _pool_kernel(u_ref, w_ref, sc_ref, o_ref):
    seq = u_ref.shape[2]
    row = lax.broadcasted_iota(jnp.int32, (seq, LANE), 0)
    for g, win in enumerate(B_WINDOWS):
        x = u_ref[0, g]
        s = x
        step = 1
        while step < win:
            s = s + _shift_rows(s, step, row)
            step *= 2
        cnt = jnp.minimum(row + 1, win).astype(_F32)
        pooled = s / cnt - x
        mixed = _dot(pooled.astype(_BF16), w_ref[g]) * sc_ref[:, g * LANE:(g + 1) * LANE]
        o_ref[0, :, g * LANE:(g + 1) * LANE] = mixed.astype(o_ref.dtype)


def _multiscale_pool(slabs, pool_w, pool_scale):
    batch, _, seq, _ = slabs.shape
    first = 3 * A_HEADS // B_GROUPS
    return pl.pallas_call(
        _pool_kernel,
        grid=(batch,),
        in_specs=[pl.BlockSpec((1, B_GROUPS, seq, LANE), lambda b: (b, first, 0, 0)),
                  pl.BlockSpec((B_GROUPS, LANE, LANE), lambda b: (0, 0, 0)),
                  pl.BlockSpec((1, B_GROUPS * LANE), lambda b: (0, 0))],
        out_specs=pl.BlockSpec((1, seq, B_GROUPS * LANE), lambda b: (b, 0, 0)),
        out_shape=jax.ShapeDtypeStruct((batch, seq, B_GROUPS * LANE), _BF16),
        compiler_params=_params("arbitrary"),
        name="multiscale_pool",
    )(slabs, pool_w, pool_scale)


def _compress_kernel(x_ref, pe_ref, w1_ref, w2_ref, o_ref):
    nrow = o_ref.shape[3]
    hidden = w1_ref.shape[2]
    a = jnp.zeros((nrow, hidden), _F32)
    b = jnp.zeros((nrow, hidden), _F32)
    for tok in range(CMP_STRIDE):
        x = x_ref[0, 0, pl.ds(tok, nrow, stride=CMP_STRIDE), :]
        lo, hi = tok, CMP_STRIDE + tok
        a = a + _dot((x + pe_ref[0, lo:lo + 1, :]).astype(_BF16), w1_ref[0, lo * HEAD_DIM:(lo + 1) * HEAD_DIM, :])
        b = b + _dot((x + pe_ref[0, hi:hi + 1, :]).astype(_BF16), w1_ref[0, hi * HEAD_DIM:(hi + 1) * HEAD_DIM, :])
    hid = a + pltpu.roll(b, nrow - 1, 0)
    o_ref[0, 0, 0] = _dot(jax.nn.gelu(hid).astype(_BF16), w2_ref[0])


def _compress(slabs, pe, w1, w2, first_slab):
    batch, _, seq, _ = slabs.shape
    nrow = seq // CMP_STRIDE
    return pl.pallas_call(
        _compress_kernel,
        grid=(2, batch, C_KV_HEADS),
        in_specs=[pl.BlockSpec((1, 1, seq, LANE), lambda kv, b, g: (b, first_slab + 2 * kv + g, 0, 0)),
                  pl.BlockSpec((1, CMP_BLOCK, HEAD_DIM), lambda kv, b, g: (kv, 0, 0)),
                  pl.BlockSpec((1, CMP_BLOCK * HEAD_DIM, w1.shape[2]), lambda kv, b, g: (kv, 0, 0)),
                  pl.BlockSpec((1, w2.shape[1], HEAD_DIM), lambda kv, b, g: (kv, 0, 0))],
        out_specs=pl.BlockSpec((1, 1, 1, nrow, HEAD_DIM), lambda kv, b, g: (kv, b, g, 0, 0)),
        out_shape=jax.ShapeDtypeStruct((2, batch, C_KV_HEADS, nrow, HEAD_DIM), _F32),
        compiler_params=_params("arbitrary", "arbitrary", "arbitrary"),
        name="compress",
    )(slabs, pe, w1, w2)


def _cmp_select_kernel(q_ref, kc_ref, vc_ref, o_ref, sel_ref, *, nslc, n_sub):
    tq = q_ref.shape[2] // n_sub
    ncmp = kc_ref.shape[3]
    scale = HEAD_DIM ** -0.5
    n = lax.broadcasted_iota(jnp.int32, (tq, ncmp), 1)

    cj = lax.broadcasted_iota(jnp.int32, (LANE, ncmp), 0) * SLC_BLOCK
    cn = lax.broadcasted_iota(jnp.int32, (LANE, ncmp), 1) * CMP_STRIDE
    cover_t = ((cn < cj + SLC_BLOCK) & (cn + CMP_BLOCK > cj)).astype(_BF16)
    j = lax.broadcasted_iota(jnp.int32, (nslc, tq), 0)
    jf = j.astype(_F32)

    kc = kc_ref[0, 0, 0].astype(_BF16)
    vc = vc_ref[0, 0, 0].astype(_BF16)
    for u in range(n_sub):
        rows = slice(u * tq, (u + 1) * tq)
        t0 = (pl.program_id(2) * n_sub + u) * tq
        t = t0 + lax.broadcasted_iota(jnp.int32, (tq, ncmp), 0)
        cmask = n * CMP_STRIDE + (CMP_BLOCK - 1) <= t
        imp_t = jnp.zeros((LANE, tq), _F32)
        for hh in range(C_GROUP):
            sc = jnp.where(cmask, _dot_t(q_ref[0, hh, rows, :].astype(_BF16), kc) * scale, -jnp.inf)
            m = jnp.max(sc, axis=-1, keepdims=True)
            m = jnp.where(jnp.isfinite(m), m, 0.0)
            p = jnp.exp(sc - m)
            den = jnp.sum(p, axis=-1, keepdims=True)
            p = (p / jnp.maximum(den, 1.0)).astype(_BF16)
            o_ref[0, hh, rows, :] = _dot(p, vc)
            imp_t = imp_t + _dot_t(cover_t, p)

        tt = t0 + lax.broadcasted_iota(jnp.int32, (nslc, tq), 1)
        cur = lax.shift_right_logical(tt, SLC_BLOCK.bit_length() - 1)
        visible = j <= cur
        forced = visible & ((j == 0) | (j >= cur - 1))
        score = jnp.where(forced, FORCED_SCORE, jnp.where(visible, imp_t[0:nslc, :], -FORCED_SCORE))
        sel_t = jnp.zeros((nslc, tq), _F32)
        for _ in range(min(SLC_TOPN, nslc)):
            best = jnp.max(score, axis=0, keepdims=True)
            pick = jnp.min(jnp.where(score == best, jf, float(nslc)), axis=0, keepdims=True)
            hit = jf == pick
            sel_t = jnp.where(hit, 1.0, sel_t)
            score = jnp.where(hit, -jnp.inf, score)
        sel_t = jnp.concatenate([sel_t, jnp.zeros((LANE - nslc, tq), _F32)], axis=0)
        sel_ref[0, 0, rows, :] = sel_t.T.astype(sel_ref.dtype)


def _cmp_select(slabs, kv_cmp, tq=512, n_sub=2):
    batch, _, seq, _ = slabs.shape
    cmp_spec = lambda which: pl.BlockSpec((1, 1, 1) + kv_cmp.shape[3:], lambda b, g, i: (which, b, g, 0, 0))
    return pl.pallas_call(
        functools.partial(_cmp_select_kernel, nslc=seq // SLC_BLOCK, n_sub=n_sub),
        grid=(batch, C_KV_HEADS, seq // tq),
        in_specs=[pl.BlockSpec((1, C_GROUP, tq, LANE), lambda b, g, i: (b, g, i, 0)),
                  cmp_spec(0), cmp_spec(1)],
        out_specs=[pl.BlockSpec((1, C_GROUP, tq, LANE), lambda b, g, i: (b, g, i, 0)),
                   pl.BlockSpec((1, 1, tq, LANE), lambda b, g, i: (b, g, i, 0))],
        out_shape=[jax.ShapeDtypeStruct((batch, C_HEADS, seq, HEAD_DIM), _F32),
                   jax.ShapeDtypeStruct((batch, C_KV_HEADS, seq, LANE), _BF16)],
        compiler_params=_params("arbitrary", "arbitrary", "arbitrary"),
        name="cmp_select",
    )(slabs, kv_cmp, kv_cmp)


def _group_softmax_pv(q, k, v1, bias_s, s_s, p_s, o_s):
    nk = k.shape[0]
    tq = bias_s.shape[0]
    piece = 512 if nk % 512 == 0 else LANE
    pieces = [slice(c, c + piece) for c in range(0, nk, piece)]
    s = _dot_t(q, k)
    for hh in range(C_GROUP):
        r = slice(hh * tq, (hh + 1) * tq)
        s_s[r, 0:nk] = s[r] + bias_s[:, 0:nk]
    for hh in range(C_GROUP):
        r = slice(hh * tq, (hh + 1) * tq)
        m = None
        for c in pieces:
            mc = jnp.max(s_s[r, c], axis=-1, keepdims=True)
            m = mc if m is None else jnp.maximum(m, mc)
        for c in pieces:
            p_s[r, c] = jnp.exp2(s_s[r, c] - m).astype(_BF16)
    o = _dot(p_s[:, 0:nk], v1)
    o_s[...] = o[:, 0:HEAD_DIM] / o[:, HEAD_DIM:]


def _nsa_kernel(q_ref, ks_ref, vs_ref, kw_ref, vw_ref, sel_ref, oc_ref, gl_ref, cos_ref, sin_ref, o_ref,
                ksr, vs1, kwr, vw1, expand_s, qr, bias_s, s_s, p_s, bias_w, s_w, p_w, oslc_s, owin_s):
    n_sub = qr.shape[0]
    tq = q_ref.shape[2] // n_sub
    seq = ks_ref.shape[2]
    qi = pl.program_id(2)
    chunk = 256
    shift = SLC_BLOCK.bit_length() - 1

    @pl.when(qi == 0)
    def _():
        def prep(c, _):
            r = pl.ds(pl.multiple_of(c * chunk, chunk), chunk)
            cos, sin = cos_ref[0, r, :], sin_ref[0, r, :]
            ones = jnp.ones((chunk, HEAD_DIM), _BF16)
            ksr[r, :] = _rope(ks_ref[0, 0, r, :], cos, sin).astype(_BF16)
            kwr[r, :] = _rope(kw_ref[0, 0, r, :], cos, sin).astype(_BF16)
            vs1[r, 0:HEAD_DIM] = vs_ref[0, 0, r, :].astype(_BF16)
            vs1[r, HEAD_DIM:] = ones
            vw1[r, 0:HEAD_DIM] = vw_ref[0, 0, r, :].astype(_BF16)
            vw1[r, HEAD_DIM:] = ones
            return 0

        lax.fori_loop(0, seq // chunk, prep, 0)
        blk = lax.broadcasted_iota(jnp.int32, (LANE, seq), 0)
        key = lax.broadcasted_iota(jnp.int32, (LANE, seq), 1)
        expand_s[...] = (blk == lax.shift_right_logical(key, shift)).astype(_BF16)

    base = pl.multiple_of(qi * (n_sub * tq), n_sub * tq)
    sub_rows = [slice(u * tq, (u + 1) * tq) for u in range(n_sub)]
    for u in range(n_sub):
        cos_q = cos_ref[0, pl.ds(base + u * tq, tq), :]
        sin_q = sin_ref[0, pl.ds(base + u * tq, tq), :]
        for hh in range(C_GROUP):
            q_rot = _rope(q_ref[0, hh, sub_rows[u], :], cos_q, sin_q)
            qr[u, hh * tq:(hh + 1) * tq, :] = (q_rot * (HEAD_DIM ** -0.5 * LOG2_E)).astype(_BF16)

    def window_branch(u):
        q0 = base + u * tq
        span = WIN_SIZE + tq
        k0 = pl.multiple_of(jnp.maximum(q0 - WIN_SIZE, 0), tq)
        t = q0 + lax.broadcasted_iota(jnp.int32, (tq, span), 0)
        key = k0 + lax.broadcasted_iota(jnp.int32, (tq, span), 1)
        bias_w[u] = jnp.where((key <= t) & (key > t - WIN_SIZE), 0.0, -jnp.inf)
        _group_softmax_pv(qr[u], kwr[pl.ds(k0, span), :], vw1[pl.ds(k0, span), :],
                          bias_w.at[u], s_w.at[u], p_w.at[u], owin_s.at[u])

    n_extent = 4
    step = seq // n_extent
    for nt in range(1, n_extent + 1):
        @pl.when(base // step + 1 == nt)
        def _(nk=nt * step):
            for u in range(n_sub):
                for c in range(0, nk, step):
                    t = base + u * tq + lax.broadcasted_iota(jnp.int32, (tq, step), 0)
                    key = c + lax.broadcasted_iota(jnp.int32, (tq, step), 1)
                    ok = (_dot(sel_ref[0, 0, sub_rows[u], :], expand_s[:, c:c + step]) > 0.5) & (key <= t)
                    bias_s[u, :, c:c + step] = jnp.where(ok, 0.0, -jnp.inf)
                _group_softmax_pv(qr[u], ksr[0:nk, :], vs1[0:nk, :],
                                  bias_s.at[u], s_s.at[u], p_s.at[u], oslc_s.at[u])
                window_branch(u)

    gates = jax.nn.sigmoid(gl_ref[0, 0])
    for u in range(n_sub):
        for hh in range(C_GROUP):
            r = slice(hh * tq, (hh + 1) * tq)
            g_cmp, g_slc, g_win = (gates[sub_rows[u], 3 * hh + c:3 * hh + c + 1] for c in range(3))
            o = g_cmp * oc_ref[0, hh, sub_rows[u], :] + g_slc * oslc_s[u, r, :] + g_win * owin_s[u, r, :]
            o_ref[0, sub_rows[u], hh * HEAD_DIM:(hh + 1) * HEAD_DIM] = o.astype(o_ref.dtype)


def _nsa_attention(slabs, sel, o_cmp, cos, sin, tq=128, n_sub=2):
    batch, _, seq, _ = slabs.shape
    kv = lambda off: pl.BlockSpec((1, 1, seq, LANE), lambda b, g, i: (b, off + g, 0, 0))
    table = pl.BlockSpec((1, seq, LANE), lambda b, g, i: (b, 0, 0))
    rows = C_GROUP * tq
    span = WIN_SIZE + tq
    tile = n_sub * tq
    assert (seq // 4) % tile == 0, "the sub-tiles of a step must share one selected-branch extent"
    return pl.pallas_call(
        _nsa_kernel,
        grid=(batch, C_KV_HEADS, seq // tile),
        in_specs=[pl.BlockSpec((1, C_GROUP, tile, LANE), lambda b, g, i: (b, g, i, 0)),
                  kv(16), kv(18), kv(20), kv(22),
                  pl.BlockSpec((1, 1, tile, LANE), lambda b, g, i: (b, g, i, 0)),
                  pl.BlockSpec((1, C_GROUP, tile, LANE), lambda b, g, i: (b, g, i, 0)),
                  pl.BlockSpec((1, 1, tile, LANE), lambda b, g, i: (b, 24 + g, i, 0)),
                  table, table],
        out_specs=pl.BlockSpec((1, tile, C_GROUP * HEAD_DIM), lambda b, g, i: (b, i, g)),
        out_shape=jax.ShapeDtypeStruct((batch, seq, C_HEADS * HEAD_DIM), _BF16),
        scratch_shapes=[pltpu.VMEM((seq, HEAD_DIM), _BF16), pltpu.VMEM((seq, 2 * HEAD_DIM), _BF16)] * 2
        + [pltpu.VMEM((LANE, seq), _BF16), pltpu.VMEM((n_sub, rows, HEAD_DIM), _BF16),
           pltpu.VMEM((n_sub, tq, seq), _F32), pltpu.VMEM((n_sub, rows, seq), _F32),
           pltpu.VMEM((n_sub, rows, seq), _BF16),
           pltpu.VMEM((n_sub, tq, span), _F32), pltpu.VMEM((n_sub, rows, span), _F32),
           pltpu.VMEM((n_sub, rows, span), _BF16)]
        + [pltpu.VMEM((n_sub, rows, HEAD_DIM), _F32)] * 2,
        compiler_params=_params("arbitrary", "arbitrary", "arbitrary"),
        name="nsa_attention",
    )(slabs, slabs, slabs, slabs, slabs, sel, o_cmp, slabs, cos, sin)


def _conv_kernel(u_ref, c_ref, b_ref, w_ref, o_ref):
    seq = u_ref.shape[2]
    row = lax.broadcasted_iota(jnp.int32, (seq, LANE), 0)
    u = c_ref[0, 0] * u_ref[0, 0]
    taps = w_ref.shape[0]
    conv = _shift_rows(u, taps - 1, row) * w_ref[0:1, :]
    for j in range(1, taps):
        shifted = u if j == taps - 1 else _shift_rows(u, taps - 1 - j, row)
        conv = conv + shifted * w_ref[j:j + 1, :]
    o_ref[0] = (b_ref[0, 0] * conv).astype(o_ref.dtype)


def _short_conv(slabs, conv_w, first_slab):
    batch, _, seq, _ = slabs.shape
    nd = D_WIDTH // LANE
    part = lambda off: pl.BlockSpec((1, 1, seq, LANE), lambda b, j: (b, first_slab + off + j, 0, 0))
    return pl.pallas_call(
        _conv_kernel,
        grid=(batch, nd),
        in_specs=[part(0), part(nd), part(2 * nd),
                  pl.BlockSpec((conv_w.shape[0], LANE), lambda b, j: (0, j))],
        out_specs=pl.BlockSpec((1, seq, LANE), lambda b, j: (b, 0, j)),
        out_shape=jax.ShapeDtypeStruct((batch, seq, D_WIDTH), _BF16),
        compiler_params=_params("arbitrary", "arbitrary"),
        name="short_conv",
    )(slabs, slabs, slabs, conv_w)


def _pad_cols(w, total):
    return jnp.pad(w, ((0, 0), (0, total - w.shape[1])))


def _odd_in_weight(w):
    qkv = C_HEADS * HEAD_DIM + 6 * C_KV_HEADS * HEAD_DIM
    ngate = 3 * C_GROUP
    gates = [_pad_cols(w[:, qkv + g * ngate:qkv + (g + 1) * ngate], LANE) for g in range(C_KV_HEADS)]
    rest = w[:, qkv + C_KV_HEADS * ngate:]
    return _pad_cols(jnp.concatenate([w[:, :qkv]] + gates + [rest], axis=1), PROJ_SLABS * LANE)


def kernel(x, positions, norm_w, ffn_w_gate, ffn_w_up, ffn_w_down, ev_w_in, ev_w_out, pool_w, pool_scale,
           od_w_in, od_w_out, cmp_pe_k, cmp_w1_k, cmp_w2_k, cmp_pe_v, cmp_w1_v, cmp_w2_v, conv_w):
    batch, seq, d = x.shape
    depth = norm_w.shape[0]
    bf = lambda a: a.astype(_BF16)

    inv_freq = 1.0 / (ROPE_THETA ** (jnp.arange(0, HEAD_DIM, 2, dtype=_F32) / HEAD_DIM))
    ang = positions.astype(_F32)[..., None] * inv_freq
    cos = jnp.concatenate([jnp.cos(ang), jnp.cos(ang)], axis=-1)
    sin = jnp.concatenate([-jnp.sin(ang), jnp.sin(ang)], axis=-1)

    nw = norm_w.reshape(depth, 6, 1, d)
    h = x.reshape(batch * seq, d)

    w_gate, w_up, w_down = bf(ffn_w_gate), bf(ffn_w_up), bf(ffn_w_down)
    w_ev_in, w_ev_out = bf(ev_w_in), bf(ev_w_out)
    w_od_in, w_od_out = bf(jax.vmap(_odd_in_weight)(od_w_in)), bf(od_w_out)

    def ffn(h, layer, which):
        a = _ffn_up(h, nw[layer, 4 * which], w_gate, w_up, (layer, which))
        return _ffn_down(a, w_down, (layer, which), h, nw[layer, 4 * which + 1], HALF_STEP)

    for layer in range(depth):
        i = layer // 2
        h = ffn(h, layer, 0)
        if layer % 2 == 0:
            slabs = _norm_proj(h, nw[layer, 2], w_ev_in, (i,), batch, seq)
            o_a = _dilated_attention(slabs, cos, sin)
            o_b = _multiscale_pool(slabs, bf(pool_w[i]), pool_scale[i].reshape(1, -1))
            h = _mix_out(o_a.reshape(batch * seq, -1), o_b.reshape(batch * seq, -1), w_ev_out, (i,), h,
                         nw[layer, 3])
        else:
            slabs = _norm_proj(h, nw[layer, 2], w_od_in, (i,), batch, seq)
            kv_cmp = _compress(slabs, jnp.stack([cmp_pe_k[i], cmp_pe_v[i]]),
                               bf(jnp.stack([cmp_w1_k[i], cmp_w1_v[i]])),
                               bf(jnp.stack([cmp_w2_k[i], cmp_w2_v[i]])), C_HEADS)
            o_cmp, sel = _cmp_select(slabs, kv_cmp)
            o_c = _nsa_attention(slabs, sel, o_cmp, cos, sin)
            y_d = _short_conv(slabs, conv_w[i], 26)
            h = _mix_out(o_c.reshape(batch * seq, -1), y_d.reshape(batch * seq, -1), w_od_out, (i,), h,
                         nw[layer, 3])
        h = ffn(h, layer, 1)
    return h.reshape(batch, seq, d)
```

```python
import functools

import jax
import jax.numpy as jnp
from jax import lax
from jax.experimental import pallas as pl
from jax.experimental.pallas import tpu as pltpu

HEAD_DIM = 128
ROPE_THETA = 10000.0
NORM_EPS = 1e-6
LOG2_E = 1.4426950408889634
HALF_STEP = 0.5
LANE = 128

A_HEADS = 12
A_DILATIONS = (1, 4, 16)
A_BACK = 128
B_WINDOWS = (2, 4, 8, 16)
B_GROUPS = 4

C_HEADS = 12
C_KV_HEADS = 2
C_GROUP = C_HEADS // C_KV_HEADS
CMP_BLOCK = 32
CMP_STRIDE = 16
SLC_BLOCK = 64
SLC_TOPN = 8
WIN_SIZE = 512
FORCED_SCORE = 1e9
D_WIDTH = 512
PROJ_SLABS = 40
NORM_SUB_TILES = 4

VMEM_LIMIT = 56 * 1024 * 1024

_BF16 = jnp.bfloat16
_F32 = jnp.float32


def _params(*sem):
    return pltpu.CompilerParams(dimension_semantics=sem, vmem_limit_bytes=VMEM_LIMIT)


def _rms(x, w):
    return x * lax.rsqrt(jnp.mean(x * x, axis=-1, keepdims=True) + NORM_EPS) * w


def _dot(a, b):
    return jnp.dot(a, b, preferred_element_type=_F32)


def _dot_t(a, b):
    return lax.dot_general(a, b, (((1,), (1,)), ((), ())), preferred_element_type=_F32)


def _rope(x, cos, sin):
    return x * cos + pltpu.roll(x, HEAD_DIM // 2, 1) * sin


def _shift_rows(x, k, row):
    return jnp.where(row >= k, pltpu.roll(x, k, 0), 0.0)


def _proj_kernel(h_ref, nw_ref, w_ref, *rest, n_rope, n_scaled, q_scale):
    if n_rope:
        cos_ref, sin_ref, o_ref, xn_ref = rest
    else:
        o_ref, xn_ref = rest
    per_step = o_ref.shape[1]
    tn = per_step * LANE

    def use(xn, r, step):
        if step is None:
            cols = pl.ds(pl.multiple_of(pl.program_id(1) * tn, tn), tn)
        else:
            cols = slice(step * tn, (step + 1) * tn)
        acc = _dot(xn, w_ref[:, cols])
        for s in range(per_step):
            x = acc[:, s * LANE:(s + 1) * LANE]
            slab = None if step is None else step * per_step + s
            if slab is not None and slab < n_rope:
                x = _rope(x, cos_ref[0, r, :], sin_ref[0, r, :])
                if slab < n_scaled:
                    x = x * q_scale
            o_ref[0, s, r, :] = x

    steps = w_ref.shape[1] // tn if n_rope else None
    _normed_rows(h_ref, nw_ref, xn_ref, use, static_steps=steps)


def _stacked(block, index_map, lead, **kwargs):
    return pl.BlockSpec((None,) * len(lead) + block, lambda *g: tuple(lead) + index_map(*g), **kwargs)


def _norm_proj(h, nw, w, lead, batch, seq, rope=None, n_rope=0, n_scaled=0, q_scale=1.0, tm=1024, tn=1024):
    t, d = h.shape
    n = w.shape[-1]
    per_b = seq // tm
    table = pl.BlockSpec((1, tm, LANE), lambda i, j: (i // per_b, i % per_b, 0))
    return pl.pallas_call(
        functools.partial(_proj_kernel, n_rope=n_rope, n_scaled=n_scaled, q_scale=q_scale),
        grid=(t // tm, n // tn),
        in_specs=[pl.BlockSpec((tm, d), lambda i, j: (i, 0)),
                  pl.BlockSpec((1, d), lambda i, j: (0, 0)),
                  _stacked((d, n), lambda i, j: (0, 0), lead, pipeline_mode=pl.Buffered(1))]
        + ([table, table] if n_rope else []),
        out_specs=pl.BlockSpec((1, tn // LANE, tm, LANE), lambda i, j: (i // per_b, j, i % per_b, 0)),
        out_shape=jax.ShapeDtypeStruct((batch, n // LANE, seq, LANE), _F32),
        scratch_shapes=[pltpu.VMEM((tm, d), _BF16)],
        compiler_params=_params("arbitrary", "arbitrary"),
        name="norm_proj",
    )(h, nw, w, *(rope if n_rope else ()))


def _normed_rows(h_ref, nw_ref, xn_ref, use, static_steps=None):
    j = pl.program_id(1)

    def first(step):
        sub = h_ref.shape[0] // NORM_SUB_TILES
        for s in range(NORM_SUB_TILES):
            r = slice(s * sub, (s + 1) * sub)
            xn = _rms(h_ref[r, :], nw_ref[...]).astype(_BF16)
            xn_ref[r, :] = xn
            use(xn, r, step)

    if static_steps is None:
        pl.when(j == 0)(lambda: first(None))
        pl.when(j > 0)(lambda: use(xn_ref[...], slice(None), None))
    else:
        pl.when(j == 0)(lambda: first(0))
        for step in range(1, static_steps):
            pl.when(j == step)(lambda step=step: use(xn_ref[...], slice(None), step))


def _ffn_up_kernel(h_ref, nw_ref, wg_ref, wu_ref, o_ref, xn_ref):
    def use(xn, r, step):
        g = _dot(xn, wg_ref[...])
        u = _dot(xn, wu_ref[...])
        o_ref[r, :] = (jax.nn.silu(g) * u).astype(_BF16)

    _normed_rows(h_ref, nw_ref, xn_ref, use)


def _ffn_up(h, nw, wg, wu, lead, tm=1024, tn=512):
    t, d = h.shape
    f = wg.shape[-1]
    return pl.pallas_call(
        _ffn_up_kernel,
        grid=(t // tm, f // tn),
        in_specs=[pl.BlockSpec((tm, d), lambda i, j: (i, 0)),
                  pl.BlockSpec((1, d), lambda i, j: (0, 0)),
                  _stacked((d, tn), lambda i, j: (0, j), lead),
                  _stacked((d, tn), lambda i, j: (0, j), lead)],
        out_specs=pl.BlockSpec((tm, tn), lambda i, j: (i, j)),
        out_shape=jax.ShapeDtypeStruct((t, f), _BF16),
        scratch_shapes=[pltpu.VMEM((tm, d), _BF16)],
        compiler_params=_params("arbitrary", "arbitrary"),
        name="ffn_up",
    )(h, nw, wg, wu)


def _ffn_down_kernel(a_ref, w_ref, h_ref, nw_ref, o_ref, *, scale, n_sub):
    sub = a_ref.shape[0] // n_sub
    for s in range(n_sub):
        r = slice(s * sub, (s + 1) * sub)
        f = _dot(a_ref[r, :], w_ref[...])
        o_ref[r, :] = h_ref[r, :] + scale * _rms(f, nw_ref[...])


def _ffn_down(a, w, lead, h, nw, scale, tm=512, n_sub=2):
    t, f = a.shape
    d = w.shape[-1]
    w_spec = _stacked((f, d), lambda i: (0, 0), lead, pipeline_mode=pl.Buffered(1))
    return pl.pallas_call(
        functools.partial(_ffn_down_kernel, scale=scale, n_sub=n_sub),
        grid=(t // tm,),
        in_specs=[pl.BlockSpec((tm, f), lambda i: (i, 0)),
                  w_spec,
                  pl.BlockSpec((tm, d), lambda i: (i, 0)),
                  pl.BlockSpec((1, d), lambda i: (0, 0))],
        out_specs=pl.BlockSpec((tm, d), lambda i: (i, 0)),
        out_shape=jax.ShapeDtypeStruct((t, d), _F32),
        compiler_params=_params("arbitrary"),
        name="ffn_down",
    )(a, w, h, nw)


def _mix_out_kernel(x1_ref, x2_ref, w_ref, h_ref, nw_ref, o_ref):
    c1 = x1_ref.shape[1]
    sub = x1_ref.shape[0] // 2
    for s in range(2):
        r = slice(s * sub, (s + 1) * sub)
        m = _dot(x1_ref[r, :], w_ref[0:c1, :]) + _dot(x2_ref[r, :], w_ref[c1:, :])
        o_ref[r, :] = h_ref[r, :] + _rms(m, nw_ref[...])


def _mix_out(x1, x2, w, lead, h, nw, tm=512):
    t, c1 = x1.shape
    c2 = x2.shape[1]
    d = w.shape[-1]
    return pl.pallas_call(
        _mix_out_kernel,
        grid=(t // tm,),
        in_specs=[pl.BlockSpec((tm, c1), lambda i: (i, 0)),
                  pl.BlockSpec((tm, c2), lambda i: (i, 0)),
                  _stacked((c1 + c2, d), lambda i: (0, 0), lead),
                  pl.BlockSpec((tm, d), lambda i: (i, 0)),
                  pl.BlockSpec((1, d), lambda i: (0, 0))],
        out_specs=pl.BlockSpec((tm, d), lambda i: (i, 0)),
        out_shape=jax.ShapeDtypeStruct((t, d), _F32),
        compiler_params=_params("arbitrary"),
        name="mix_out",
    )(x1, x2, w, h, nw)


def _rows(start, size, stride):
    return pl.ds(start, size) if stride == 1 else pl.ds(start, size, stride=stride)


def _dilated_kernel(q_ref, k_ref, v_ref, o_ref, s_s, p_s, pv_s, m_s, l_s):
    seq = q_ref.shape[2]
    blk = A_BACK
    chunk = 256

    qi = lax.broadcasted_iota(jnp.int32, (blk, blk), 0)
    kj = lax.broadcasted_iota(jnp.int32, (blk, blk), 1)
    cur_ok = kj <= qi
    both_ok = jnp.concatenate([kj >= qi, cur_ok], axis=1)
    ones = jnp.ones((blk, HEAD_DIM), _BF16)

    blocks = []
    for g, dil in enumerate(A_DILATIONS):
        for res in range(dil):
            prev = None
            for n in range(seq // (blk * dil)):
                rows = _rows(res + n * blk * dil, blk, dil)
                blocks.append((g, rows, prev))
                prev = rows

    def keys_bf16(rows):
        return k_ref[0, 0, rows, :].astype(_BF16)

    def values1_bf16(rows):
        return jnp.concatenate([v_ref[0, 0, rows, :].astype(_BF16), ones], axis=1)

    for b, (g, rows, prev) in enumerate(blocks):
        qb = q_ref[0, 0, rows, :].astype(_BF16)
        if prev is None:
            s_s[b, :, 0:blk] = jnp.where(cur_ok, _dot_t(qb, keys_bf16(rows)), -jnp.inf)
        else:
            k_all = jnp.concatenate([keys_bf16(prev), keys_bf16(rows)], axis=0)
            s_s[b] = jnp.where(both_ok, _dot_t(qb, k_all), -jnp.inf)

    for b, (g, rows, prev) in enumerate(blocks):
        width = blk if prev is None else 2 * blk
        s = s_s[b, :, 0:width]
        m = jnp.max(s, axis=-1, keepdims=True)
        p_s[b, :, 0:width] = jnp.exp2(s - m).astype(_BF16)
        m_s[g, rows, :] = jnp.broadcast_to(m, (blk, HEAD_DIM))

    for b, (g, rows, prev) in enumerate(blocks):
        if prev is None:
            pva = _dot(p_s[b, :, 0:blk], values1_bf16(rows))
        else:
            pva = _dot(p_s[b], jnp.concatenate([values1_bf16(prev), values1_bf16(rows)], axis=0))
        pv_s[g, rows, :] = pva[:, 0:HEAD_DIM]
        l_s[g, rows, :] = pva[:, HEAD_DIM:]

    for c in range(0, seq, chunk):
        r = slice(c, c + chunk)
        m_all = jnp.maximum(jnp.maximum(m_s[0, r, :], m_s[1, r, :]), m_s[2, r, :])
        num = jnp.zeros((chunk, HEAD_DIM), _F32)
        den = jnp.zeros((chunk, HEAD_DIM), _F32)
        for g in range(len(A_DILATIONS)):
            w = jnp.exp2(m_s[g, r, :] - m_all)
            num = num + w * pv_s[g, r, :]
            den = den + w * l_s[g, r, :]
        o_ref[0, r, :] = (num / den).astype(o_ref.dtype)


def _dilated_attention(slabs):
    batch, _, seq, _ = slabs.shape
    head = lambda off: pl.BlockSpec((1, 1, seq, LANE), lambda b, h: (b, off + h, 0, 0))
    nblocks = len(A_DILATIONS) * seq // A_BACK
    return pl.pallas_call(
        _dilated_kernel,
        grid=(batch, A_HEADS),
        in_specs=[head(0), head(A_HEADS), head(2 * A_HEADS)],
        out_specs=pl.BlockSpec((1, seq, LANE), lambda b, h: (b, 0, h)),
        out_shape=jax.ShapeDtypeStruct((batch, seq, A_HEADS * HEAD_DIM), _BF16),
        scratch_shapes=[pltpu.VMEM((nblocks, A_BACK, 2 * A_BACK), _F32),
                        pltpu.VMEM((nblocks, A_BACK, 2 * A_BACK), _BF16),
                        pltpu.VMEM((3, seq, HEAD_DIM), _F32), pltpu.VMEM((3, seq, HEAD_DIM), _F32),
                        pltpu.VMEM((3, seq, HEAD_DIM), _F32)],
        compiler_params=_params("arbitrary", "arbitrary"),
        name="dilated_attention",
    )(slabs, slabs, slabs)


def _pool_kernel(u_ref, w_ref, sc_ref, o_ref):
    seq = u_ref.shape[2]
    row = lax.broadcasted_iota(jnp.int32, (seq, LANE), 0)
    for g, win in enumerate(B_WINDOWS):
        x = u_ref[0, g]
        s = x
        step = 1
        while step < win:
            s = s + _shift_rows(s, step, row)
            step *= 2
        cnt = jnp.minimum(row + 1, win).astype(_F32)
        pooled = s / cnt - x
        mixed = _dot(pooled.astype(_BF16), w_ref[g]) * sc_ref[:, g * LANE:(g + 1) * LANE]
        o_ref[0, :, g * LANE:(g + 1) * LANE] = mixed.astype(o_ref.dtype)


def _multiscale_pool(slabs, pool_w, pool_scale):
    batch, _, seq, _ = slabs.shape
    first = 3 * A_HEADS // B_GROUPS
    return pl.pallas_call(
        _pool_kernel,
        grid=(batch,),
        in_specs=[pl.BlockSpec((1, B_GROUPS, seq, LANE), lambda b: (b, first, 0, 0)),
                  pl.BlockSpec((B_GROUPS, LANE, LANE), lambda b: (0, 0, 0)),
                  pl.BlockSpec((1, B_GROUPS * LANE), lambda b: (0, 0))],
        out_specs=pl.BlockSpec((1, seq, B_GROUPS * LANE), lambda b: (b, 0, 0)),
        out_shape=jax.ShapeDtypeStruct((batch, seq, B_GROUPS * LANE), _BF16),
        compiler_params=_params("arbitrary"),
        name="multiscale_pool",
    )(slabs, pool_w, pool_scale)


def _compress_kernel(x_ref, pe_ref, w1_ref, w2_ref, o_ref):
    nrow = o_ref.shape[3]
    hidden = w1_ref.shape[2]
    a = jnp.zeros((nrow, hidden), _F32)
    b = jnp.zeros((nrow, hidden), _F32)
    for tok in range(CMP_STRIDE):
        x = x_ref[0, 0, pl.ds(tok, nrow, stride=CMP_STRIDE), :]
        lo, hi = tok, CMP_STRIDE + tok
        a = a + _dot((x + pe_ref[0, lo:lo + 1, :]).astype(_BF16), w1_ref[0, lo * HEAD_DIM:(lo + 1) * HEAD_DIM, :])
        b = b + _dot((x + pe_ref[0, hi:hi + 1, :]).astype(_BF16), w1_ref[0, hi * HEAD_DIM:(hi + 1) * HEAD_DIM, :])
    hid = a + pltpu.roll(b, nrow - 1, 0)
    o_ref[0, 0, 0] = _dot(jax.nn.gelu(hid).astype(_BF16), w2_ref[0])


def _compress(slabs, pe, w1, w2, first_slab):
    batch, _, seq, _ = slabs.shape
    nrow = seq // CMP_STRIDE
    return pl.pallas_call(
        _compress_kernel,
        grid=(2, batch, C_KV_HEADS),
        in_specs=[pl.BlockSpec((1, 1, seq, LANE), lambda kv, b, g: (b, first_slab + 2 * kv + g, 0, 0)),
                  pl.BlockSpec((1, CMP_BLOCK, HEAD_DIM), lambda kv, b, g: (kv, 0, 0)),
                  pl.BlockSpec((1, CMP_BLOCK * HEAD_DIM, w1.shape[2]), lambda kv, b, g: (kv, 0, 0)),
                  pl.BlockSpec((1, w2.shape[1], HEAD_DIM), lambda kv, b, g: (kv, 0, 0))],
        out_specs=pl.BlockSpec((1, 1, 1, nrow, HEAD_DIM), lambda kv, b, g: (kv, b, g, 0, 0)),
        out_shape=jax.ShapeDtypeStruct((2, batch, C_KV_HEADS, nrow, HEAD_DIM), _F32),
        compiler_params=_params("arbitrary", "arbitrary", "arbitrary"),
        name="compress",
    )(slabs, pe, w1, w2)


def _cmp_select_kernel(q_ref, kc_ref, vc_ref, o_ref, sel_ref, *, nslc, n_sub):
    tq = q_ref.shape[2] // n_sub
    ncmp = kc_ref.shape[3]
    scale = HEAD_DIM ** -0.5
    n = lax.broadcasted_iota(jnp.int32, (tq, ncmp), 1)

    cj = lax.broadcasted_iota(jnp.int32, (LANE, ncmp), 0) * SLC_BLOCK
    cn = lax.broadcasted_iota(jnp.int32, (LANE, ncmp), 1) * CMP_STRIDE
    cover_t = ((cn < cj + SLC_BLOCK) & (cn + CMP_BLOCK > cj)).astype(_BF16)
    j = lax.broadcasted_iota(jnp.int32, (nslc, tq), 0)
    jf = j.astype(_F32)

    kc = kc_ref[0, 0, 0].astype(_BF16)
    vc = vc_ref[0, 0, 0].astype(_BF16)
    for u in range(n_sub):
        rows = slice(u * tq, (u + 1) * tq)
        t0 = (pl.program_id(2) * n_sub + u) * tq
        t = t0 + lax.broadcasted_iota(jnp.int32, (tq, ncmp), 0)
        cmask = n * CMP_STRIDE + (CMP_BLOCK - 1) <= t
        imp_t = jnp.zeros((LANE, tq), _F32)
        for hh in range(C_GROUP):
            sc = jnp.where(cmask, _dot_t(q_ref[0, hh, rows, :].astype(_BF16), kc) * scale, -jnp.inf)
            m = jnp.max(sc, axis=-1, keepdims=True)
            m = jnp.where(jnp.isfinite(m), m, 0.0)
            p = jnp.exp(sc - m)
            den = jnp.sum(p, axis=-1, keepdims=True)
            p = (p / jnp.maximum(den, 1.0)).astype(_BF16)
            o_ref[0, hh, rows, :] = _dot(p, vc)
            imp_t = imp_t + _dot_t(cover_t, p)

        tt = t0 + lax.broadcasted_iota(jnp.int32, (nslc, tq), 1)
        cur = lax.shift_right_logical(tt, SLC_BLOCK.bit_length() - 1)
        visible = j <= cur
        forced = visible & ((j == 0) | (j >= cur - 1))
        score = jnp.where(forced, FORCED_SCORE, jnp.where(visible, imp_t[0:nslc, :], -FORCED_SCORE))
        sel_t = jnp.zeros((nslc, tq), _F32)
        for _ in range(min(SLC_TOPN, nslc)):
            best = jnp.max(score, axis=0, keepdims=True)
            pick = jnp.min(jnp.where(score == best, jf, float(nslc)), axis=0, keepdims=True)
            hit = jf == pick
            sel_t = jnp.where(hit, 1.0, sel_t)
            score = jnp.where(hit, -jnp.inf, score)
        sel_t = jnp.concatenate([sel_t, jnp.zeros((LANE - nslc, tq), _F32)], axis=0)
        sel_ref[0, 0, rows, :] = sel_t.T.astype(sel_ref.dtype)


def _cmp_select(slabs, kv_cmp, tq=512, n_sub=2):
    batch, _, seq, _ = slabs.shape
    cmp_spec = lambda which: pl.BlockSpec((1, 1, 1) + kv_cmp.shape[3:], lambda b, g, i: (which, b, g, 0, 0))
    return pl.pallas_call(
        functools.partial(_cmp_select_kernel, nslc=seq // SLC_BLOCK, n_sub=n_sub),
        grid=(batch, C_KV_HEADS, seq // tq),
        in_specs=[pl.BlockSpec((1, C_GROUP, tq, LANE), lambda b, g, i: (b, g, i, 0)),
                  cmp_spec(0), cmp_spec(1)],
        out_specs=[pl.BlockSpec((1, C_GROUP, tq, LANE), lambda b, g, i: (b, g, i, 0)),
                   pl.BlockSpec((1, 1, tq, LANE), lambda b, g, i: (b, g, i, 0))],
        out_shape=[jax.ShapeDtypeStruct((batch, C_HEADS, seq, HEAD_DIM), _F32),
                   jax.ShapeDtypeStruct((batch, C_KV_HEADS, seq, LANE), _BF16)],
        compiler_params=_params("arbitrary", "arbitrary", "arbitrary"),
        name="cmp_select",
    )(slabs, kv_cmp, kv_cmp)


def _group_softmax_pv(q, k, v1, bias_s, s_s, p_s, o_s):
    nk = k.shape[0]
    tq = bias_s.shape[0]
    pieces = [slice(c, min(c + 512, nk)) for c in range(0, nk, 512)]
    s = _dot_t(q, k)
    for hh in range(C_GROUP):
        r = slice(hh * tq, (hh + 1) * tq)
        s_s[r, 0:nk] = s[r] + bias_s[:, 0:nk]
    for hh in range(C_GROUP):
        r = slice(hh * tq, (hh + 1) * tq)
        m = None
        for c in pieces:
            mc = jnp.max(s_s[r, c], axis=-1, keepdims=True)
            m = mc if m is None else jnp.maximum(m, mc)
        for c in pieces:
            p_s[r, c] = jnp.exp2(s_s[r, c] - m).astype(_BF16)
    o = _dot(p_s[:, 0:nk], v1)
    o_s[...] = o[:, 0:HEAD_DIM] / o[:, HEAD_DIM:]


def _nsa_kernel(q_ref, ks_ref, vs_ref, kw_ref, vw_ref, sel_ref, oc_ref, gl_ref, cos_ref, sin_ref, o_ref,
                ksr, vs1, kwr, vw1, expand_s, qr, bias_s, s_s, p_s, bias_w, s_w, p_w, oslc_s, owin_s):
    n_sub = qr.shape[0]
    tq = q_ref.shape[2] // n_sub
    seq = ks_ref.shape[2]
    qi = pl.program_id(2)
    chunk = 256
    shift = SLC_BLOCK.bit_length() - 1

    @pl.when(qi == 0)
    def _():
        def prep(c, _):
            r = pl.ds(pl.multiple_of(c * chunk, chunk), chunk)
            cos, sin = cos_ref[0, r, :], sin_ref[0, r, :]
            ones = jnp.ones((chunk, HEAD_DIM), _BF16)
            ksr[r, :] = _rope(ks_ref[0, 0, r, :], cos, sin).astype(_BF16)
            kwr[r, :] = _rope(kw_ref[0, 0, r, :], cos, sin).astype(_BF16)
            vs1[r, 0:HEAD_DIM] = vs_ref[0, 0, r, :].astype(_BF16)
            vs1[r, HEAD_DIM:] = ones
            vw1[r, 0:HEAD_DIM] = vw_ref[0, 0, r, :].astype(_BF16)
            vw1[r, HEAD_DIM:] = ones
            return 0

        lax.fori_loop(0, seq // chunk, prep, 0)
        blk = lax.broadcasted_iota(jnp.int32, (LANE, seq), 0)
        key = lax.broadcasted_iota(jnp.int32, (LANE, seq), 1)
        expand_s[...] = (blk == lax.shift_right_logical(key, shift)).astype(_BF16)

    base = pl.multiple_of(qi * (n_sub * tq), n_sub * tq)
    sub_rows = [slice(u * tq, (u + 1) * tq) for u in range(n_sub)]
    for u in range(n_sub):
        cos_q = cos_ref[0, pl.ds(base + u * tq, tq), :]
        sin_q = sin_ref[0, pl.ds(base + u * tq, tq), :]
        for hh in range(C_GROUP):
            q_rot = _rope(q_ref[0, hh, sub_rows[u], :], cos_q, sin_q)
            qr[u, hh * tq:(hh + 1) * tq, :] = (q_rot * (HEAD_DIM ** -0.5 * LOG2_E)).astype(_BF16)

    def window_branch(u):
        q0 = base + u * tq
        span = WIN_SIZE + tq
        k0 = pl.multiple_of(jnp.maximum(q0 - WIN_SIZE, 0), tq)
        t = q0 + lax.broadcasted_iota(jnp.int32, (tq, span), 0)
        key = k0 + lax.broadcasted_iota(jnp.int32, (tq, span), 1)
        bias_w[u] = jnp.where((key <= t) & (key > t - WIN_SIZE), 0.0, -jnp.inf)
        _group_softmax_pv(qr[u], kwr[pl.ds(k0, span), :], vw1[pl.ds(k0, span), :],
                          bias_w.at[u], s_w.at[u], p_w.at[u], owin_s.at[u])

    step = n_sub * tq
    for nt in range(1, seq // step + 1):
        @pl.when(qi + 1 == nt)
        def _(nk=nt * step):
            for u in range(n_sub):
                for c in range(0, nk, step):
                    t = base + u * tq + lax.broadcasted_iota(jnp.int32, (tq, step), 0)
                    key = c + lax.broadcasted_iota(jnp.int32, (tq, step), 1)
                    ok = (_dot(sel_ref[0, 0, sub_rows[u], :], expand_s[:, c:c + step]) > 0.5) & (key <= t)
                    bias_s[u, :, c:c + step] = jnp.where(ok, 0.0, -jnp.inf)
                _group_softmax_pv(qr[u], ksr[0:nk, :], vs1[0:nk, :],
                                  bias_s.at[u], s_s.at[u], p_s.at[u], oslc_s.at[u])
                window_branch(u)

    gates = jax.nn.sigmoid(gl_ref[0, 0])
    for u in range(n_sub):
        for hh in range(C_GROUP):
            r = slice(hh * tq, (hh + 1) * tq)
            g_cmp, g_slc, g_win = (gates[sub_rows[u], 3 * hh + c:3 * hh + c + 1] for c in range(3))
            o = g_cmp * oc_ref[0, hh, sub_rows[u], :] + g_slc * oslc_s[u, r, :] + g_win * owin_s[u, r, :]
            o_ref[0, sub_rows[u], hh * HEAD_DIM:(hh + 1) * HEAD_DIM] = o.astype(o_ref.dtype)


def _nsa_attention(slabs, sel, o_cmp, cos, sin, tq=128, n_sub=2):
    batch, _, seq, _ = slabs.shape
    kv = lambda off: pl.BlockSpec((1, 1, seq, LANE), lambda b, g, i: (b, off + g, 0, 0))
    table = pl.BlockSpec((1, seq, LANE), lambda b, g, i: (b, 0, 0))
    rows = C_GROUP * tq
    span = WIN_SIZE + tq
    tile = n_sub * tq
    return pl.pallas_call(
        _nsa_kernel,
        grid=(batch, C_KV_HEADS, seq // tile),
        in_specs=[pl.BlockSpec((1, C_GROUP, tile, LANE), lambda b, g, i: (b, g, i, 0)),
                  kv(16), kv(18), kv(20), kv(22),
                  pl.BlockSpec((1, 1, tile, LANE), lambda b, g, i: (b, g, i, 0)),
                  pl.BlockSpec((1, C_GROUP, tile, LANE), lambda b, g, i: (b, g, i, 0)),
                  pl.BlockSpec((1, 1, tile, LANE), lambda b, g, i: (b, 24 + g, i, 0)),
                  table, table],
        out_specs=pl.BlockSpec((1, tile, C_GROUP * HEAD_DIM), lambda b, g, i: (b, i, g)),
        out_shape=jax.ShapeDtypeStruct((batch, seq, C_HEADS * HEAD_DIM), _BF16),
        scratch_shapes=[pltpu.VMEM((seq, HEAD_DIM), _BF16), pltpu.VMEM((seq, 2 * HEAD_DIM), _BF16)] * 2
        + [pltpu.VMEM((LANE, seq), _BF16), pltpu.VMEM((n_sub, rows, HEAD_DIM), _BF16),
           pltpu.VMEM((n_sub, tq, seq), _F32), pltpu.VMEM((n_sub, rows, seq), _F32),
           pltpu.VMEM((n_sub, rows, seq), _BF16),
           pltpu.VMEM((n_sub, tq, span), _F32), pltpu.VMEM((n_sub, rows, span), _F32),
           pltpu.VMEM((n_sub, rows, span), _BF16)]
        + [pltpu.VMEM((n_sub, rows, HEAD_DIM), _F32)] * 2,
        compiler_params=_params("arbitrary", "arbitrary", "arbitrary"),
        name="nsa_attention",
    )(slabs, slabs, slabs, slabs, slabs, sel, o_cmp, slabs, cos, sin)


def _conv_kernel(u_ref, c_ref, b_ref, w_ref, o_ref):
    seq = u_ref.shape[2]
    row = lax.broadcasted_iota(jnp.int32, (seq, LANE), 0)
    u = c_ref[0, 0] * u_ref[0, 0]
    taps = w_ref.shape[0]
    conv = _shift_rows(u, taps - 1, row) * w_ref[0:1, :]
    for j in range(1, taps):
        shifted = u if j == taps - 1 else _shift_rows(u, taps - 1 - j, row)
        conv = conv + shifted * w_ref[j:j + 1, :]
    o_ref[0] = (b_ref[0, 0] * conv).astype(o_ref.dtype)


def _short_conv(slabs, conv_w, first_slab):
    batch, _, seq, _ = slabs.shape
    nd = D_WIDTH // LANE
    part = lambda off: pl.BlockSpec((1, 1, seq, LANE), lambda b, j: (b, first_slab + off + j, 0, 0))
    return pl.pallas_call(
        _conv_kernel,
        grid=(batch, nd),
        in_specs=[part(0), part(nd), part(2 * nd),
                  pl.BlockSpec((conv_w.shape[0], LANE), lambda b, j: (0, j))],
        out_specs=pl.BlockSpec((1, seq, LANE), lambda b, j: (b, 0, j)),
        out_shape=jax.ShapeDtypeStruct((batch, seq, D_WIDTH), _BF16),
        compiler_params=_params("arbitrary", "arbitrary"),
        name="short_conv",
    )(slabs, slabs, slabs, conv_w)


def _pad_cols(w, total):
    return jnp.pad(w, ((0, 0), (0, total - w.shape[1])))


def _odd_in_weight(w):
    qkv = C_HEADS * HEAD_DIM + 6 * C_KV_HEADS * HEAD_DIM
    ngate = 3 * C_GROUP
    gates = [_pad_cols(w[:, qkv + g * ngate:qkv + (g + 1) * ngate], LANE) for g in range(C_KV_HEADS)]
    rest = w[:, qkv + C_KV_HEADS * ngate:]
    return _pad_cols(jnp.concatenate([w[:, :qkv]] + gates + [rest], axis=1), PROJ_SLABS * LANE)


def kernel(x, positions, norm_w, ffn_w_gate, ffn_w_up, ffn_w_down, ev_w_in, ev_w_out, pool_w, pool_scale,
           od_w_in, od_w_out, cmp_pe_k, cmp_w1_k, cmp_w2_k, cmp_pe_v, cmp_w1_v, cmp_w2_v, conv_w):
    batch, seq, d = x.shape
    depth = norm_w.shape[0]
    bf = lambda a: a.astype(_BF16)

    inv_freq = 1.0 / (ROPE_THETA ** (jnp.arange(0, HEAD_DIM, 2, dtype=_F32) / HEAD_DIM))
    ang = positions.astype(_F32)[..., None] * inv_freq
    cos = jnp.concatenate([jnp.cos(ang), jnp.cos(ang)], axis=-1)
    sin = jnp.concatenate([-jnp.sin(ang), jnp.sin(ang)], axis=-1)

    nw = norm_w.reshape(depth, 6, 1, d)
    h = x.reshape(batch * seq, d)

    w_gate, w_up, w_down = bf(ffn_w_gate), bf(ffn_w_up), bf(ffn_w_down)
    w_ev_in, w_ev_out = bf(ev_w_in), bf(ev_w_out)
    w_od_in, w_od_out = bf(jax.vmap(_odd_in_weight)(od_w_in)), bf(od_w_out)

    def ffn(h, layer, which):
        a = _ffn_up(h, nw[layer, 4 * which], w_gate, w_up, (layer, which))
        return _ffn_down(a, w_down, (layer, which), h, nw[layer, 4 * which + 1], HALF_STEP)

    for layer in range(depth):
        i = layer // 2
        h = ffn(h, layer, 0)
        if layer % 2 == 0:
            slabs = _norm_proj(h, nw[layer, 2], w_ev_in, (i,), batch, seq, rope=(cos, sin),
                               n_rope=2 * A_HEADS, n_scaled=A_HEADS, q_scale=HEAD_DIM ** -0.5 * LOG2_E)
            o_a = _dilated_attention(slabs)
            o_b = _multiscale_pool(slabs, bf(pool_w[i]), pool_scale[i].reshape(1, -1))
            h = _mix_out(o_a.reshape(batch * seq, -1), o_b.reshape(batch * seq, -1), w_ev_out, (i,), h,
                         nw[layer, 3])
        else:
            slabs = _norm_proj(h, nw[layer, 2], w_od_in, (i,), batch, seq)
            kv_cmp = _compress(slabs, jnp.stack([cmp_pe_k[i], cmp_pe_v[i]]),
                               bf(jnp.stack([cmp_w1_k[i], cmp_w1_v[i]])),
                               bf(jnp.stack([cmp_w2_k[i], cmp_w2_v[i]])), C_HEADS)
            o_cmp, sel = _cmp_select(slabs, kv_cmp)
            o_c = _nsa_attention(slabs, sel, o_cmp, cos, sin)
            y_d = _short_conv(slabs, conv_w[i], 26)
            h = _mix_out(o_c.reshape(batch * seq, -1), y_d.reshape(batch * seq, -1), w_od_out, (i,), h,
                         nw[layer, 3])
        h = ffn(h, layer, 1)
    return h.reshape(batch, seq, d)
```

```python
import functools

import jax
import jax.numpy as jnp
from jax import lax
from jax.experimental import pallas as pl
from jax.experimental.pallas import tpu as pltpu

HEAD_DIM = 128
ROPE_THETA = 10000.0
NORM_EPS = 1e-6
LOG2_E = 1.4426950408889634
HALF_STEP = 0.5
LANE = 128

A_HEADS = 12
A_DILATIONS = (1, 4, 16)
A_BACK = 128
B_WINDOWS = (2, 4, 8, 16)
B_GROUPS = 4

C_HEADS = 12
C_KV_HEADS = 2
C_GROUP = C_HEADS // C_KV_HEADS
CMP_BLOCK = 32
CMP_STRIDE = 16
SLC_BLOCK = 64
SLC_TOPN = 8
WIN_SIZE = 512
FORCED_SCORE = 1e9
D_WIDTH = 512
PROJ_SLABS = 40
NORM_SUB_TILES = 4

VMEM_LIMIT = 56 * 1024 * 1024

_BF16 = jnp.bfloat16
_F32 = jnp.float32


def _params(*sem):
    return pltpu.CompilerParams(dimension_semantics=sem, vmem_limit_bytes=VMEM_LIMIT)


def _rms(x, w):
    return x * lax.rsqrt(jnp.mean(x * x, axis=-1, keepdims=True) + NORM_EPS) * w


def _dot(a, b):
    return jnp.dot(a, b, preferred_element_type=_F32)


def _dot_t(a, b):
    return lax.dot_general(a, b, (((1,), (1,)), ((), ())), preferred_element_type=_F32)


def _rope(x, cos, sin):
    return x * cos + pltpu.roll(x, HEAD_DIM // 2, 1) * sin


def _shift_rows(x, k, row):
    return jnp.where(row >= k, pltpu.roll(x, k, 0), 0.0)


def _proj_kernel(h_ref, nw_ref, w_ref, *rest, n_rope, n_scaled, q_scale):
    if n_rope:
        cos_ref, sin_ref, o_ref, xn_ref = rest
    else:
        o_ref, xn_ref = rest
    per_step = o_ref.shape[1]
    tn = per_step * LANE

    def use(xn, r, step):
        if step is None:
            cols = pl.ds(pl.multiple_of(pl.program_id(1) * tn, tn), tn)
        else:
            cols = slice(step * tn, (step + 1) * tn)
        acc = _dot(xn, w_ref[:, cols])
        for s in range(per_step):
            x = acc[:, s * LANE:(s + 1) * LANE]
            slab = None if step is None else step * per_step + s
            if slab is not None and slab < n_rope:
                x = _rope(x, cos_ref[0, r, :], sin_ref[0, r, :])
                if slab < n_scaled:
                    x = x * q_scale
            o_ref[0, s, r, :] = x

    steps = w_ref.shape[1] // tn if n_rope else None
    _normed_rows(h_ref, nw_ref, xn_ref, use, static_steps=steps)


def _stacked(block, index_map, lead, **kwargs):
    return pl.BlockSpec((None,) * len(lead) + block, lambda *g: tuple(lead) + index_map(*g), **kwargs)


def _norm_proj(h, nw, w, lead, batch, seq, rope=None, n_rope=0, n_scaled=0, q_scale=1.0, tm=1024, tn=1024):
    t, d = h.shape
    n = w.shape[-1]
    per_b = seq // tm
    table = pl.BlockSpec((1, tm, LANE), lambda i, j: (i // per_b, i % per_b, 0))
    return pl.pallas_call(
        functools.partial(_proj_kernel, n_rope=n_rope, n_scaled=n_scaled, q_scale=q_scale),
        grid=(t // tm, n // tn),
        in_specs=[pl.BlockSpec((tm, d), lambda i, j: (i, 0)),
                  pl.BlockSpec((1, d), lambda i, j: (0, 0)),
                  _stacked((d, n), lambda i, j: (0, 0), lead, pipeline_mode=pl.Buffered(1))]
        + ([table, table] if n_rope else []),
        out_specs=pl.BlockSpec((1, tn // LANE, tm, LANE), lambda i, j: (i // per_b, j, i % per_b, 0)),
        out_shape=jax.ShapeDtypeStruct((batch, n // LANE, seq, LANE), _F32),
        scratch_shapes=[pltpu.VMEM((tm, d), _BF16)],
        compiler_params=_params("arbitrary", "arbitrary"),
        name="norm_proj",
    )(h, nw, w, *(rope if n_rope else ()))


def _normed_rows(h_ref, nw_ref, xn_ref, use, static_steps=None):
    j = pl.program_id(1)

    def first(step):
        sub = h_ref.shape[0] // NORM_SUB_TILES
        for s in range(NORM_SUB_TILES):
            r = slice(s * sub, (s + 1) * sub)
            xn = _rms(h_ref[r, :], nw_ref[...]).astype(_BF16)
            xn_ref[r, :] = xn
            use(xn, r, step)

    if static_steps is None:
        pl.when(j == 0)(lambda: first(None))
        pl.when(j > 0)(lambda: use(xn_ref[...], slice(None), None))
    else:
        pl.when(j == 0)(lambda: first(0))
        for step in range(1, static_steps):
            pl.when(j == step)(lambda step=step: use(xn_ref[...], slice(None), step))


def _ffn_up_kernel(h_ref, nw_ref, wg_ref, wu_ref, o_ref, xn_ref):
    def use(xn, r, step):
        g = _dot(xn, wg_ref[...])
        u = _dot(xn, wu_ref[...])
        o_ref[r, :] = (jax.nn.silu(g) * u).astype(_BF16)

    _normed_rows(h_ref, nw_ref, xn_ref, use)


def _ffn_up(h, nw, wg, wu, lead, tm=1024, tn=512):
    t, d = h.shape
    f = wg.shape[-1]
    return pl.pallas_call(
        _ffn_up_kernel,
        grid=(t // tm, f // tn),
        in_specs=[pl.BlockSpec((tm, d), lambda i, j: (i, 0)),
                  pl.BlockSpec((1, d), lambda i, j: (0, 0)),
                  _stacked((d, tn), lambda i, j: (0, j), lead),
                  _stacked((d, tn), lambda i, j: (0, j), lead)],
        out_specs=pl.BlockSpec((tm, tn), lambda i, j: (i, j)),
        out_shape=jax.ShapeDtypeStruct((t, f), _BF16),
        scratch_shapes=[pltpu.VMEM((tm, d), _BF16)],
        compiler_params=_params("arbitrary", "arbitrary"),
        name="ffn_up",
    )(h, nw, wg, wu)


def _ffn_down_kernel(a_ref, w_ref, h_ref, nw_ref, o_ref, *, scale, n_sub):
    sub = a_ref.shape[0] // n_sub
    for s in range(n_sub):
        r = slice(s * sub, (s + 1) * sub)
        f = _dot(a_ref[r, :], w_ref[...])
        o_ref[r, :] = h_ref[r, :] + scale * _rms(f, nw_ref[...])


def _ffn_down(a, w, lead, h, nw, scale, tm=512, n_sub=2):
    t, f = a.shape
    d = w.shape[-1]
    w_spec = _stacked((f, d), lambda i: (0, 0), lead, pipeline_mode=pl.Buffered(1))
    return pl.pallas_call(
        functools.partial(_ffn_down_kernel, scale=scale, n_sub=n_sub),
        grid=(t // tm,),
        in_specs=[pl.BlockSpec((tm, f), lambda i: (i, 0)),
                  w_spec,
                  pl.BlockSpec((tm, d), lambda i: (i, 0)),
                  pl.BlockSpec((1, d), lambda i: (0, 0))],
        out_specs=pl.BlockSpec((tm, d), lambda i: (i, 0)),
        out_shape=jax.ShapeDtypeStruct((t, d), _F32),
        compiler_params=_params("arbitrary"),
        name="ffn_down",
    )(a, w, h, nw)


def _mix_out_kernel(x1_ref, x2_ref, w_ref, h_ref, nw_ref, o_ref):
    c1 = x1_ref.shape[1]
    sub = x1_ref.shape[0] // 2
    for s in range(2):
        r = slice(s * sub, (s + 1) * sub)
        m = _dot(x1_ref[r, :], w_ref[0:c1, :]) + _dot(x2_ref[r, :], w_ref[c1:, :])
        o_ref[r, :] = h_ref[r, :] + _rms(m, nw_ref[...])


def _mix_out(x1, x2, w, lead, h, nw, tm=512):
    t, c1 = x1.shape
    c2 = x2.shape[1]
    d = w.shape[-1]
    return pl.pallas_call(
        _mix_out_kernel,
        grid=(t // tm,),
        in_specs=[pl.BlockSpec((tm, c1), lambda i: (i, 0)),
                  pl.BlockSpec((tm, c2), lambda i: (i, 0)),
                  _stacked((c1 + c2, d), lambda i: (0, 0), lead),
                  pl.BlockSpec((tm, d), lambda i: (i, 0)),
                  pl.BlockSpec((1, d), lambda i: (0, 0))],
        out_specs=pl.BlockSpec((tm, d), lambda i: (i, 0)),
        out_shape=jax.ShapeDtypeStruct((t, d), _F32),
        compiler_params=_params("arbitrary"),
        name="mix_out",
    )(x1, x2, w, h, nw)


def _rows(start, size, stride):
    return pl.ds(start, size) if stride == 1 else pl.ds(start, size, stride=stride)


def _dilated_kernel(q_ref, k_ref, v_ref, o_ref, s_s, p_s, pv_s, m_s, l_s):
    seq = q_ref.shape[2]
    blk = A_BACK
    chunk = 256

    qi = lax.broadcasted_iota(jnp.int32, (blk, blk), 0)
    kj = lax.broadcasted_iota(jnp.int32, (blk, blk), 1)
    cur_ok = kj <= qi
    both_ok = jnp.concatenate([kj >= qi, cur_ok], axis=1)
    ones = jnp.ones((blk, HEAD_DIM), _BF16)

    blocks = []
    for g, dil in enumerate(A_DILATIONS):
        for res in range(dil):
            prev = None
            for n in range(seq // (blk * dil)):
                rows = _rows(res + n * blk * dil, blk, dil)
                blocks.append((g, rows, prev))
                prev = rows

    def keys_bf16(rows):
        return k_ref[0, 0, rows, :].astype(_BF16)

    def values1_bf16(rows):
        return jnp.concatenate([v_ref[0, 0, rows, :].astype(_BF16), ones], axis=1)

    for b, (g, rows, prev) in enumerate(blocks):
        qb = q_ref[0, 0, rows, :].astype(_BF16)
        if prev is None:
            s_s[b, :, 0:blk] = jnp.where(cur_ok, _dot_t(qb, keys_bf16(rows)), -jnp.inf)
        else:
            k_all = jnp.concatenate([keys_bf16(prev), keys_bf16(rows)], axis=0)
            s_s[b] = jnp.where(both_ok, _dot_t(qb, k_all), -jnp.inf)

    for b, (g, rows, prev) in enumerate(blocks):
        width = blk if prev is None else 2 * blk
        s = s_s[b, :, 0:width]
        m = jnp.max(s, axis=-1, keepdims=True)
        p_s[b, :, 0:width] = jnp.exp2(s - m).astype(_BF16)
        m_s[g, rows, :] = jnp.broadcast_to(m, (blk, HEAD_DIM))

    for b, (g, rows, prev) in enumerate(blocks):
        if prev is None:
            pva = _dot(p_s[b, :, 0:blk], values1_bf16(rows))
        else:
            pva = _dot(p_s[b], jnp.concatenate([values1_bf16(prev), values1_bf16(rows)], axis=0))
        pv_s[g, rows, :] = pva[:, 0:HEAD_DIM]
        l_s[g, rows, :] = pva[:, HEAD_DIM:]

    for c in range(0, seq, chunk):
        r = slice(c, c + chunk)
        m_all = jnp.maximum(jnp.maximum(m_s[0, r, :], m_s[1, r, :]), m_s[2, r, :])
        num = jnp.zeros((chunk, HEAD_DIM), _F32)
        den = jnp.zeros((chunk, HEAD_DIM), _F32)
        for g in range(len(A_DILATIONS)):
            w = jnp.exp2(m_s[g, r, :] - m_all)
            num = num + w * pv_s[g, r, :]
            den = den + w * l_s[g, r, :]
        o_ref[0, r, :] = (num / den).astype(o_ref.dtype)


def _dilated_attention(slabs):
    batch, _, seq, _ = slabs.shape
    head = lambda off: pl.BlockSpec((1, 1, seq, LANE), lambda b, h: (b, off + h, 0, 0))
    nblocks = len(A_DILATIONS) * seq // A_BACK
    return pl.pallas_call(
        _dilated_kernel,
        grid=(batch, A_HEADS),
        in_specs=[head(0), head(A_HEADS), head(2 * A_HEADS)],
        out_specs=pl.BlockSpec((1, seq, LANE), lambda b, h: (b, 0, h)),
        out_shape=jax.ShapeDtypeStruct((batch, seq, A_HEADS * HEAD_DIM), _BF16),
        scratch_shapes=[pltpu.VMEM((nblocks, A_BACK, 2 * A_BACK), _F32),
                        pltpu.VMEM((nblocks, A_BACK, 2 * A_BACK), _BF16),
                        pltpu.VMEM((3, seq, HEAD_DIM), _F32), pltpu.VMEM((3, seq, HEAD_DIM), _F32),
                        pltpu.VMEM((3, seq, HEAD_DIM), _F32)],
        compiler_params=_params("arbitrary", "arbitrary"),
        name="dilated_attention",
    )(slabs, slabs, slabs)


def _pool_kernel(u_ref, w_ref, sc_ref, o_ref):
    seq = u_ref.shape[2]
    row = lax.broadcasted_iota(jnp.int32, (seq, LANE), 0)
    for g, win in enumerate(B_WINDOWS):
        x = u_ref[0, g]
        s = x
        step = 1
        while step < win:
            s = s + _shift_rows(s, step, row)
            step *= 2
        cnt = jnp.minimum(row + 1, win).astype(_F32)
        pooled = s / cnt - x
        mixed = _dot(pooled.astype(_BF16), w_ref[g]) * sc_ref[:, g * LANE:(g + 1) * LANE]
        o_ref[0, :, g * LANE:(g + 1) * LANE] = mixed.astype(o_ref.dtype)


def _multiscale_pool(slabs, pool_w, pool_scale):
    batch, _, seq, _ = slabs.shape
    first = 3 * A_HEADS // B_GROUPS
    return pl.pallas_call(
        _pool_kernel,
        grid=(batch,),
        in_specs=[pl.BlockSpec((1, B_GROUPS, seq, LANE), lambda b: (b, first, 0, 0)),
                  pl.BlockSpec((B_GROUPS, LANE, LANE), lambda b: (0, 0, 0)),
                  pl.BlockSpec((1, B_GROUPS * LANE), lambda b: (0, 0))],
        out_specs=pl.BlockSpec((1, seq, B_GROUPS * LANE), lambda b: (b, 0, 0)),
        out_shape=jax.ShapeDtypeStruct((batch, seq, B_GROUPS * LANE), _BF16),
        compiler_params=_params("arbitrary"),
        name="multiscale_pool",
    )(slabs, pool_w, pool_scale)


def _compress_kernel(x_ref, pe_ref, w1_ref, w2_ref, o_ref):
    nrow = o_ref.shape[3]
    hidden = w1_ref.shape[2]
    a = jnp.zeros((nrow, hidden), _F32)
    b = jnp.zeros((nrow, hidden), _F32)
    for tok in range(CMP_STRIDE):
        x = x_ref[0, 0, pl.ds(tok, nrow, stride=CMP_STRIDE), :]
        lo, hi = tok, CMP_STRIDE + tok
        a = a + _dot((x + pe_ref[0, lo:lo + 1, :]).astype(_BF16), w1_ref[0, lo * HEAD_DIM:(lo + 1) * HEAD_DIM, :])
        b = b + _dot((x + pe_ref[0, hi:hi + 1, :]).astype(_BF16), w1_ref[0, hi * HEAD_DIM:(hi + 1) * HEAD_DIM, :])
    hid = a + pltpu.roll(b, nrow - 1, 0)
    o_ref[0, 0, 0] = _dot(jax.nn.gelu(hid).astype(_BF16), w2_ref[0])


def _compress(slabs, pe, w1, w2, first_slab):
    batch, _, seq, _ = slabs.shape
    nrow = seq // CMP_STRIDE
    return pl.pallas_call(
        _compress_kernel,
        grid=(2, batch, C_KV_HEADS),
        in_specs=[pl.BlockSpec((1, 1, seq, LANE), lambda kv, b, g: (b, first_slab + 2 * kv + g, 0, 0)),
                  pl.BlockSpec((1, CMP_BLOCK, HEAD_DIM), lambda kv, b, g: (kv, 0, 0)),
                  pl.BlockSpec((1, CMP_BLOCK * HEAD_DIM, w1.shape[2]), lambda kv, b, g: (kv, 0, 0)),
                  pl.BlockSpec((1, w2.shape[1], HEAD_DIM), lambda kv, b, g: (kv, 0, 0))],
        out_specs=pl.BlockSpec((1, 1, 1, nrow, HEAD_DIM), lambda kv, b, g: (kv, b, g, 0, 0)),
        out_shape=jax.ShapeDtypeStruct((2, batch, C_KV_HEADS, nrow, HEAD_DIM), _F32),
        compiler_params=_params("arbitrary", "arbitrary", "arbitrary"),
        name="compress",
    )(slabs, pe, w1, w2)


def _cmp_select_kernel(q_ref, kc_ref, vc_ref, o_ref, sel_ref, *, nslc, n_sub):
    tq = q_ref.shape[2] // n_sub
    ncmp = kc_ref.shape[3]
    scale = HEAD_DIM ** -0.5
    n = lax.broadcasted_iota(jnp.int32, (tq, ncmp), 1)

    cj = lax.broadcasted_iota(jnp.int32, (LANE, ncmp), 0) * SLC_BLOCK
    cn = lax.broadcasted_iota(jnp.int32, (LANE, ncmp), 1) * CMP_STRIDE
    cover_t = ((cn < cj + SLC_BLOCK) & (cn + CMP_BLOCK > cj)).astype(_BF16)
    j = lax.broadcasted_iota(jnp.int32, (nslc, tq), 0)
    jf = j.astype(_F32)

    kc = kc_ref[0, 0, 0].astype(_BF16)
    vc = vc_ref[0, 0, 0].astype(_BF16)
    for u in range(n_sub):
        rows = slice(u * tq, (u + 1) * tq)
        t0 = (pl.program_id(2) * n_sub + u) * tq
        t = t0 + lax.broadcasted_iota(jnp.int32, (tq, ncmp), 0)
        cmask = n * CMP_STRIDE + (CMP_BLOCK - 1) <= t
        imp_t = jnp.zeros((LANE, tq), _F32)
        for hh in range(C_GROUP):
            sc = jnp.where(cmask, _dot_t(q_ref[0, hh, rows, :].astype(_BF16), kc) * scale, -jnp.inf)
            m = jnp.max(sc, axis=-1, keepdims=True)
            m = jnp.where(jnp.isfinite(m), m, 0.0)
            p = jnp.exp(sc - m)
            den = jnp.sum(p, axis=-1, keepdims=True)
            p = (p / jnp.maximum(den, 1.0)).astype(_BF16)
            o_ref[0, hh, rows, :] = _dot(p, vc)
            imp_t = imp_t + _dot_t(cover_t, p)

        tt = t0 + lax.broadcasted_iota(jnp.int32, (nslc, tq), 1)
        cur = lax.shift_right_logical(tt, SLC_BLOCK.bit_length() - 1)
        visible = j <= cur
        forced = visible & ((j == 0) | (j >= cur - 1))
        score = jnp.where(forced, FORCED_SCORE, jnp.where(visible, imp_t[0:nslc, :], -FORCED_SCORE))
        sel_t = jnp.zeros((nslc, tq), _F32)
        for _ in range(min(SLC_TOPN, nslc)):
            best = jnp.max(score, axis=0, keepdims=True)
            pick = jnp.min(jnp.where(score == best, jf, float(nslc)), axis=0, keepdims=True)
            hit = jf == pick
            sel_t = jnp.where(hit, 1.0, sel_t)
            score = jnp.where(hit, -jnp.inf, score)
        sel_t = jnp.concatenate([sel_t, jnp.zeros((LANE - nslc, tq), _F32)], axis=0)
        sel_ref[0, 0, rows, :] = sel_t.T.astype(sel_ref.dtype)


def _cmp_select(slabs, kv_cmp, tq=1024, n_sub=4):
    batch, _, seq, _ = slabs.shape
    cmp_spec = lambda which: pl.BlockSpec((1, 1, 1) + kv_cmp.shape[3:], lambda b, g, i: (which, b, g, 0, 0))
    return pl.pallas_call(
        functools.partial(_cmp_select_kernel, nslc=seq // SLC_BLOCK, n_sub=n_sub),
        grid=(batch, C_KV_HEADS, seq // tq),
        in_specs=[pl.BlockSpec((1, C_GROUP, tq, LANE), lambda b, g, i: (b, g, i, 0)),
                  cmp_spec(0), cmp_spec(1)],
        out_specs=[pl.BlockSpec((1, C_GROUP, tq, LANE), lambda b, g, i: (b, g, i, 0)),
                   pl.BlockSpec((1, 1, tq, LANE), lambda b, g, i: (b, g, i, 0))],
        out_shape=[jax.ShapeDtypeStruct((batch, C_HEADS, seq, HEAD_DIM), _F32),
                   jax.ShapeDtypeStruct((batch, C_KV_HEADS, seq, LANE), _BF16)],
        compiler_params=_params("arbitrary", "arbitrary", "arbitrary"),
        name="cmp_select",
    )(slabs, kv_cmp, kv_cmp)


def _group_softmax_pv(q, k, v1, bias_s, s_s, p_s, o_s):
    nk = k.shape[0]
    tq = bias_s.shape[0]
    pieces = [slice(c, min(c + 512, nk)) for c in range(0, nk, 512)]
    s = _dot_t(q, k)
    for hh in range(C_GROUP):
        r = slice(hh * tq, (hh + 1) * tq)
        s_s[r, 0:nk] = s[r] + bias_s[:, 0:nk]
    for hh in range(C_GROUP):
        r = slice(hh * tq, (hh + 1) * tq)
        m = None
        for c in pieces:
            mc = jnp.max(s_s[r, c], axis=-1, keepdims=True)
            m = mc if m is None else jnp.maximum(m, mc)
        for c in pieces:
            p_s[r, c] = jnp.exp2(s_s[r, c] - m).astype(_BF16)
    o = _dot(p_s[:, 0:nk], v1)
    o_s[...] = o[:, 0:HEAD_DIM] / o[:, HEAD_DIM:]


def _nsa_kernel(q_ref, ks_ref, vs_ref, kw_ref, vw_ref, sel_ref, oc_ref, gl_ref, cos_ref, sin_ref, o_ref,
                ksr, vs1, kwr, vw1, expand_s, qr, bias_s, s_s, p_s, bias_w, s_w, p_w, oslc_s, owin_s):
    n_sub = qr.shape[0]
    tq = q_ref.shape[2] // n_sub
    seq = ks_ref.shape[2]
    qi = pl.program_id(2)
    chunk = 256
    shift = SLC_BLOCK.bit_length() - 1

    @pl.when(qi == 0)
    def _():
        def prep(c, _):
            r = pl.ds(pl.multiple_of(c * chunk, chunk), chunk)
            cos, sin = cos_ref[0, r, :], sin_ref[0, r, :]
            ones = jnp.ones((chunk, HEAD_DIM), _BF16)
            ksr[r, :] = _rope(ks_ref[0, 0, r, :], cos, sin).astype(_BF16)
            kwr[r, :] = _rope(kw_ref[0, 0, r, :], cos, sin).astype(_BF16)
            vs1[r, 0:HEAD_DIM] = vs_ref[0, 0, r, :].astype(_BF16)
            vs1[r, HEAD_DIM:] = ones
            vw1[r, 0:HEAD_DIM] = vw_ref[0, 0, r, :].astype(_BF16)
            vw1[r, HEAD_DIM:] = ones
            return 0

        lax.fori_loop(0, seq // chunk, prep, 0)
        blk = lax.broadcasted_iota(jnp.int32, (LANE, seq), 0)
        key = lax.broadcasted_iota(jnp.int32, (LANE, seq), 1)
        expand_s[...] = (blk == lax.shift_right_logical(key, shift)).astype(_BF16)

    base = pl.multiple_of(qi * (n_sub * tq), n_sub * tq)
    sub_rows = [slice(u * tq, (u + 1) * tq) for u in range(n_sub)]
    for u in range(n_sub):
        cos_q = cos_ref[0, pl.ds(base + u * tq, tq), :]
        sin_q = sin_ref[0, pl.ds(base + u * tq, tq), :]
        for hh in range(C_GROUP):
            q_rot = _rope(q_ref[0, hh, sub_rows[u], :], cos_q, sin_q)
            qr[u, hh * tq:(hh + 1) * tq, :] = (q_rot * (HEAD_DIM ** -0.5 * LOG2_E)).astype(_BF16)

    def window_branch(u):
        q0 = base + u * tq
        span = WIN_SIZE + tq
        k0 = pl.multiple_of(jnp.maximum(q0 - WIN_SIZE, 0), tq)
        t = q0 + lax.broadcasted_iota(jnp.int32, (tq, span), 0)
        key = k0 + lax.broadcasted_iota(jnp.int32, (tq, span), 1)
        bias_w[u] = jnp.where((key <= t) & (key > t - WIN_SIZE), 0.0, -jnp.inf)
        _group_softmax_pv(qr[u], kwr[pl.ds(k0, span), :], vw1[pl.ds(k0, span), :],
                          bias_w.at[u], s_w.at[u], p_w.at[u], owin_s.at[u])

    step = n_sub * tq
    for nt in range(1, seq // step + 1):
        @pl.when(qi + 1 == nt)
        def _(nk=nt * step):
            for u in range(n_sub):
                for c in range(0, nk, step):
                    t = base + u * tq + lax.broadcasted_iota(jnp.int32, (tq, step), 0)
                    key = c + lax.broadcasted_iota(jnp.int32, (tq, step), 1)
                    ok = (_dot(sel_ref[0, 0, sub_rows[u], :], expand_s[:, c:c + step]) > 0.5) & (key <= t)
                    bias_s[u, :, c:c + step] = jnp.where(ok, 0.0, -jnp.inf)
                _group_softmax_pv(qr[u], ksr[0:nk, :], vs1[0:nk, :],
                                  bias_s.at[u], s_s.at[u], p_s.at[u], oslc_s.at[u])
                window_branch(u)

    gates = jax.nn.sigmoid(gl_ref[0, 0])
    for u in range(n_sub):
        for hh in range(C_GROUP):
            r = slice(hh * tq, (hh + 1) * tq)
            g_cmp, g_slc, g_win = (gates[sub_rows[u], 3 * hh + c:3 * hh + c + 1] for c in range(3))
            o = g_cmp * oc_ref[0, hh, sub_rows[u], :] + g_slc * oslc_s[u, r, :] + g_win * owin_s[u, r, :]
            o_ref[0, sub_rows[u], hh * HEAD_DIM:(hh + 1) * HEAD_DIM] = o.astype(o_ref.dtype)


def _nsa_attention(slabs, sel, o_cmp, cos, sin, tq=128, n_sub=2):
    batch, _, seq, _ = slabs.shape
    kv = lambda off: pl.BlockSpec((1, 1, seq, LANE), lambda b, g, i: (b, off + g, 0, 0))
    table = pl.BlockSpec((1, seq, LANE), lambda b, g, i: (b, 0, 0))
    rows = C_GROUP * tq
    span = WIN_SIZE + tq
    tile = n_sub * tq
    return pl.pallas_call(
        _nsa_kernel,
        grid=(batch, C_KV_HEADS, seq // tile),
        in_specs=[pl.BlockSpec((1, C_GROUP, tile, LANE), lambda b, g, i: (b, g, i, 0)),
                  kv(16), kv(18), kv(20), kv(22),
                  pl.BlockSpec((1, 1, tile, LANE), lambda b, g, i: (b, g, i, 0)),
                  pl.BlockSpec((1, C_GROUP, tile, LANE), lambda b, g, i: (b, g, i, 0)),
                  pl.BlockSpec((1, 1, tile, LANE), lambda b, g, i: (b, 24 + g, i, 0)),
                  table, table],
        out_specs=pl.BlockSpec((1, tile, C_GROUP * HEAD_DIM), lambda b, g, i: (b, i, g)),
        out_shape=jax.ShapeDtypeStruct((batch, seq, C_HEADS * HEAD_DIM), _BF16),
        scratch_shapes=[pltpu.VMEM((seq, HEAD_DIM), _BF16), pltpu.VMEM((seq, 2 * HEAD_DIM), _BF16)] * 2
        + [pltpu.VMEM((LANE, seq), _BF16), pltpu.VMEM((n_sub, rows, HEAD_DIM), _BF16),
           pltpu.VMEM((n_sub, tq, seq), _F32), pltpu.VMEM((n_sub, rows, seq), _F32),
           pltpu.VMEM((n_sub, rows, seq), _BF16),
           pltpu.VMEM((n_sub, tq, span), _F32), pltpu.VMEM((n_sub, rows, span), _F32),
           pltpu.VMEM((n_sub, rows, span), _BF16)]
        + [pltpu.VMEM((n_sub, rows, HEAD_DIM), _F32)] * 2,
        compiler_params=_params("arbitrary", "arbitrary", "arbitrary"),
        name="nsa_attention",
    )(slabs, slabs, slabs, slabs, slabs, sel, o_cmp, slabs, cos, sin)


def _conv_kernel(u_ref, c_ref, b_ref, w_ref, o_ref):
    seq = u_ref.shape[2]
    row = lax.broadcasted_iota(jnp.int32, (seq, LANE), 0)
    u = c_ref[0, 0] * u_ref[0, 0]
    taps = w_ref.shape[0]
    conv = _shift_rows(u, taps - 1, row) * w_ref[0:1, :]
    for j in range(1, taps):
        shifted = u if j == taps - 1 else _shift_rows(u, taps - 1 - j, row)
        conv = conv + shifted * w_ref[j:j + 1, :]
    o_ref[0] = (b_ref[0, 0] * conv).astype(o_ref.dtype)


def _short_conv(slabs, conv_w, first_slab):
    batch, _, seq, _ = slabs.shape
    nd = D_WIDTH // LANE
    part = lambda off: pl.BlockSpec((1, 1, seq, LANE), lambda b, j: (b, first_slab + off + j, 0, 0))
    return pl.pallas_call(
        _conv_kernel,
        grid=(batch, nd),
        in_specs=[part(0), part(nd), part(2 * nd),
                  pl.BlockSpec((conv_w.shape[0], LANE), lambda b, j: (0, j))],
        out_specs=pl.BlockSpec((1, seq, LANE), lambda b, j: (b, 0, j)),
        out_shape=jax.ShapeDtypeStruct((batch, seq, D_WIDTH), _BF16),
        compiler_params=_params("arbitrary", "arbitrary"),
        name="short_conv",
    )(slabs, slabs, slabs, conv_w)


def _pad_cols(w, total):
    return jnp.pad(w, ((0, 0), (0, total - w.shape[1])))


def _odd_in_weight(w):
    qkv = C_HEADS * HEAD_DIM + 6 * C_KV_HEADS * HEAD_DIM
    ngate = 3 * C_GROUP
    gates = [_pad_cols(w[:, qkv + g * ngate:qkv + (g + 1) * ngate], LANE) for g in range(C_KV_HEADS)]
    rest = w[:, qkv + C_KV_HEADS * ngate:]
    return _pad_cols(jnp.concatenate([w[:, :qkv]] + gates + [rest], axis=1), PROJ_SLABS * LANE)


def kernel(x, positions, norm_w, ffn_w_gate, ffn_w_up, ffn_w_down, ev_w_in, ev_w_out, pool_w, pool_scale,
           od_w_in, od_w_out, cmp_pe_k, cmp_w1_k, cmp_w2_k, cmp_pe_v, cmp_w1_v, cmp_w2_v, conv_w):
    batch, seq, d = x.shape
    depth = norm_w.shape[0]
    bf = lambda a: a.astype(_BF16)

    inv_freq = 1.0 / (ROPE_THETA ** (jnp.arange(0, HEAD_DIM, 2, dtype=_F32) / HEAD_DIM))
    ang = positions.astype(_F32)[..., None] * inv_freq
    cos = jnp.concatenate([jnp.cos(ang), jnp.cos(ang)], axis=-1)
    sin = jnp.concatenate([-jnp.sin(ang), jnp.sin(ang)], axis=-1)

    nw = norm_w.reshape(depth, 6, 1, d)
    h = x.reshape(batch * seq, d)

    w_gate, w_up, w_down = bf(ffn_w_gate), bf(ffn_w_up), bf(ffn_w_down)
    w_ev_in, w_ev_out = bf(ev_w_in), bf(ev_w_out)
    w_od_in, w_od_out = bf(jax.vmap(_odd_in_weight)(od_w_in)), bf(od_w_out)

    def ffn(h, layer, which):
        a = _ffn_up(h, nw[layer, 4 * which], w_gate, w_up, (layer, which))
        return _ffn_down(a, w_down, (layer, which), h, nw[layer, 4 * which + 1], HALF_STEP)

    for layer in range(depth):
        i = layer // 2
        h = ffn(h, layer, 0)
        if layer % 2 == 0:
            slabs = _norm_proj(h, nw[layer, 2], w_ev_in, (i,), batch, seq, rope=(cos, sin),
                               n_rope=2 * A_HEADS, n_scaled=A_HEADS, q_scale=HEAD_DIM ** -0.5 * LOG2_E)
            o_a = _dilated_attention(slabs)
            o_b = _multiscale_pool(slabs, bf(pool_w[i]), pool_scale[i].reshape(1, -1))
            h = _mix_out(o_a.reshape(batch * seq, -1), o_b.reshape(batch * seq, -1), w_ev_out, (i,), h,
                         nw[layer, 3])
        else:
            slabs = _norm_proj(h, nw[layer, 2], w_od_in, (i,), batch, seq)
            kv_cmp = _compress(slabs, jnp.stack([cmp_pe_k[i], cmp_pe_v[i]]),
                               bf(jnp.stack([cmp_w1_k[i], cmp_w1_v[i]])),
                               bf(jnp.stack([cmp_w2_k[i], cmp_w2_v[i]])), C_HEADS)
            o_cmp, sel = _cmp_select(slabs, kv_cmp)
            o_c = _nsa_attention(slabs, sel, o_cmp, cos, sin)
            y_d = _short_conv(slabs, conv_w[i], 26)
            h = _mix_out(o_c.reshape(batch * seq, -1), y_d.reshape(batch * seq, -1), w_od_out, (i,), h,
                         nw[layer, 3])
        h = ffn(h, layer, 1)
    return h.reshape(batch, seq, d)
```

```python
import functools

import jax
import jax.numpy as jnp
from jax import lax
from jax.experimental import pallas as pl
from jax.experimental.pallas import tpu as pltpu

HEAD_DIM = 128
ROPE_THETA = 10000.0
NORM_EPS = 1e-6
LOG2_E = 1.4426950408889634
HALF_STEP = 0.5
LANE = 128

A_HEADS = 12
A_DILATIONS = (1, 4, 16)
A_BACK = 128
B_WINDOWS = (2, 4, 8, 16)
B_GROUPS = 4

C_HEADS = 12
C_KV_HEADS = 2
C_GROUP = C_HEADS // C_KV_HEADS
CMP_BLOCK = 32
CMP_STRIDE = 16
SLC_BLOCK = 64
SLC_TOPN = 8
WIN_SIZE = 512
FORCED_SCORE = 1e9
D_WIDTH = 512
PROJ_SLABS = 40
NORM_SUB_TILES = 4

VMEM_LIMIT = 56 * 1024 * 1024

_BF16 = jnp.bfloat16
_F32 = jnp.float32


def _params(*sem):
    return pltpu.CompilerParams(dimension_semantics=sem, vmem_limit_bytes=VMEM_LIMIT)


def _rms(x, w):
    return x * lax.rsqrt(jnp.mean(x * x, axis=-1, keepdims=True) + NORM_EPS) * w


def _dot(a, b):
    return jnp.dot(a, b, preferred_element_type=_F32)


def _dot_t(a, b):
    return lax.dot_general(a, b, (((1,), (1,)), ((), ())), preferred_element_type=_F32)


def _rope(x, cos, sin):
    return x * cos + pltpu.roll(x, HEAD_DIM // 2, 1) * sin


def _shift_rows(x, k, row):
    return jnp.where(row >= k, pltpu.roll(x, k, 0), 0.0)


def _proj_kernel(h_ref, nw_ref, w_ref, *rest, n_rope, n_scaled, q_scale):
    if n_rope:
        cos_ref, sin_ref, o_ref, xn_ref = rest
    else:
        o_ref, xn_ref = rest
    per_step = o_ref.shape[1]
    tn = per_step * LANE

    def use(xn, r, step):
        if step is None:
            cols = pl.ds(pl.multiple_of(pl.program_id(1) * tn, tn), tn)
        else:
            cols = slice(step * tn, (step + 1) * tn)
        acc = _dot(xn, w_ref[:, cols])
        for s in range(per_step):
            x = acc[:, s * LANE:(s + 1) * LANE]
            slab = None if step is None else step * per_step + s
            if slab is not None and slab < n_rope:
                x = _rope(x, cos_ref[0, r, :], sin_ref[0, r, :])
                if slab < n_scaled:
                    x = x * q_scale
            o_ref[0, s, r, :] = x

    steps = w_ref.shape[1] // tn if n_rope else None
    _normed_rows(h_ref, nw_ref, xn_ref, use, static_steps=steps)


def _stacked(block, index_map, lead, **kwargs):
    return pl.BlockSpec((None,) * len(lead) + block, lambda *g: tuple(lead) + index_map(*g), **kwargs)


def _norm_proj(h, nw, w, lead, batch, seq, rope=None, n_rope=0, n_scaled=0, q_scale=1.0, tm=1024, tn=1024):
    t, d = h.shape
    n = w.shape[-1]
    per_b = seq // tm
    table = pl.BlockSpec((1, tm, LANE), lambda i, j: (i // per_b, i % per_b, 0))
    return pl.pallas_call(
        functools.partial(_proj_kernel, n_rope=n_rope, n_scaled=n_scaled, q_scale=q_scale),
        grid=(t // tm, n // tn),
        in_specs=[pl.BlockSpec((tm, d), lambda i, j: (i, 0)),
                  pl.BlockSpec((1, d), lambda i, j: (0, 0)),
                  _stacked((d, n), lambda i, j: (0, 0), lead, pipeline_mode=pl.Buffered(1))]
        + ([table, table] if n_rope else []),
        out_specs=pl.BlockSpec((1, tn // LANE, tm, LANE), lambda i, j: (i // per_b, j, i % per_b, 0)),
        out_shape=jax.ShapeDtypeStruct((batch, n // LANE, seq, LANE), _F32),
        scratch_shapes=[pltpu.VMEM((tm, d), _BF16)],
        compiler_params=_params("arbitrary", "arbitrary"),
        name="norm_proj",
    )(h, nw, w, *(rope if n_rope else ()))


def _normed_rows(h_ref, nw_ref, xn_ref, use, static_steps=None):
    j = pl.program_id(1)

    def first(step):
        sub = h_ref.shape[0] // NORM_SUB_TILES
        for s in range(NORM_SUB_TILES):
            r = slice(s * sub, (s + 1) * sub)
            xn = _rms(h_ref[r, :], nw_ref[...]).astype(_BF16)
            xn_ref[r, :] = xn
            use(xn, r, step)

    if static_steps is None:
        pl.when(j == 0)(lambda: first(None))
        pl.when(j > 0)(lambda: use(xn_ref[...], slice(None), None))
    else:
        pl.when(j == 0)(lambda: first(0))
        for step in range(1, static_steps):
            pl.when(j == step)(lambda step=step: use(xn_ref[...], slice(None), step))


def _ffn_up_kernel(h_ref, nw_ref, wg_ref, wu_ref, o_ref, xn_ref):
    def use(xn, r, step):
        g = _dot(xn, wg_ref[...])
        u = _dot(xn, wu_ref[...])
        o_ref[r, :] = (jax.nn.silu(g) * u).astype(_BF16)

    _normed_rows(h_ref, nw_ref, xn_ref, use)


def _ffn_up(h, nw, wg, wu, lead, tm=1024, tn=512):
    t, d = h.shape
    f = wg.shape[-1]
    return pl.pallas_call(
        _ffn_up_kernel,
        grid=(t // tm, f // tn),
        in_specs=[pl.BlockSpec((tm, d), lambda i, j: (i, 0)),
                  pl.BlockSpec((1, d), lambda i, j: (0, 0)),
                  _stacked((d, tn), lambda i, j: (0, j), lead),
                  _stacked((d, tn), lambda i, j: (0, j), lead)],
        out_specs=pl.BlockSpec((tm, tn), lambda i, j: (i, j)),
        out_shape=jax.ShapeDtypeStruct((t, f), _BF16),
        scratch_shapes=[pltpu.VMEM((tm, d), _BF16)],
        compiler_params=_params("arbitrary", "arbitrary"),
        name="ffn_up",
    )(h, nw, wg, wu)


def _ffn_down_kernel(a_ref, w_ref, h_ref, nw_ref, o_ref, *, scale, n_sub):
    sub = a_ref.shape[0] // n_sub
    for s in range(n_sub):
        r = slice(s * sub, (s + 1) * sub)
        f = _dot(a_ref[r, :], w_ref[...])
        o_ref[r, :] = h_ref[r, :] + scale * _rms(f, nw_ref[...])


def _ffn_down(a, w, lead, h, nw, scale, tm=512, n_sub=2):
    t, f = a.shape
    d = w.shape[-1]
    w_spec = _stacked((f, d), lambda i: (0, 0), lead, pipeline_mode=pl.Buffered(1))
    return pl.pallas_call(
        functools.partial(_ffn_down_kernel, scale=scale, n_sub=n_sub),
        grid=(t // tm,),
        in_specs=[pl.BlockSpec((tm, f), lambda i: (i, 0)),
                  w_spec,
                  pl.BlockSpec((tm, d), lambda i: (i, 0)),
                  pl.BlockSpec((1, d), lambda i: (0, 0))],
        out_specs=pl.BlockSpec((tm, d), lambda i: (i, 0)),
        out_shape=jax.ShapeDtypeStruct((t, d), _F32),
        compiler_params=_params("arbitrary"),
        name="ffn_down",
    )(a, w, h, nw)


def _mix_out_kernel(x1_ref, x2_ref, w_ref, h_ref, nw_ref, o_ref):
    c1 = x1_ref.shape[1]
    sub = x1_ref.shape[0] // 2
    for s in range(2):
        r = slice(s * sub, (s + 1) * sub)
        m = _dot(x1_ref[r, :], w_ref[0:c1, :]) + _dot(x2_ref[r, :], w_ref[c1:, :])
        o_ref[r, :] = h_ref[r, :] + _rms(m, nw_ref[...])


def _mix_out(x1, x2, w, lead, h, nw, tm=512):
    t, c1 = x1.shape
    c2 = x2.shape[1]
    d = w.shape[-1]
    return pl.pallas_call(
        _mix_out_kernel,
        grid=(t // tm,),
        in_specs=[pl.BlockSpec((tm, c1), lambda i: (i, 0)),
                  pl.BlockSpec((tm, c2), lambda i: (i, 0)),
                  _stacked((c1 + c2, d), lambda i: (0, 0), lead),
                  pl.BlockSpec((tm, d), lambda i: (i, 0)),
                  pl.BlockSpec((1, d), lambda i: (0, 0))],
        out_specs=pl.BlockSpec((tm, d), lambda i: (i, 0)),
        out_shape=jax.ShapeDtypeStruct((t, d), _F32),
        compiler_params=_params("arbitrary"),
        name="mix_out",
    )(x1, x2, w, h, nw)


def _rows(start, size, stride):
    return pl.ds(start, size) if stride == 1 else pl.ds(start, size, stride=stride)


def _dilated_kernel(q_ref, k_ref, v_ref, o_ref, s_s, p_s, pv_s, m_s, l_s):
    seq = q_ref.shape[2]
    blk = A_BACK
    chunk = 256

    qi = lax.broadcasted_iota(jnp.int32, (blk, blk), 0)
    kj = lax.broadcasted_iota(jnp.int32, (blk, blk), 1)
    cur_ok = kj <= qi
    both_ok = jnp.concatenate([kj >= qi, cur_ok], axis=1)
    ones = jnp.ones((blk, HEAD_DIM), _BF16)

    blocks = []
    for g, dil in enumerate(A_DILATIONS):
        for res in range(dil):
            prev = None
            for n in range(seq // (blk * dil)):
                rows = _rows(res + n * blk * dil, blk, dil)
                blocks.append((g, rows, prev))
                prev = rows

    def keys_bf16(rows):
        return k_ref[0, 0, rows, :].astype(_BF16)

    def values1_bf16(rows):
        return jnp.concatenate([v_ref[0, 0, rows, :].astype(_BF16), ones], axis=1)

    for b, (g, rows, prev) in enumerate(blocks):
        qb = q_ref[0, 0, rows, :].astype(_BF16)
        if prev is None:
            s_s[b, :, 0:blk] = jnp.where(cur_ok, _dot_t(qb, keys_bf16(rows)), -jnp.inf)
        else:
            k_all = jnp.concatenate([keys_bf16(prev), keys_bf16(rows)], axis=0)
            s_s[b] = jnp.where(both_ok, _dot_t(qb, k_all), -jnp.inf)

    for b, (g, rows, prev) in enumerate(blocks):
        width = blk if prev is None else 2 * blk
        s = s_s[b, :, 0:width]
        m = jnp.max(s, axis=-1, keepdims=True)
        p_s[b, :, 0:width] = jnp.exp2(s - m).astype(_BF16)
        m_s[g, rows, :] = jnp.broadcast_to(m, (blk, HEAD_DIM))

    for b, (g, rows, prev) in enumerate(blocks):
        if prev is None:
            pva = _dot(p_s[b, :, 0:blk], values1_bf16(rows))
        else:
            pva = _dot(p_s[b], jnp.concatenate([values1_bf16(prev), values1_bf16(rows)], axis=0))
        pv_s[g, rows, :] = pva[:, 0:HEAD_DIM]
        l_s[g, rows, :] = pva[:, HEAD_DIM:]

    for c in range(0, seq, chunk):
        r = slice(c, c + chunk)
        m_all = jnp.maximum(jnp.maximum(m_s[0, r, :], m_s[1, r, :]), m_s[2, r, :])
        num = jnp.zeros((chunk, HEAD_DIM), _F32)
        den = jnp.zeros((chunk, HEAD_DIM), _F32)
        for g in range(len(A_DILATIONS)):
            w = jnp.exp2(m_s[g, r, :] - m_all)
            num = num + w * pv_s[g, r, :]
            den = den + w * l_s[g, r, :]
        o_ref[0, r, :] = (num / den).astype(o_ref.dtype)


def _dilated_attention(slabs):
    batch, _, seq, _ = slabs.shape
    head = lambda off: pl.BlockSpec((1, 1, seq, LANE), lambda b, h: (b, off + h, 0, 0))
    nblocks = len(A_DILATIONS) * seq // A_BACK
    return pl.pallas_call(
        _dilated_kernel,
        grid=(batch, A_HEADS),
        in_specs=[head(0), head(A_HEADS), head(2 * A_HEADS)],
        out_specs=pl.BlockSpec((1, seq, LANE), lambda b, h: (b, 0, h)),
        out_shape=jax.ShapeDtypeStruct((batch, seq, A_HEADS * HEAD_DIM), _BF16),
        scratch_shapes=[pltpu.VMEM((nblocks, A_BACK, 2 * A_BACK), _F32),
                        pltpu.VMEM((nblocks, A_BACK, 2 * A_BACK), _BF16),
                        pltpu.VMEM((3, seq, HEAD_DIM), _F32), pltpu.VMEM((3, seq, HEAD_DIM), _F32),
                        pltpu.VMEM((3, seq, HEAD_DIM), _F32)],
        compiler_params=_params("arbitrary", "arbitrary"),
        name="dilated_attention",
    )(slabs, slabs, slabs)


def _pool_kernel(u_ref, w_ref, sc_ref, o_ref):
    seq = u_ref.shape[2]
    row = lax.broadcasted_iota(jnp.int32, (seq, LANE), 0)
    for g, win in enumerate(B_WINDOWS):
        x = u_ref[0, g]
        s = x
        step = 1
        while step < win:
            s = s + _shift_rows(s, step, row)
            step *= 2
        cnt = jnp.minimum(row + 1, win).astype(_F32)
        pooled = s / cnt - x
        mixed = _dot(pooled.astype(_BF16), w_ref[g]) * sc_ref[:, g * LANE:(g + 1) * LANE]
        o_ref[0, :, g * LANE:(g + 1) * LANE] = mixed.astype(o_ref.dtype)


def _multiscale_pool(slabs, pool_w, pool_scale):
    batch, _, seq, _ = slabs.shape
    first = 3 * A_HEADS // B_GROUPS
    return pl.pallas_call(
        _pool_kernel,
        grid=(batch,),
        in_specs=[pl.BlockSpec((1, B_GROUPS, seq, LANE), lambda b: (b, first, 0, 0)),
                  pl.BlockSpec((B_GROUPS, LANE, LANE), lambda b: (0, 0, 0)),
                  pl.BlockSpec((1, B_GROUPS * LANE), lambda b: (0, 0))],
        out_specs=pl.BlockSpec((1, seq, B_GROUPS * LANE), lambda b: (b, 0, 0)),
        out_shape=jax.ShapeDtypeStruct((batch, seq, B_GROUPS * LANE), _BF16),
        compiler_params=_params("arbitrary"),
        name="multiscale_pool",
    )(slabs, pool_w, pool_scale)


def _compress_kernel(x_ref, pe_ref, w1_ref, w2_ref, o_ref):
    nrow = o_ref.shape[3]
    hidden = w1_ref.shape[2]
    a = jnp.zeros((nrow, hidden), _F32)
    b = jnp.zeros((nrow, hidden), _F32)
    for tok in range(CMP_STRIDE):
        x = x_ref[0, 0, pl.ds(tok, nrow, stride=CMP_STRIDE), :]
        lo, hi = tok, CMP_STRIDE + tok
        a = a + _dot((x + pe_ref[0, lo:lo + 1, :]).astype(_BF16), w1_ref[0, lo * HEAD_DIM:(lo + 1) * HEAD_DIM, :])
        b = b + _dot((x + pe_ref[0, hi:hi + 1, :]).astype(_BF16), w1_ref[0, hi * HEAD_DIM:(hi + 1) * HEAD_DIM, :])
    hid = a + pltpu.roll(b, nrow - 1, 0)
    o_ref[0, 0, 0] = _dot(jax.nn.gelu(hid).astype(_BF16), w2_ref[0])


def _compress(slabs, pe, w1, w2, first_slab):
    batch, _, seq, _ = slabs.shape
    nrow = seq // CMP_STRIDE
    return pl.pallas_call(
        _compress_kernel,
        grid=(2, batch, C_KV_HEADS),
        in_specs=[pl.BlockSpec((1, 1, seq, LANE), lambda kv, b, g: (b, first_slab + 2 * kv + g, 0, 0)),
                  pl.BlockSpec((1, CMP_BLOCK, HEAD_DIM), lambda kv, b, g: (kv, 0, 0)),
                  pl.BlockSpec((1, CMP_BLOCK * HEAD_DIM, w1.shape[2]), lambda kv, b, g: (kv, 0, 0)),
                  pl.BlockSpec((1, w2.shape[1], HEAD_DIM), lambda kv, b, g: (kv, 0, 0))],
        out_specs=pl.BlockSpec((1, 1, 1, nrow, HEAD_DIM), lambda kv, b, g: (kv, b, g, 0, 0)),
        out_shape=jax.ShapeDtypeStruct((2, batch, C_KV_HEADS, nrow, HEAD_DIM), _F32),
        compiler_params=_params("arbitrary", "arbitrary", "arbitrary"),
        name="compress",
    )(slabs, pe, w1, w2)


def _cmp_select_kernel(q_ref, kc_ref, vc_ref, o_ref, sel_ref, *, nslc, n_sub):
    tq = q_ref.shape[2] // n_sub
    ncmp = kc_ref.shape[3]
    scale = HEAD_DIM ** -0.5
    n = lax.broadcasted_iota(jnp.int32, (tq, ncmp), 1)

    cj = lax.broadcasted_iota(jnp.int32, (LANE, ncmp), 0) * SLC_BLOCK
    cn = lax.broadcasted_iota(jnp.int32, (LANE, ncmp), 1) * CMP_STRIDE
    cover_t = ((cn < cj + SLC_BLOCK) & (cn + CMP_BLOCK > cj)).astype(_BF16)
    j = lax.broadcasted_iota(jnp.int32, (nslc, tq), 0)
    jf = j.astype(_F32)

    kc = kc_ref[0, 0, 0].astype(_BF16)
    vc = vc_ref[0, 0, 0].astype(_BF16)
    for u in range(n_sub):
        rows = slice(u * tq, (u + 1) * tq)
        t0 = (pl.program_id(2) * n_sub + u) * tq
        t = t0 + lax.broadcasted_iota(jnp.int32, (tq, ncmp), 0)
        cmask = n * CMP_STRIDE + (CMP_BLOCK - 1) <= t
        imp_t = jnp.zeros((LANE, tq), _F32)
        for hh in range(C_GROUP):
            sc = jnp.where(cmask, _dot_t(q_ref[0, hh, rows, :].astype(_BF16), kc) * scale, -jnp.inf)
            m = jnp.max(sc, axis=-1, keepdims=True)
            m = jnp.where(jnp.isfinite(m), m, 0.0)
            p = jnp.exp(sc - m)
            den = jnp.sum(p, axis=-1, keepdims=True)
            p = (p / jnp.maximum(den, 1.0)).astype(_BF16)
            o_ref[0, hh, rows, :] = _dot(p, vc)
            imp_t = imp_t + _dot_t(cover_t, p)

        tt = t0 + lax.broadcasted_iota(jnp.int32, (nslc, tq), 1)
        cur = lax.shift_right_logical(tt, SLC_BLOCK.bit_length() - 1)
        visible = j <= cur
        forced = visible & ((j == 0) | (j >= cur - 1))
        score = jnp.where(forced, FORCED_SCORE, jnp.where(visible, imp_t[0:nslc, :], -FORCED_SCORE))
        sel_t = jnp.zeros((nslc, tq), _F32)
        for _ in range(min(SLC_TOPN, nslc)):
            best = jnp.max(score, axis=0, keepdims=True)
            pick = jnp.min(jnp.where(score == best, jf, float(nslc)), axis=0, keepdims=True)
            hit = jf == pick
            sel_t = jnp.where(hit, 1.0, sel_t)
            score = jnp.where(hit, -jnp.inf, score)
        sel_t = jnp.concatenate([sel_t, jnp.zeros((LANE - nslc, tq), _F32)], axis=0)
        sel_ref[0, 0, rows, :] = sel_t.T.astype(sel_ref.dtype)


def _cmp_select(slabs, kv_cmp, tq=1024, n_sub=4):
    batch, _, seq, _ = slabs.shape
    cmp_spec = lambda which: pl.BlockSpec((1, 1, 1) + kv_cmp.shape[3:], lambda b, g, i: (which, b, g, 0, 0))
    return pl.pallas_call(
        functools.partial(_cmp_select_kernel, nslc=seq // SLC_BLOCK, n_sub=n_sub),
        grid=(batch, C_KV_HEADS, seq // tq),
        in_specs=[pl.BlockSpec((1, C_GROUP, tq, LANE), lambda b, g, i: (b, g, i, 0)),
                  cmp_spec(0), cmp_spec(1)],
        out_specs=[pl.BlockSpec((1, C_GROUP, tq, LANE), lambda b, g, i: (b, g, i, 0)),
                   pl.BlockSpec((1, 1, tq, LANE), lambda b, g, i: (b, g, i, 0))],
        out_shape=[jax.ShapeDtypeStruct((batch, C_HEADS, seq, HEAD_DIM), _F32),
                   jax.ShapeDtypeStruct((batch, C_KV_HEADS, seq, LANE), _BF16)],
        compiler_params=_params("arbitrary", "arbitrary", "arbitrary"),
        name="cmp_select",
    )(slabs, kv_cmp, kv_cmp)


def _group_softmax_pv(q, k, v1, bias_s, s_s, p_s, o_s):
    nk = k.shape[0]
    tq = bias_s.shape[0]
    pieces = [slice(c, min(c + 512, nk)) for c in range(0, nk, 512)]
    s = _dot_t(q, k)
    for hh in range(C_GROUP):
        r = slice(hh * tq, (hh + 1) * tq)
        s_s[r, 0:nk] = s[r] + bias_s[:, 0:nk]
    for hh in range(C_GROUP):
        r = slice(hh * tq, (hh + 1) * tq)
        m = None
        for c in pieces:
            mc = jnp.max(s_s[r, c], axis=-1, keepdims=True)
            m = mc if m is None else jnp.maximum(m, mc)
        for c in pieces:
            p_s[r, c] = jnp.exp2(s_s[r, c] - m).astype(_BF16)
    o = _dot(p_s[:, 0:nk], v1)
    o_s[...] = o[:, 0:HEAD_DIM] / o[:, HEAD_DIM:]


def _nsa_prep_kernel(ks_ref, vs_ref, kw_ref, vw_ref, cos_ref, sin_ref, ksr_ref, vs1_ref, kwr_ref, vw1_ref):
    seq = ks_ref.shape[2]
    chunk = 256

    def prep(c, carry):
        r = pl.ds(pl.multiple_of(c * chunk, chunk), chunk)
        cos, sin = cos_ref[0, r, :], sin_ref[0, r, :]
        ones = jnp.ones((chunk, HEAD_DIM), _BF16)
        ksr_ref[0, 0, r, :] = _rope(ks_ref[0, 0, r, :], cos, sin).astype(_BF16)
        kwr_ref[0, 0, r, :] = _rope(kw_ref[0, 0, r, :], cos, sin).astype(_BF16)
        vs1_ref[0, 0, r, 0:HEAD_DIM] = vs_ref[0, 0, r, :].astype(_BF16)
        vs1_ref[0, 0, r, HEAD_DIM:] = ones
        vw1_ref[0, 0, r, 0:HEAD_DIM] = vw_ref[0, 0, r, :].astype(_BF16)
        vw1_ref[0, 0, r, HEAD_DIM:] = ones
        return carry

    lax.fori_loop(0, seq // chunk, prep, 0)


def _nsa_prep(slabs, cos, sin):
    batch, _, seq, _ = slabs.shape
    kv = lambda off: pl.BlockSpec((1, 1, seq, LANE), lambda b, g: (b, off + g, 0, 0))
    table = pl.BlockSpec((1, seq, LANE), lambda b, g: (b, 0, 0))
    out = lambda width: pl.BlockSpec((1, 1, seq, width), lambda b, g: (b, g, 0, 0))
    shape = lambda width: jax.ShapeDtypeStruct((batch, C_KV_HEADS, seq, width), _BF16)
    return pl.pallas_call(
        _nsa_prep_kernel,
        grid=(batch, C_KV_HEADS),
        in_specs=[kv(16), kv(18), kv(20), kv(22), table, table],
        out_specs=[out(HEAD_DIM), out(2 * HEAD_DIM), out(HEAD_DIM), out(2 * HEAD_DIM)],
        out_shape=[shape(HEAD_DIM), shape(2 * HEAD_DIM), shape(HEAD_DIM), shape(2 * HEAD_DIM)],
        compiler_params=_params("arbitrary", "arbitrary"),
        name="nsa_prep",
    )(slabs, slabs, slabs, slabs, cos, sin)


def _nsa_kernel(q_ref, ksr_ref, vs1_ref, kwr_ref, vw1_ref, sel_ref, oc_ref, gl_ref, cos_ref, sin_ref, o_ref,
                expand_s, qr, bias_s, s_s, p_s, bias_w, s_w, p_w, oslc_s, owin_s):
    n_sub = qr.shape[0]
    tq = q_ref.shape[2] // n_sub
    seq = ksr_ref.shape[2]
    qi = pl.program_id(0)
    shift = SLC_BLOCK.bit_length() - 1
    ksr, vs1, kwr, vw1 = (ref.at[0, 0] for ref in (ksr_ref, vs1_ref, kwr_ref, vw1_ref))

    @pl.when((qi == 0) & (pl.program_id(1) == 0) & (pl.program_id(2) == 0))
    def _():
        blk = lax.broadcasted_iota(jnp.int32, (LANE, seq), 0)
        key = lax.broadcasted_iota(jnp.int32, (LANE, seq), 1)
        expand_s[...] = (blk == lax.shift_right_logical(key, shift)).astype(_BF16)

    base = pl.multiple_of(qi * (n_sub * tq), n_sub * tq)
    sub_rows = [slice(u * tq, (u + 1) * tq) for u in range(n_sub)]
    for u in range(n_sub):
        cos_q = cos_ref[0, sub_rows[u], :]
        sin_q = sin_ref[0, sub_rows[u], :]
        for hh in range(C_GROUP):
            q_rot = _rope(q_ref[0, hh, sub_rows[u], :], cos_q, sin_q)
            qr[u, hh * tq:(hh + 1) * tq, :] = (q_rot * (HEAD_DIM ** -0.5 * LOG2_E)).astype(_BF16)

    def window_branch(u):
        q0 = base + u * tq
        span = WIN_SIZE + tq
        k0 = pl.multiple_of(jnp.maximum(q0 - WIN_SIZE, 0), tq)
        t = q0 + lax.broadcasted_iota(jnp.int32, (tq, span), 0)
        key = k0 + lax.broadcasted_iota(jnp.int32, (tq, span), 1)
        bias_w[u] = jnp.where((key <= t) & (key > t - WIN_SIZE), 0.0, -jnp.inf)
        _group_softmax_pv(qr[u], kwr[pl.ds(k0, span), :], vw1[pl.ds(k0, span), :],
                          bias_w.at[u], s_w.at[u], p_w.at[u], owin_s.at[u])

    step = n_sub * tq
    for nt in range(1, seq // step + 1):
        @pl.when(qi + 1 == nt)
        def _(nk=nt * step):
            for u in range(n_sub):
                for c in range(0, nk, step):
                    t = base + u * tq + lax.broadcasted_iota(jnp.int32, (tq, step), 0)
                    key = c + lax.broadcasted_iota(jnp.int32, (tq, step), 1)
                    ok = (_dot(sel_ref[0, 0, sub_rows[u], :], expand_s[:, c:c + step]) > 0.5) & (key <= t)
                    bias_s[u, :, c:c + step] = jnp.where(ok, 0.0, -jnp.inf)
                _group_softmax_pv(qr[u], ksr[0:nk, :], vs1[0:nk, :],
                                  bias_s.at[u], s_s.at[u], p_s.at[u], oslc_s.at[u])
                window_branch(u)

    gates = jax.nn.sigmoid(gl_ref[0, 0])
    for u in range(n_sub):
        for hh in range(C_GROUP):
            r = slice(hh * tq, (hh + 1) * tq)
            g_cmp, g_slc, g_win = (gates[sub_rows[u], 3 * hh + c:3 * hh + c + 1] for c in range(3))
            o = g_cmp * oc_ref[0, hh, sub_rows[u], :] + g_slc * oslc_s[u, r, :] + g_win * owin_s[u, r, :]
            o_ref[0, sub_rows[u], hh * HEAD_DIM:(hh + 1) * HEAD_DIM] = o.astype(o_ref.dtype)


def _nsa_attention(slabs, prepped, sel, o_cmp, cos, sin, tq=128, n_sub=2):
    batch, _, seq, _ = slabs.shape
    rows = C_GROUP * tq
    span = WIN_SIZE + tq
    tile = n_sub * tq
    heads = pl.BlockSpec((1, C_GROUP, tile, LANE), lambda i, b, g: (b, g, i, 0))
    kv = lambda width: pl.BlockSpec((1, 1, seq, width), lambda i, b, g: (b, g, 0, 0))
    table = pl.BlockSpec((1, tile, LANE), lambda i, b, g: (b, i, 0))
    return pl.pallas_call(
        _nsa_kernel,
        grid=(seq // tile, batch, C_KV_HEADS),
        in_specs=[heads, kv(HEAD_DIM), kv(2 * HEAD_DIM), kv(HEAD_DIM), kv(2 * HEAD_DIM),
                  pl.BlockSpec((1, 1, tile, LANE), lambda i, b, g: (b, g, i, 0)),
                  heads,
                  pl.BlockSpec((1, 1, tile, LANE), lambda i, b, g: (b, 24 + g, i, 0)),
                  table, table],
        out_specs=pl.BlockSpec((1, tile, C_GROUP * HEAD_DIM), lambda i, b, g: (b, i, g)),
        out_shape=jax.ShapeDtypeStruct((batch, seq, C_HEADS * HEAD_DIM), _BF16),
        scratch_shapes=[pltpu.VMEM((LANE, seq), _BF16), pltpu.VMEM((n_sub, rows, HEAD_DIM), _BF16),
                        pltpu.VMEM((n_sub, tq, seq), _F32), pltpu.VMEM((n_sub, rows, seq), _F32),
                        pltpu.VMEM((n_sub, rows, seq), _BF16),
                        pltpu.VMEM((n_sub, tq, span), _F32), pltpu.VMEM((n_sub, rows, span), _F32),
                        pltpu.VMEM((n_sub, rows, span), _BF16)]
        + [pltpu.VMEM((n_sub, rows, HEAD_DIM), _F32)] * 2,
        compiler_params=_params("arbitrary", "arbitrary", "arbitrary"),
        name="nsa_attention",
    )(slabs, *prepped, sel, o_cmp, slabs, cos, sin)


def _conv_kernel(u_ref, c_ref, b_ref, w_ref, o_ref):
    seq = u_ref.shape[2]
    row = lax.broadcasted_iota(jnp.int32, (seq, LANE), 0)
    u = c_ref[0, 0] * u_ref[0, 0]
    taps = w_ref.shape[0]
    conv = _shift_rows(u, taps - 1, row) * w_ref[0:1, :]
    for j in range(1, taps):
        shifted = u if j == taps - 1 else _shift_rows(u, taps - 1 - j, row)
        conv = conv + shifted * w_ref[j:j + 1, :]
    o_ref[0] = (b_ref[0, 0] * conv).astype(o_ref.dtype)


def _short_conv(slabs, conv_w, first_slab):
    batch, _, seq, _ = slabs.shape
    nd = D_WIDTH // LANE
    part = lambda off: pl.BlockSpec((1, 1, seq, LANE), lambda b, j: (b, first_slab + off + j, 0, 0))
    return pl.pallas_call(
        _conv_kernel,
        grid=(batch, nd),
        in_specs=[part(0), part(nd), part(2 * nd),
                  pl.BlockSpec((conv_w.shape[0], LANE), lambda b, j: (0, j))],
        out_specs=pl.BlockSpec((1, seq, LANE), lambda b, j: (b, 0, j)),
        out_shape=jax.ShapeDtypeStruct((batch, seq, D_WIDTH), _BF16),
        compiler_params=_params("arbitrary", "arbitrary"),
        name="short_conv",
    )(slabs, slabs, slabs, conv_w)


def _pad_cols(w, total):
    return jnp.pad(w, ((0, 0), (0, total - w.shape[1])))


def _odd_in_weight(w):
    qkv = C_HEADS * HEAD_DIM + 6 * C_KV_HEADS * HEAD_DIM
    ngate = 3 * C_GROUP
    gates = [_pad_cols(w[:, qkv + g * ngate:qkv + (g + 1) * ngate], LANE) for g in range(C_KV_HEADS)]
    rest = w[:, qkv + C_KV_HEADS * ngate:]
    return _pad_cols(jnp.concatenate([w[:, :qkv]] + gates + [rest], axis=1), PROJ_SLABS * LANE)


def kernel(x, positions, norm_w, ffn_w_gate, ffn_w_up, ffn_w_down, ev_w_in, ev_w_out, pool_w, pool_scale,
           od_w_in, od_w_out, cmp_pe_k, cmp_w1_k, cmp_w2_k, cmp_pe_v, cmp_w1_v, cmp_w2_v, conv_w):
    batch, seq, d = x.shape
    depth = norm_w.shape[0]
    bf = lambda a: a.astype(_BF16)

    inv_freq = 1.0 / (ROPE_THETA ** (jnp.arange(0, HEAD_DIM, 2, dtype=_F32) / HEAD_DIM))
    ang = positions.astype(_F32)[..., None] * inv_freq
    cos = jnp.concatenate([jnp.cos(ang), jnp.cos(ang)], axis=-1)
    sin = jnp.concatenate([-jnp.sin(ang), jnp.sin(ang)], axis=-1)

    nw = norm_w.reshape(depth, 6, 1, d)
    h = x.reshape(batch * seq, d)

    w_gate, w_up, w_down = bf(ffn_w_gate), bf(ffn_w_up), bf(ffn_w_down)
    w_ev_in, w_ev_out = bf(ev_w_in), bf(ev_w_out)
    w_od_in, w_od_out = bf(jax.vmap(_odd_in_weight)(od_w_in)), bf(od_w_out)

    def ffn(h, layer, which):
        a = _ffn_up(h, nw[layer, 4 * which], w_gate, w_up, (layer, which))
        return _ffn_down(a, w_down, (layer, which), h, nw[layer, 4 * which + 1], HALF_STEP)

    for layer in range(depth):
        i = layer // 2
        h = ffn(h, layer, 0)
        if layer % 2 == 0:
            slabs = _norm_proj(h, nw[layer, 2], w_ev_in, (i,), batch, seq, rope=(cos, sin),
                               n_rope=2 * A_HEADS, n_scaled=A_HEADS, q_scale=HEAD_DIM ** -0.5 * LOG2_E)
            o_a = _dilated_attention(slabs)
            o_b = _multiscale_pool(slabs, bf(pool_w[i]), pool_scale[i].reshape(1, -1))
            h = _mix_out(o_a.reshape(batch * seq, -1), o_b.reshape(batch * seq, -1), w_ev_out, (i,), h,
                         nw[layer, 3])
        else:
            slabs = _norm_proj(h, nw[layer, 2], w_od_in, (i,), batch, seq)
            kv_cmp = _compress(slabs, jnp.stack([cmp_pe_k[i], cmp_pe_v[i]]),
                               bf(jnp.stack([cmp_w1_k[i], cmp_w1_v[i]])),
                               bf(jnp.stack([cmp_w2_k[i], cmp_w2_v[i]])), C_HEADS)
            o_cmp, sel = _cmp_select(slabs, kv_cmp)
            o_c = _nsa_attention(slabs, _nsa_prep(slabs, cos, sin), sel, o_cmp, cos, sin)
            y_d = _short_conv(slabs, conv_w[i], 26)
            h = _mix_out(o_c.reshape(batch * seq, -1), y_d.reshape(batch * seq, -1), w_od_out, (i,), h,
                         nw[layer, 3])
        h = ffn(h, layer, 1)
    return h.reshape(batch, seq, d)
```

```python
import functools

import jax
import jax.numpy as jnp
from jax import lax
from jax.experimental import pallas as pl
from jax.experimental.pallas import tpu as pltpu

HEAD_DIM = 128
ROPE_THETA = 10000.0
NORM_EPS = 1e-6
LOG2_E = 1.4426950408889634
HALF_STEP = 0.5
LANE = 128

A_HEADS = 12
A_DILATIONS = (1, 4, 16)
A_BACK = 128
B_WINDOWS = (2, 4, 8, 16)
B_GROUPS = 4

C_HEADS = 12
C_KV_HEADS = 2
C_GROUP = C_HEADS // C_KV_HEADS
CMP_BLOCK = 32
CMP_STRIDE = 16
SLC_BLOCK = 64
SLC_TOPN = 8
WIN_SIZE = 512
FORCED_SCORE = 1e9
D_WIDTH = 512
PROJ_SLABS = 40
NORM_SUB_TILES = 4

VMEM_LIMIT = 56 * 1024 * 1024

_BF16 = jnp.bfloat16
_F32 = jnp.float32


def _params(*sem):
    return pltpu.CompilerParams(dimension_semantics=sem, vmem_limit_bytes=VMEM_LIMIT)


def _rms(x, w):
    return x * lax.rsqrt(jnp.mean(x * x, axis=-1, keepdims=True) + NORM_EPS) * w


def _dot(a, b):
    return jnp.dot(a, b, preferred_element_type=_F32)


def _dot_t(a, b):
    return lax.dot_general(a, b, (((1,), (1,)), ((), ())), preferred_element_type=_F32)


def _rope(x, cos, sin):
    return x * cos + pltpu.roll(x, HEAD_DIM // 2, 1) * sin


def _shift_rows(x, k, row):
    return jnp.where(row >= k, pltpu.roll(x, k, 0), 0.0)


def _proj_kernel(h_ref, nw_ref, w_ref, *rest, n_rope, n_scaled, q_scale):
    if n_rope:
        cos_ref, sin_ref, o_ref, xn_ref = rest
    else:
        o_ref, xn_ref = rest
    per_step = o_ref.shape[1]
    tn = per_step * LANE

    def use(xn, r, step):
        if step is None:
            cols = pl.ds(pl.multiple_of(pl.program_id(1) * tn, tn), tn)
        else:
            cols = slice(step * tn, (step + 1) * tn)
        acc = _dot(xn, w_ref[:, cols])
        for s in range(per_step):
            x = acc[:, s * LANE:(s + 1) * LANE]
            slab = None if step is None else step * per_step + s
            if slab is not None and slab < n_rope:
                x = _rope(x, cos_ref[0, r, :], sin_ref[0, r, :])
                if slab < n_scaled:
                    x = x * q_scale
            o_ref[0, s, r, :] = x

    steps = w_ref.shape[1] // tn if n_rope else None
    _normed_rows(h_ref, nw_ref, xn_ref, use, static_steps=steps)


def _stacked(block, index_map, lead, **kwargs):
    return pl.BlockSpec((None,) * len(lead) + block, lambda *g: tuple(lead) + index_map(*g), **kwargs)


def _norm_proj(h, nw, w, lead, batch, seq, rope=None, n_rope=0, n_scaled=0, q_scale=1.0, tm=1024, tn=1024):
    t, d = h.shape
    n = w.shape[-1]
    per_b = seq // tm
    table = pl.BlockSpec((1, tm, LANE), lambda i, j: (i // per_b, i % per_b, 0))
    return pl.pallas_call(
        functools.partial(_proj_kernel, n_rope=n_rope, n_scaled=n_scaled, q_scale=q_scale),
        grid=(t // tm, n // tn),
        in_specs=[pl.BlockSpec((tm, d), lambda i, j: (i, 0)),
                  pl.BlockSpec((1, d), lambda i, j: (0, 0)),
                  _stacked((d, n), lambda i, j: (0, 0), lead, pipeline_mode=pl.Buffered(1))]
        + ([table, table] if n_rope else []),
        out_specs=pl.BlockSpec((1, tn // LANE, tm, LANE), lambda i, j: (i // per_b, j, i % per_b, 0)),
        out_shape=jax.ShapeDtypeStruct((batch, n // LANE, seq, LANE), _F32),
        scratch_shapes=[pltpu.VMEM((tm, d), _BF16)],
        compiler_params=_params("arbitrary", "arbitrary"),
        name="norm_proj",
    )(h, nw, w, *(rope if n_rope else ()))


def _normed_rows(h_ref, nw_ref, xn_ref, use, static_steps=None):
    j = pl.program_id(1)

    def first(step):
        sub = h_ref.shape[0] // NORM_SUB_TILES
        for s in range(NORM_SUB_TILES):
            r = slice(s * sub, (s + 1) * sub)
            xn = _rms(h_ref[r, :], nw_ref[...]).astype(_BF16)
            xn_ref[r, :] = xn
            use(xn, r, step)

    if static_steps is None:
        pl.when(j == 0)(lambda: first(None))
        pl.when(j > 0)(lambda: use(xn_ref[...], slice(None), None))
    else:
        pl.when(j == 0)(lambda: first(0))
        for step in range(1, static_steps):
            pl.when(j == step)(lambda step=step: use(xn_ref[...], slice(None), step))


def _ffn_up_kernel(h_ref, nw_ref, wg_ref, wu_ref, o_ref, xn_ref):
    def use(xn, r, step):
        g = _dot(xn, wg_ref[...])
        u = _dot(xn, wu_ref[...])
        o_ref[r, :] = (jax.nn.silu(g) * u).astype(_BF16)

    _normed_rows(h_ref, nw_ref, xn_ref, use)


def _ffn_up(h, nw, wg, wu, lead, tm=1024, tn=512):
    t, d = h.shape
    f = wg.shape[-1]
    return pl.pallas_call(
        _ffn_up_kernel,
        grid=(t // tm, f // tn),
        in_specs=[pl.BlockSpec((tm, d), lambda i, j: (i, 0)),
                  pl.BlockSpec((1, d), lambda i, j: (0, 0)),
                  _stacked((d, tn), lambda i, j: (0, j), lead),
                  _stacked((d, tn), lambda i, j: (0, j), lead)],
        out_specs=pl.BlockSpec((tm, tn), lambda i, j: (i, j)),
        out_shape=jax.ShapeDtypeStruct((t, f), _BF16),
        scratch_shapes=[pltpu.VMEM((tm, d), _BF16)],
        compiler_params=_params("arbitrary", "arbitrary"),
        name="ffn_up",
    )(h, nw, wg, wu)


def _ffn_down_kernel(a_ref, w_ref, h_ref, nw_ref, o_ref, *, scale, n_sub):
    sub = a_ref.shape[0] // n_sub
    for s in range(n_sub):
        r = slice(s * sub, (s + 1) * sub)
        f = _dot(a_ref[r, :], w_ref[...])
        o_ref[r, :] = h_ref[r, :] + scale * _rms(f, nw_ref[...])


def _ffn_fused_kernel(h_ref, nw_in_ref, wg_ref, wu_ref, wd_ref, nw_out_ref, o_ref, xn_ref, acc_ref, *, scale):
    j = pl.program_id(1)
    last = pl.num_programs(1) - 1

    @pl.when(j == 0)
    def _():
        xn_ref[...] = _rms(h_ref[...], nw_in_ref[...]).astype(_BF16)

    xn = xn_ref[...]
    a = (jax.nn.silu(_dot(xn, wg_ref[...])) * _dot(xn, wu_ref[...])).astype(_BF16)
    part = _dot(a, wd_ref[...])

    @pl.when(j == 0)
    def _():
        acc_ref[...] = part

    @pl.when((j > 0) & (j < last))
    def _():
        acc_ref[...] += part

    @pl.when(j == last)
    def _():
        o_ref[...] = h_ref[...] + scale * _rms(acc_ref[...] + part, nw_out_ref[...])


def _ffn_fused(h, nw_in, wg, wu, wd, lead, nw_out, scale, tm=512, tf=512):
    t, d = h.shape
    f = wg.shape[-1]
    assert f // tf >= 2, "the first and last hidden chunks must be different grid steps"
    return pl.pallas_call(
        functools.partial(_ffn_fused_kernel, scale=scale),
        grid=(t // tm, f // tf),
        in_specs=[pl.BlockSpec((tm, d), lambda i, j: (i, 0)),
                  pl.BlockSpec((1, d), lambda i, j: (0, 0)),
                  _stacked((d, tf), lambda i, j: (0, j), lead),
                  _stacked((d, tf), lambda i, j: (0, j), lead),
                  _stacked((tf, d), lambda i, j: (j, 0), lead),
                  pl.BlockSpec((1, d), lambda i, j: (0, 0))],
        out_specs=pl.BlockSpec((tm, d), lambda i, j: (i, 0)),
        out_shape=jax.ShapeDtypeStruct((t, d), _F32),
        scratch_shapes=[pltpu.VMEM((tm, d), _BF16), pltpu.VMEM((tm, d), _F32)],
        compiler_params=_params("arbitrary", "arbitrary"),
        name="ffn_fused",
    )(h, nw_in, wg, wu, wd, nw_out)


def _ffn_down(a, w, lead, h, nw, scale, tm=512, n_sub=2):
    t, f = a.shape
    d = w.shape[-1]
    w_spec = _stacked((f, d), lambda i: (0, 0), lead, pipeline_mode=pl.Buffered(1))
    return pl.pallas_call(
        functools.partial(_ffn_down_kernel, scale=scale, n_sub=n_sub),
        grid=(t // tm,),
        in_specs=[pl.BlockSpec((tm, f), lambda i: (i, 0)),
                  w_spec,
                  pl.BlockSpec((tm, d), lambda i: (i, 0)),
                  pl.BlockSpec((1, d), lambda i: (0, 0))],
        out_specs=pl.BlockSpec((tm, d), lambda i: (i, 0)),
        out_shape=jax.ShapeDtypeStruct((t, d), _F32),
        compiler_params=_params("arbitrary"),
        name="ffn_down",
    )(a, w, h, nw)


def _mix_out_kernel(x1_ref, x2_ref, w_ref, h_ref, nw_ref, o_ref):
    c1 = x1_ref.shape[1]
    sub = x1_ref.shape[0] // 2
    for s in range(2):
        r = slice(s * sub, (s + 1) * sub)
        m = _dot(x1_ref[r, :], w_ref[0:c1, :]) + _dot(x2_ref[r, :], w_ref[c1:, :])
        o_ref[r, :] = h_ref[r, :] + _rms(m, nw_ref[...])


def _mix_out(x1, x2, w, lead, h, nw, tm=512):
    t, c1 = x1.shape
    c2 = x2.shape[1]
    d = w.shape[-1]
    return pl.pallas_call(
        _mix_out_kernel,
        grid=(t // tm,),
        in_specs=[pl.BlockSpec((tm, c1), lambda i: (i, 0)),
                  pl.BlockSpec((tm, c2), lambda i: (i, 0)),
                  _stacked((c1 + c2, d), lambda i: (0, 0), lead),
                  pl.BlockSpec((tm, d), lambda i: (i, 0)),
                  pl.BlockSpec((1, d), lambda i: (0, 0))],
        out_specs=pl.BlockSpec((tm, d), lambda i: (i, 0)),
        out_shape=jax.ShapeDtypeStruct((t, d), _F32),
        compiler_params=_params("arbitrary"),
        name="mix_out",
    )(x1, x2, w, h, nw)


def _rows(start, size, stride):
    return pl.ds(start, size) if stride == 1 else pl.ds(start, size, stride=stride)


def _dilated_kernel(q_ref, k_ref, v_ref, o_ref, s_s, p_s, pv_s, m_s, l_s):
    seq = q_ref.shape[2]
    blk = A_BACK
    chunk = 256

    qi = lax.broadcasted_iota(jnp.int32, (blk, blk), 0)
    kj = lax.broadcasted_iota(jnp.int32, (blk, blk), 1)
    cur_ok = kj <= qi
    both_ok = jnp.concatenate([kj >= qi, cur_ok], axis=1)
    ones = jnp.ones((blk, HEAD_DIM), _BF16)

    blocks = []
    for g, dil in enumerate(A_DILATIONS):
        for res in range(dil):
            prev = None
            for n in range(seq // (blk * dil)):
                rows = _rows(res + n * blk * dil, blk, dil)
                blocks.append((g, rows, prev))
                prev = rows

    def keys_bf16(rows):
        return k_ref[0, 0, rows, :].astype(_BF16)

    def values1_bf16(rows):
        return jnp.concatenate([v_ref[0, 0, rows, :].astype(_BF16), ones], axis=1)

    for b, (g, rows, prev) in enumerate(blocks):
        qb = q_ref[0, 0, rows, :].astype(_BF16)
        if prev is None:
            s_s[b, :, 0:blk] = jnp.where(cur_ok, _dot_t(qb, keys_bf16(rows)), -jnp.inf)
        else:
            k_all = jnp.concatenate([keys_bf16(prev), keys_bf16(rows)], axis=0)
            s_s[b] = jnp.where(both_ok, _dot_t(qb, k_all), -jnp.inf)

    for b, (g, rows, prev) in enumerate(blocks):
        width = blk if prev is None else 2 * blk
        s = s_s[b, :, 0:width]
        m = jnp.max(s, axis=-1, keepdims=True)
        p_s[b, :, 0:width] = jnp.exp2(s - m).astype(_BF16)
        m_s[g, rows, :] = jnp.broadcast_to(m, (blk, HEAD_DIM))

    for b, (g, rows, prev) in enumerate(blocks):
        if prev is None:
            pva = _dot(p_s[b, :, 0:blk], values1_bf16(rows))
        else:
            pva = _dot(p_s[b], jnp.concatenate([values1_bf16(prev), values1_bf16(rows)], axis=0))
        pv_s[g, rows, :] = pva[:, 0:HEAD_DIM]
        l_s[g, rows, :] = pva[:, HEAD_DIM:]

    for c in range(0, seq, chunk):
        r = slice(c, c + chunk)
        m_all = jnp.maximum(jnp.maximum(m_s[0, r, :], m_s[1, r, :]), m_s[2, r, :])
        num = jnp.zeros((chunk, HEAD_DIM), _F32)
        den = jnp.zeros((chunk, HEAD_DIM), _F32)
        for g in range(len(A_DILATIONS)):
            w = jnp.exp2(m_s[g, r, :] - m_all)
            num = num + w * pv_s[g, r, :]
            den = den + w * l_s[g, r, :]
        o_ref[0, r, :] = (num / den).astype(o_ref.dtype)


def _dilated_attention(slabs):
    batch, _, seq, _ = slabs.shape
    head = lambda off: pl.BlockSpec((1, 1, seq, LANE), lambda b, h: (b, off + h, 0, 0))
    nblocks = len(A_DILATIONS) * seq // A_BACK
    return pl.pallas_call(
        _dilated_kernel,
        grid=(batch, A_HEADS),
        in_specs=[head(0), head(A_HEADS), head(2 * A_HEADS)],
        out_specs=pl.BlockSpec((1, seq, LANE), lambda b, h: (b, 0, h)),
        out_shape=jax.ShapeDtypeStruct((batch, seq, A_HEADS * HEAD_DIM), _BF16),
        scratch_shapes=[pltpu.VMEM((nblocks, A_BACK, 2 * A_BACK), _F32),
                        pltpu.VMEM((nblocks, A_BACK, 2 * A_BACK), _BF16),
                        pltpu.VMEM((3, seq, HEAD_DIM), _F32), pltpu.VMEM((3, seq, HEAD_DIM), _F32),
                        pltpu.VMEM((3, seq, HEAD_DIM), _F32)],
        compiler_params=_params("arbitrary", "arbitrary"),
        name="dilated_attention",
    )(slabs, slabs, slabs)


def _pool_kernel(u_ref, w_ref, sc_ref, o_ref):
    seq = u_ref.shape[2]
    row = lax.broadcasted_iota(jnp.int32, (seq, LANE), 0)
    for g, win in enumerate(B_WINDOWS):
        x = u_ref[0, g]
        s = x
        step = 1
        while step < win:
            s = s + _shift_rows(s, step, row)
            step *= 2
        cnt = jnp.minimum(row + 1, win).astype(_F32)
        pooled = s / cnt - x
        mixed = _dot(pooled.astype(_BF16), w_ref[g]) * sc_ref[:, g * LANE:(g + 1) * LANE]
        o_ref[0, :, g * LANE:(g + 1) * LANE] = mixed.astype(o_ref.dtype)


def _multiscale_pool(slabs, pool_w, pool_scale):
    batch, _, seq, _ = slabs.shape
    first = 3 * A_HEADS // B_GROUPS
    return pl.pallas_call(
        _pool_kernel,
        grid=(batch,),
        in_specs=[pl.BlockSpec((1, B_GROUPS, seq, LANE), lambda b: (b, first, 0, 0)),
                  pl.BlockSpec((B_GROUPS, LANE, LANE), lambda b: (0, 0, 0)),
                  pl.BlockSpec((1, B_GROUPS * LANE), lambda b: (0, 0))],
        out_specs=pl.BlockSpec((1, seq, B_GROUPS * LANE), lambda b: (b, 0, 0)),
        out_shape=jax.ShapeDtypeStruct((batch, seq, B_GROUPS * LANE), _BF16),
        compiler_params=_params("arbitrary"),
        name="multiscale_pool",
    )(slabs, pool_w, pool_scale)


def _compress_kernel(x_ref, pe_ref, w1_ref, w2_ref, o_ref):
    nrow = o_ref.shape[3]
    hidden = w1_ref.shape[2]
    a = jnp.zeros((nrow, hidden), _F32)
    b = jnp.zeros((nrow, hidden), _F32)
    for tok in range(CMP_STRIDE):
        x = x_ref[0, 0, pl.ds(tok, nrow, stride=CMP_STRIDE), :]
        lo, hi = tok, CMP_STRIDE + tok
        a = a + _dot((x + pe_ref[0, lo:lo + 1, :]).astype(_BF16), w1_ref[0, lo * HEAD_DIM:(lo + 1) * HEAD_DIM, :])
        b = b + _dot((x + pe_ref[0, hi:hi + 1, :]).astype(_BF16), w1_ref[0, hi * HEAD_DIM:(hi + 1) * HEAD_DIM, :])
    hid = a + pltpu.roll(b, nrow - 1, 0)
    o_ref[0, 0, 0] = _dot(jax.nn.gelu(hid).astype(_BF16), w2_ref[0])


def _compress(slabs, pe, w1, w2, first_slab):
    batch, _, seq, _ = slabs.shape
    nrow = seq // CMP_STRIDE
    return pl.pallas_call(
        _compress_kernel,
        grid=(2, batch, C_KV_HEADS),
        in_specs=[pl.BlockSpec((1, 1, seq, LANE), lambda kv, b, g: (b, first_slab + 2 * kv + g, 0, 0)),
                  pl.BlockSpec((1, CMP_BLOCK, HEAD_DIM), lambda kv, b, g: (kv, 0, 0)),
                  pl.BlockSpec((1, CMP_BLOCK * HEAD_DIM, w1.shape[2]), lambda kv, b, g: (kv, 0, 0)),
                  pl.BlockSpec((1, w2.shape[1], HEAD_DIM), lambda kv, b, g: (kv, 0, 0))],
        out_specs=pl.BlockSpec((1, 1, 1, nrow, HEAD_DIM), lambda kv, b, g: (kv, b, g, 0, 0)),
        out_shape=jax.ShapeDtypeStruct((2, batch, C_KV_HEADS, nrow, HEAD_DIM), _F32),
        compiler_params=_params("arbitrary", "arbitrary", "arbitrary"),
        name="compress",
    )(slabs, pe, w1, w2)


def _cmp_select_kernel(q_ref, kc_ref, vc_ref, o_ref, sel_ref, *, nslc, n_sub):
    tq = q_ref.shape[2] // n_sub
    ncmp = kc_ref.shape[3]
    scale = HEAD_DIM ** -0.5
    n = lax.broadcasted_iota(jnp.int32, (tq, ncmp), 1)

    cj = lax.broadcasted_iota(jnp.int32, (LANE, ncmp), 0) * SLC_BLOCK
    cn = lax.broadcasted_iota(jnp.int32, (LANE, ncmp), 1) * CMP_STRIDE
    cover_t = ((cn < cj + SLC_BLOCK) & (cn + CMP_BLOCK > cj)).astype(_BF16)
    j = lax.broadcasted_iota(jnp.int32, (nslc, tq), 0)
    jf = j.astype(_F32)

    kc = kc_ref[0, 0, 0].astype(_BF16)
    vc = vc_ref[0, 0, 0].astype(_BF16)
    for u in range(n_sub):
        rows = slice(u * tq, (u + 1) * tq)
        t0 = (pl.program_id(2) * n_sub + u) * tq
        t = t0 + lax.broadcasted_iota(jnp.int32, (tq, ncmp), 0)
        cmask = n * CMP_STRIDE + (CMP_BLOCK - 1) <= t
        imp_t = jnp.zeros((LANE, tq), _F32)
        for hh in range(C_GROUP):
            sc = jnp.where(cmask, _dot_t(q_ref[0, hh, rows, :].astype(_BF16), kc) * scale, -jnp.inf)
            m = jnp.max(sc, axis=-1, keepdims=True)
            m = jnp.where(jnp.isfinite(m), m, 0.0)
            p = jnp.exp(sc - m)
            den = jnp.sum(p, axis=-1, keepdims=True)
            p = (p / jnp.maximum(den, 1.0)).astype(_BF16)
            o_ref[0, hh, rows, :] = _dot(p, vc)
            imp_t = imp_t + _dot_t(cover_t, p)

        tt = t0 + lax.broadcasted_iota(jnp.int32, (nslc, tq), 1)
        cur = lax.shift_right_logical(tt, SLC_BLOCK.bit_length() - 1)
        visible = j <= cur
        forced = visible & ((j == 0) | (j >= cur - 1))
        score = jnp.where(forced, FORCED_SCORE, jnp.where(visible, imp_t[0:nslc, :], -FORCED_SCORE))
        sel_t = jnp.zeros((nslc, tq), _F32)
        for _ in range(min(SLC_TOPN, nslc)):
            best = jnp.max(score, axis=0, keepdims=True)
            pick = jnp.min(jnp.where(score == best, jf, float(nslc)), axis=0, keepdims=True)
            hit = jf == pick
            sel_t = jnp.where(hit, 1.0, sel_t)
            score = jnp.where(hit, -jnp.inf, score)
        sel_t = jnp.concatenate([sel_t, jnp.zeros((LANE - nslc, tq), _F32)], axis=0)
        sel_ref[0, 0, rows, :] = sel_t.T.astype(sel_ref.dtype)


def _cmp_select(slabs, kv_cmp, tq=1024, n_sub=4):
    batch, _, seq, _ = slabs.shape
    cmp_spec = lambda which: pl.BlockSpec((1, 1, 1) + kv_cmp.shape[3:], lambda b, g, i: (which, b, g, 0, 0))
    return pl.pallas_call(
        functools.partial(_cmp_select_kernel, nslc=seq // SLC_BLOCK, n_sub=n_sub),
        grid=(batch, C_KV_HEADS, seq // tq),
        in_specs=[pl.BlockSpec((1, C_GROUP, tq, LANE), lambda b, g, i: (b, g, i, 0)),
                  cmp_spec(0), cmp_spec(1)],
        out_specs=[pl.BlockSpec((1, C_GROUP, tq, LANE), lambda b, g, i: (b, g, i, 0)),
                   pl.BlockSpec((1, 1, tq, LANE), lambda b, g, i: (b, g, i, 0))],
        out_shape=[jax.ShapeDtypeStruct((batch, C_HEADS, seq, HEAD_DIM), _F32),
                   jax.ShapeDtypeStruct((batch, C_KV_HEADS, seq, LANE), _BF16)],
        compiler_params=_params("arbitrary", "arbitrary", "arbitrary"),
        name="cmp_select",
    )(slabs, kv_cmp, kv_cmp)


def _group_softmax_pv(q, k, v1, bias_s, s_s, p_s, o_s):
    nk = k.shape[0]
    tq = bias_s.shape[0]
    pieces = [slice(c, min(c + 512, nk)) for c in range(0, nk, 512)]
    s = _dot_t(q, k)
    for hh in range(C_GROUP):
        r = slice(hh * tq, (hh + 1) * tq)
        s_s[r, 0:nk] = s[r] + bias_s[:, 0:nk]
    for hh in range(C_GROUP):
        r = slice(hh * tq, (hh + 1) * tq)
        m = None
        for c in pieces:
            mc = jnp.max(s_s[r, c], axis=-1, keepdims=True)
            m = mc if m is None else jnp.maximum(m, mc)
        for c in pieces:
            p_s[r, c] = jnp.exp2(s_s[r, c] - m).astype(_BF16)
    o = _dot(p_s[:, 0:nk], v1)
    o_s[...] = o[:, 0:HEAD_DIM] / o[:, HEAD_DIM:]


def _nsa_kernel(q_ref, ks_ref, vs_ref, kw_ref, vw_ref, sel_ref, oc_ref, gl_ref, cos_ref, sin_ref, o_ref,
                ksr, vs1, kwr, vw1, expand_s, qr, bias_s, s_s, p_s, bias_w, s_w, p_w, oslc_s, owin_s):
    n_sub = qr.shape[0]
    tq = q_ref.shape[2] // n_sub
    seq = ks_ref.shape[2]
    qi = pl.program_id(2)
    chunk = 256
    shift = SLC_BLOCK.bit_length() - 1

    @pl.when(qi == 0)
    def _():
        def prep(c, _):
            r = pl.ds(pl.multiple_of(c * chunk, chunk), chunk)
            cos, sin = cos_ref[0, r, :], sin_ref[0, r, :]
            ones = jnp.ones((chunk, HEAD_DIM), _BF16)
            ksr[r, :] = _rope(ks_ref[0, 0, r, :], cos, sin).astype(_BF16)
            kwr[r, :] = _rope(kw_ref[0, 0, r, :], cos, sin).astype(_BF16)
            vs1[r, 0:HEAD_DIM] = vs_ref[0, 0, r, :].astype(_BF16)
            vs1[r, HEAD_DIM:] = ones
            vw1[r, 0:HEAD_DIM] = vw_ref[0, 0, r, :].astype(_BF16)
            vw1[r, HEAD_DIM:] = ones
            return 0

        lax.fori_loop(0, seq // chunk, prep, 0)
        blk = lax.broadcasted_iota(jnp.int32, (LANE, seq), 0)
        key = lax.broadcasted_iota(jnp.int32, (LANE, seq), 1)
        expand_s[...] = (blk == lax.shift_right_logical(key, shift)).astype(_BF16)

    base = pl.multiple_of(qi * (n_sub * tq), n_sub * tq)
    sub_rows = [slice(u * tq, (u + 1) * tq) for u in range(n_sub)]
    for u in range(n_sub):
        cos_q = cos_ref[0, pl.ds(base + u * tq, tq), :]
        sin_q = sin_ref[0, pl.ds(base + u * tq, tq), :]
        for hh in range(C_GROUP):
            q_rot = _rope(q_ref[0, hh, sub_rows[u], :], cos_q, sin_q)
            qr[u, hh * tq:(hh + 1) * tq, :] = (q_rot * (HEAD_DIM ** -0.5 * LOG2_E)).astype(_BF16)

    def window_branch(u):
        q0 = base + u * tq
        span = WIN_SIZE + tq
        k0 = pl.multiple_of(jnp.maximum(q0 - WIN_SIZE, 0), tq)
        t = q0 + lax.broadcasted_iota(jnp.int32, (tq, span), 0)
        key = k0 + lax.broadcasted_iota(jnp.int32, (tq, span), 1)
        bias_w[u] = jnp.where((key <= t) & (key > t - WIN_SIZE), 0.0, -jnp.inf)
        _group_softmax_pv(qr[u], kwr[pl.ds(k0, span), :], vw1[pl.ds(k0, span), :],
                          bias_w.at[u], s_w.at[u], p_w.at[u], owin_s.at[u])

    step = n_sub * tq
    for nt in range(1, seq // step + 1):
        @pl.when(qi + 1 == nt)
        def _(nk=nt * step):
            for u in range(n_sub):
                for c in range(0, nk, step):
                    t = base + u * tq + lax.broadcasted_iota(jnp.int32, (tq, step), 0)
                    key = c + lax.broadcasted_iota(jnp.int32, (tq, step), 1)
                    ok = (_dot(sel_ref[0, 0, sub_rows[u], :], expand_s[:, c:c + step]) > 0.5) & (key <= t)
                    bias_s[u, :, c:c + step] = jnp.where(ok, 0.0, -jnp.inf)
                _group_softmax_pv(qr[u], ksr[0:nk, :], vs1[0:nk, :],
                                  bias_s.at[u], s_s.at[u], p_s.at[u], oslc_s.at[u])
                window_branch(u)

    gates = jax.nn.sigmoid(gl_ref[0, 0])
    for u in range(n_sub):
        for hh in range(C_GROUP):
            r = slice(hh * tq, (hh + 1) * tq)
            g_cmp, g_slc, g_win = (gates[sub_rows[u], 3 * hh + c:3 * hh + c + 1] for c in range(3))
            o = g_cmp * oc_ref[0, hh, sub_rows[u], :] + g_slc * oslc_s[u, r, :] + g_win * owin_s[u, r, :]
            o_ref[0, sub_rows[u], hh * HEAD_DIM:(hh + 1) * HEAD_DIM] = o.astype(o_ref.dtype)


def _nsa_attention(slabs, sel, o_cmp, cos, sin, tq=128, n_sub=2):
    batch, _, seq, _ = slabs.shape
    kv = lambda off: pl.BlockSpec((1, 1, seq, LANE), lambda b, g, i: (b, off + g, 0, 0))
    table = pl.BlockSpec((1, seq, LANE), lambda b, g, i: (b, 0, 0))
    rows = C_GROUP * tq
    span = WIN_SIZE + tq
    tile = n_sub * tq
    return pl.pallas_call(
        _nsa_kernel,
        grid=(batch, C_KV_HEADS, seq // tile),
        in_specs=[pl.BlockSpec((1, C_GROUP, tile, LANE), lambda b, g, i: (b, g, i, 0)),
                  kv(16), kv(18), kv(20), kv(22),
                  pl.BlockSpec((1, 1, tile, LANE), lambda b, g, i: (b, g, i, 0)),
                  pl.BlockSpec((1, C_GROUP, tile, LANE), lambda b, g, i: (b, g, i, 0)),
                  pl.BlockSpec((1, 1, tile, LANE), lambda b, g, i: (b, 24 + g, i, 0)),
                  table, table],
        out_specs=pl.BlockSpec((1, tile, C_GROUP * HEAD_DIM), lambda b, g, i: (b, i, g)),
        out_shape=jax.ShapeDtypeStruct((batch, seq, C_HEADS * HEAD_DIM), _BF16),
        scratch_shapes=[pltpu.VMEM((seq, HEAD_DIM), _BF16), pltpu.VMEM((seq, 2 * HEAD_DIM), _BF16)] * 2
        + [pltpu.VMEM((LANE, seq), _BF16), pltpu.VMEM((n_sub, rows, HEAD_DIM), _BF16),
           pltpu.VMEM((n_sub, tq, seq), _F32), pltpu.VMEM((n_sub, rows, seq), _F32),
           pltpu.VMEM((n_sub, rows, seq), _BF16),
           pltpu.VMEM((n_sub, tq, span), _F32), pltpu.VMEM((n_sub, rows, span), _F32),
           pltpu.VMEM((n_sub, rows, span), _BF16)]
        + [pltpu.VMEM((n_sub, rows, HEAD_DIM), _F32)] * 2,
        compiler_params=_params("arbitrary", "arbitrary", "arbitrary"),
        name="nsa_attention",
    )(slabs, slabs, slabs, slabs, slabs, sel, o_cmp, slabs, cos, sin)


def _conv_kernel(u_ref, c_ref, b_ref, w_ref, o_ref):
    seq = u_ref.shape[2]
    row = lax.broadcasted_iota(jnp.int32, (seq, LANE), 0)
    u = c_ref[0, 0] * u_ref[0, 0]
    taps = w_ref.shape[0]
    conv = _shift_rows(u, taps - 1, row) * w_ref[0:1, :]
    for j in range(1, taps):
        shifted = u if j == taps - 1 else _shift_rows(u, taps - 1 - j, row)
        conv = conv + shifted * w_ref[j:j + 1, :]
    o_ref[0] = (b_ref[0, 0] * conv).astype(o_ref.dtype)


def _short_conv(slabs, conv_w, first_slab):
    batch, _, seq, _ = slabs.shape
    nd = D_WIDTH // LANE
    part = lambda off: pl.BlockSpec((1, 1, seq, LANE), lambda b, j: (b, first_slab + off + j, 0, 0))
    return pl.pallas_call(
        _conv_kernel,
        grid=(batch, nd),
        in_specs=[part(0), part(nd), part(2 * nd),
                  pl.BlockSpec((conv_w.shape[0], LANE), lambda b, j: (0, j))],
        out_specs=pl.BlockSpec((1, seq, LANE), lambda b, j: (b, 0, j)),
        out_shape=jax.ShapeDtypeStruct((batch, seq, D_WIDTH), _BF16),
        compiler_params=_params("arbitrary", "arbitrary"),
        name="short_conv",
    )(slabs, slabs, slabs, conv_w)


def _pad_cols(w, total):
    return jnp.pad(w, ((0, 0), (0, total - w.shape[1])))


def _odd_in_weight(w):
    qkv = C_HEADS * HEAD_DIM + 6 * C_KV_HEADS * HEAD_DIM
    ngate = 3 * C_GROUP
    gates = [_pad_cols(w[:, qkv + g * ngate:qkv + (g + 1) * ngate], LANE) for g in range(C_KV_HEADS)]
    rest = w[:, qkv + C_KV_HEADS * ngate:]
    return _pad_cols(jnp.concatenate([w[:, :qkv]] + gates + [rest], axis=1), PROJ_SLABS * LANE)


def kernel(x, positions, norm_w, ffn_w_gate, ffn_w_up, ffn_w_down, ev_w_in, ev_w_out, pool_w, pool_scale,
           od_w_in, od_w_out, cmp_pe_k, cmp_w1_k, cmp_w2_k, cmp_pe_v, cmp_w1_v, cmp_w2_v, conv_w):
    batch, seq, d = x.shape
    depth = norm_w.shape[0]
    bf = lambda a: a.astype(_BF16)

    inv_freq = 1.0 / (ROPE_THETA ** (jnp.arange(0, HEAD_DIM, 2, dtype=_F32) / HEAD_DIM))
    ang = positions.astype(_F32)[..., None] * inv_freq
    cos = jnp.concatenate([jnp.cos(ang), jnp.cos(ang)], axis=-1)
    sin = jnp.concatenate([-jnp.sin(ang), jnp.sin(ang)], axis=-1)

    nw = norm_w.reshape(depth, 6, 1, d)
    h = x.reshape(batch * seq, d)

    w_gate, w_up, w_down = bf(ffn_w_gate), bf(ffn_w_up), bf(ffn_w_down)
    w_ev_in, w_ev_out = bf(ev_w_in), bf(ev_w_out)
    w_od_in, w_od_out = bf(jax.vmap(_odd_in_weight)(od_w_in)), bf(od_w_out)

    def ffn(h, layer, which):
        return _ffn_fused(h, nw[layer, 4 * which], w_gate, w_up, w_down, (layer, which),
                          nw[layer, 4 * which + 1], HALF_STEP)

    for layer in range(depth):
        i = layer // 2
        h = ffn(h, layer, 0)
        if layer % 2 == 0:
            slabs = _norm_proj(h, nw[layer, 2], w_ev_in, (i,), batch, seq, rope=(cos, sin),
                               n_rope=2 * A_HEADS, n_scaled=A_HEADS, q_scale=HEAD_DIM ** -0.5 * LOG2_E)
            o_a = _dilated_attention(slabs)
            o_b = _multiscale_pool(slabs, bf(pool_w[i]), pool_scale[i].reshape(1, -1))
            h = _mix_out(o_a.reshape(batch * seq, -1), o_b.reshape(batch * seq, -1), w_ev_out, (i,), h,
                         nw[layer, 3])
        else:
            slabs = _norm_proj(h, nw[layer, 2], w_od_in, (i,), batch, seq)
            kv_cmp = _compress(slabs, jnp.stack([cmp_pe_k[i], cmp_pe_v[i]]),
                               bf(jnp.stack([cmp_w1_k[i], cmp_w1_v[i]])),
                               bf(jnp.stack([cmp_w2_k[i], cmp_w2_v[i]])), C_HEADS)
            o_cmp, sel = _cmp_select(slabs, kv_cmp)
            o_c = _nsa_attention(slabs, sel, o_cmp, cos, sin)
            y_d = _short_conv(slabs, conv_w[i], 26)
            h = _mix_out(o_c.reshape(batch * seq, -1), y_d.reshape(batch * seq, -1), w_od_out, (i,), h,
                         nw[layer, 3])
        h = ffn(h, layer, 1)
    return h.reshape(batch, seq, d)
```

```python
import functools

import jax
import jax.numpy as jnp
from jax import lax
from jax.experimental import pallas as pl
from jax.experimental.pallas import tpu as pltpu

HEAD_DIM = 128
ROPE_THETA = 10000.0
NORM_EPS = 1e-6
LOG2_E = 1.4426950408889634
HALF_STEP = 0.5
LANE = 128

A_HEADS = 12
A_DILATIONS = (1, 4, 16)
A_BACK = 128
B_WINDOWS = (2, 4, 8, 16)
B_GROUPS = 4

C_HEADS = 12
C_KV_HEADS = 2
C_GROUP = C_HEADS // C_KV_HEADS
CMP_BLOCK = 32
CMP_STRIDE = 16
SLC_BLOCK = 64
SLC_TOPN = 8
WIN_SIZE = 512
FORCED_SCORE = 1e9
D_WIDTH = 512
PROJ_SLABS = 40
NORM_SUB_TILES = 4

VMEM_LIMIT = 56 * 1024 * 1024

_BF16 = jnp.bfloat16
_F32 = jnp.float32


def _params(*sem):
    return pltpu.CompilerParams(dimension_semantics=sem, vmem_limit_bytes=VMEM_LIMIT)


def _rms(x, w):
    return x * lax.rsqrt(jnp.mean(x * x, axis=-1, keepdims=True) + NORM_EPS) * w


def _dot(a, b):
    return jnp.dot(a, b, preferred_element_type=_F32)


def _dot_t(a, b):
    return lax.dot_general(a, b, (((1,), (1,)), ((), ())), preferred_element_type=_F32)


def _rope(x, cos, sin):
    return x * cos + pltpu.roll(x, HEAD_DIM // 2, 1) * sin


def _shift_rows(x, k, row):
    return jnp.where(row >= k, pltpu.roll(x, k, 0), 0.0)


def _proj_kernel(h_ref, nw_ref, w_ref, *rest, n_rope, n_scaled, q_scale):
    if n_rope:
        cos_ref, sin_ref, o_ref, xn_ref = rest
    else:
        o_ref, xn_ref = rest
    per_step = o_ref.shape[1]
    tn = per_step * LANE

    def use(xn, r, step):
        if step is None:
            cols = pl.ds(pl.multiple_of(pl.program_id(1) * tn, tn), tn)
        else:
            cols = slice(step * tn, (step + 1) * tn)
        acc = _dot(xn, w_ref[:, cols])
        for s in range(per_step):
            x = acc[:, s * LANE:(s + 1) * LANE]
            slab = None if step is None else step * per_step + s
            if slab is not None and slab < n_rope:
                x = _rope(x, cos_ref[0, r, :], sin_ref[0, r, :])
                if slab < n_scaled:
                    x = x * q_scale
            o_ref[0, s, r, :] = x

    steps = w_ref.shape[1] // tn if n_rope else None
    _normed_rows(h_ref, nw_ref, xn_ref, use, static_steps=steps)


def _stacked(block, index_map, lead, **kwargs):
    return pl.BlockSpec((None,) * len(lead) + block, lambda *g: tuple(lead) + index_map(*g), **kwargs)


def _norm_proj(h, nw, w, lead, batch, seq, rope=None, n_rope=0, n_scaled=0, q_scale=1.0, tm=1024, tn=1024):
    t, d = h.shape
    n = w.shape[-1]
    per_b = seq // tm
    table = pl.BlockSpec((1, tm, LANE), lambda i, j: (i // per_b, i % per_b, 0))
    return pl.pallas_call(
        functools.partial(_proj_kernel, n_rope=n_rope, n_scaled=n_scaled, q_scale=q_scale),
        grid=(t // tm, n // tn),
        in_specs=[pl.BlockSpec((tm, d), lambda i, j: (i, 0)),
                  pl.BlockSpec((1, d), lambda i, j: (0, 0)),
                  _stacked((d, n), lambda i, j: (0, 0), lead, pipeline_mode=pl.Buffered(1))]
        + ([table, table] if n_rope else []),
        out_specs=pl.BlockSpec((1, tn // LANE, tm, LANE), lambda i, j: (i // per_b, j, i % per_b, 0)),
        out_shape=jax.ShapeDtypeStruct((batch, n // LANE, seq, LANE), _F32),
        scratch_shapes=[pltpu.VMEM((tm, d), _BF16)],
        compiler_params=_params("arbitrary", "arbitrary"),
        name="norm_proj",
    )(h, nw, w, *(rope if n_rope else ()))


def _normed_rows(h_ref, nw_ref, xn_ref, use, static_steps=None):
    j = pl.program_id(1)

    def first(step):
        sub = h_ref.shape[0] // NORM_SUB_TILES
        for s in range(NORM_SUB_TILES):
            r = slice(s * sub, (s + 1) * sub)
            xn = _rms(h_ref[r, :], nw_ref[...]).astype(_BF16)
            xn_ref[r, :] = xn
            use(xn, r, step)

    if static_steps is None:
        pl.when(j == 0)(lambda: first(None))
        pl.when(j > 0)(lambda: use(xn_ref[...], slice(None), None))
    else:
        pl.when(j == 0)(lambda: first(0))
        for step in range(1, static_steps):
            pl.when(j == step)(lambda step=step: use(xn_ref[...], slice(None), step))


def _ffn_up_kernel(h_ref, nw_ref, wg_ref, wu_ref, o_ref, xn_ref):
    def use(xn, r, step):
        g = _dot(xn, wg_ref[...])
        u = _dot(xn, wu_ref[...])
        o_ref[r, :] = (jax.nn.silu(g) * u).astype(_BF16)

    _normed_rows(h_ref, nw_ref, xn_ref, use)


def _ffn_up(h, nw, wg, wu, lead, tm=1024, tn=512):
    t, d = h.shape
    f = wg.shape[-1]
    return pl.pallas_call(
        _ffn_up_kernel,
        grid=(t // tm, f // tn),
        in_specs=[pl.BlockSpec((tm, d), lambda i, j: (i, 0)),
                  pl.BlockSpec((1, d), lambda i, j: (0, 0)),
                  _stacked((d, tn), lambda i, j: (0, j), lead),
                  _stacked((d, tn), lambda i, j: (0, j), lead)],
        out_specs=pl.BlockSpec((tm, tn), lambda i, j: (i, j)),
        out_shape=jax.ShapeDtypeStruct((t, f), _BF16),
        scratch_shapes=[pltpu.VMEM((tm, d), _BF16)],
        compiler_params=_params("arbitrary", "arbitrary"),
        name="ffn_up",
    )(h, nw, wg, wu)


def _ffn_down_kernel(a_ref, w_ref, h_ref, nw_ref, o_ref, *, scale, n_sub):
    sub = a_ref.shape[0] // n_sub
    for s in range(n_sub):
        r = slice(s * sub, (s + 1) * sub)
        f = _dot(a_ref[r, :], w_ref[...])
        o_ref[r, :] = h_ref[r, :] + scale * _rms(f, nw_ref[...])


def _ffn_down(a, w, lead, h, nw, scale, tm=512, n_sub=2):
    t, f = a.shape
    d = w.shape[-1]
    w_spec = _stacked((f, d), lambda i: (0, 0), lead, pipeline_mode=pl.Buffered(1))
    return pl.pallas_call(
        functools.partial(_ffn_down_kernel, scale=scale, n_sub=n_sub),
        grid=(t // tm,),
        in_specs=[pl.BlockSpec((tm, f), lambda i: (i, 0)),
                  w_spec,
                  pl.BlockSpec((tm, d), lambda i: (i, 0)),
                  pl.BlockSpec((1, d), lambda i: (0, 0))],
        out_specs=pl.BlockSpec((tm, d), lambda i: (i, 0)),
        out_shape=jax.ShapeDtypeStruct((t, d), _F32),
        compiler_params=_params("arbitrary"),
        name="ffn_down",
    )(a, w, h, nw)


def _mix_out_kernel(x1_ref, x2_ref, w_ref, h_ref, nw_ref, o_ref):
    c1 = x1_ref.shape[1]
    sub = x1_ref.shape[0] // 2
    for s in range(2):
        r = slice(s * sub, (s + 1) * sub)
        m = _dot(x1_ref[r, :], w_ref[0:c1, :]) + _dot(x2_ref[r, :], w_ref[c1:, :])
        o_ref[r, :] = h_ref[r, :] + _rms(m, nw_ref[...])


def _mix_out(x1, x2, w, lead, h, nw, tm=512):
    t, c1 = x1.shape
    c2 = x2.shape[1]
    d = w.shape[-1]
    return pl.pallas_call(
        _mix_out_kernel,
        grid=(t // tm,),
        in_specs=[pl.BlockSpec((tm, c1), lambda i: (i, 0)),
                  pl.BlockSpec((tm, c2), lambda i: (i, 0)),
                  _stacked((c1 + c2, d), lambda i: (0, 0), lead),
                  pl.BlockSpec((tm, d), lambda i: (i, 0)),
                  pl.BlockSpec((1, d), lambda i: (0, 0))],
        out_specs=pl.BlockSpec((tm, d), lambda i: (i, 0)),
        out_shape=jax.ShapeDtypeStruct((t, d), _F32),
        compiler_params=_params("arbitrary"),
        name="mix_out",
    )(x1, x2, w, h, nw)


def _rows(start, size, stride):
    return pl.ds(start, size) if stride == 1 else pl.ds(start, size, stride=stride)


def _dilated_kernel(q_ref, k_ref, v_ref, o_ref, s_s, p_s, pv_s, m_s, l_s):
    seq = q_ref.shape[2]
    blk = A_BACK
    chunk = 256

    qi = lax.broadcasted_iota(jnp.int32, (blk, blk), 0)
    kj = lax.broadcasted_iota(jnp.int32, (blk, blk), 1)
    cur_ok = kj <= qi
    both_ok = jnp.concatenate([kj >= qi, cur_ok], axis=1)
    ones = jnp.ones((blk, HEAD_DIM), _BF16)

    blocks = []
    for g, dil in enumerate(A_DILATIONS):
        for res in range(dil):
            prev = None
            for n in range(seq // (blk * dil)):
                rows = _rows(res + n * blk * dil, blk, dil)
                blocks.append((g, rows, prev))
                prev = rows

    def keys_bf16(rows):
        return k_ref[0, 0, rows, :].astype(_BF16)

    def values1_bf16(rows):
        return jnp.concatenate([v_ref[0, 0, rows, :].astype(_BF16), ones], axis=1)

    for b, (g, rows, prev) in enumerate(blocks):
        qb = q_ref[0, 0, rows, :].astype(_BF16)
        if prev is None:
            s_s[b, :, 0:blk] = jnp.where(cur_ok, _dot_t(qb, keys_bf16(rows)), -jnp.inf)
        else:
            k_all = jnp.concatenate([keys_bf16(prev), keys_bf16(rows)], axis=0)
            s_s[b] = jnp.where(both_ok, _dot_t(qb, k_all), -jnp.inf)

    for b, (g, rows, prev) in enumerate(blocks):
        width = blk if prev is None else 2 * blk
        s = s_s[b, :, 0:width]
        m = jnp.max(s, axis=-1, keepdims=True)
        p_s[b, :, 0:width] = jnp.exp2(s - m).astype(_BF16)
        m_s[g, rows, :] = jnp.broadcast_to(m, (blk, HEAD_DIM))

    for b, (g, rows, prev) in enumerate(blocks):
        if prev is None:
            pva = _dot(p_s[b, :, 0:blk], values1_bf16(rows))
        else:
            pva = _dot(p_s[b], jnp.concatenate([values1_bf16(prev), values1_bf16(rows)], axis=0))
        pv_s[g, rows, :] = pva[:, 0:HEAD_DIM]
        l_s[g, rows, :] = pva[:, HEAD_DIM:]

    for c in range(0, seq, chunk):
        r = slice(c, c + chunk)
        m_all = jnp.maximum(jnp.maximum(m_s[0, r, :], m_s[1, r, :]), m_s[2, r, :])
        num = jnp.zeros((chunk, HEAD_DIM), _F32)
        den = jnp.zeros((chunk, HEAD_DIM), _F32)
        for g in range(len(A_DILATIONS)):
            w = jnp.exp2(m_s[g, r, :] - m_all)
            num = num + w * pv_s[g, r, :]
            den = den + w * l_s[g, r, :]
        o_ref[0, r, :] = (num / den).astype(o_ref.dtype)


def _dilated_attention(slabs):
    batch, _, seq, _ = slabs.shape
    head = lambda off: pl.BlockSpec((1, 1, seq, LANE), lambda b, h: (b, off + h, 0, 0))
    nblocks = len(A_DILATIONS) * seq // A_BACK
    return pl.pallas_call(
        _dilated_kernel,
        grid=(batch, A_HEADS),
        in_specs=[head(0), head(A_HEADS), head(2 * A_HEADS)],
        out_specs=pl.BlockSpec((1, seq, LANE), lambda b, h: (b, 0, h)),
        out_shape=jax.ShapeDtypeStruct((batch, seq, A_HEADS * HEAD_DIM), _BF16),
        scratch_shapes=[pltpu.VMEM((nblocks, A_BACK, 2 * A_BACK), _F32),
                        pltpu.VMEM((nblocks, A_BACK, 2 * A_BACK), _BF16),
                        pltpu.VMEM((3, seq, HEAD_DIM), _F32), pltpu.VMEM((3, seq, HEAD_DIM), _F32),
                        pltpu.VMEM((3, seq, HEAD_DIM), _F32)],
        compiler_params=_params("arbitrary", "arbitrary"),
        name="dilated_attention",
    )(slabs, slabs, slabs)


def _pool_kernel(u_ref, w_ref, sc_ref, o_ref):
    seq = u_ref.shape[2]
    row = lax.broadcasted_iota(jnp.int32, (seq, LANE), 0)
    for g, win in enumerate(B_WINDOWS):
        x = u_ref[0, g]
        s = x
        step = 1
        while step < win:
            s = s + _shift_rows(s, step, row)
            step *= 2
        cnt = jnp.minimum(row + 1, win).astype(_F32)
        pooled = s / cnt - x
        mixed = _dot(pooled.astype(_BF16), w_ref[g]) * sc_ref[:, g * LANE:(g + 1) * LANE]
        o_ref[0, :, g * LANE:(g + 1) * LANE] = mixed.astype(o_ref.dtype)


def _multiscale_pool(slabs, pool_w, pool_scale):
    batch, _, seq, _ = slabs.shape
    first = 3 * A_HEADS // B_GROUPS
    return pl.pallas_call(
        _pool_kernel,
        grid=(batch,),
        in_specs=[pl.BlockSpec((1, B_GROUPS, seq, LANE), lambda b: (b, first, 0, 0)),
                  pl.BlockSpec((B_GROUPS, LANE, LANE), lambda b: (0, 0, 0)),
                  pl.BlockSpec((1, B_GROUPS * LANE), lambda b: (0, 0))],
        out_specs=pl.BlockSpec((1, seq, B_GROUPS * LANE), lambda b: (b, 0, 0)),
        out_shape=jax.ShapeDtypeStruct((batch, seq, B_GROUPS * LANE), _BF16),
        compiler_params=_params("arbitrary"),
        name="multiscale_pool",
    )(slabs, pool_w, pool_scale)


def _compress_kernel(x_ref, pe_ref, w1_ref, w2_ref, o_ref):
    nrow = o_ref.shape[3]
    hidden = w1_ref.shape[2]
    a = jnp.zeros((nrow, hidden), _F32)
    b = jnp.zeros((nrow, hidden), _F32)
    for tok in range(CMP_STRIDE):
        x = x_ref[0, 0, pl.ds(tok, nrow, stride=CMP_STRIDE), :]
        lo, hi = tok, CMP_STRIDE + tok
        a = a + _dot((x + pe_ref[0, lo:lo + 1, :]).astype(_BF16), w1_ref[0, lo * HEAD_DIM:(lo + 1) * HEAD_DIM, :])
        b = b + _dot((x + pe_ref[0, hi:hi + 1, :]).astype(_BF16), w1_ref[0, hi * HEAD_DIM:(hi + 1) * HEAD_DIM, :])
    hid = a + pltpu.roll(b, nrow - 1, 0)
    o_ref[0, 0, 0] = _dot(jax.nn.gelu(hid).astype(_BF16), w2_ref[0])


def _compress(slabs, pe, w1, w2, first_slab):
    batch, _, seq, _ = slabs.shape
    nrow = seq // CMP_STRIDE
    return pl.pallas_call(
        _compress_kernel,
        grid=(2, batch, C_KV_HEADS),
        in_specs=[pl.BlockSpec((1, 1, seq, LANE), lambda kv, b, g: (b, first_slab + 2 * kv + g, 0, 0)),
                  pl.BlockSpec((1, CMP_BLOCK, HEAD_DIM), lambda kv, b, g: (kv, 0, 0)),
                  pl.BlockSpec((1, CMP_BLOCK * HEAD_DIM, w1.shape[2]), lambda kv, b, g: (kv, 0, 0)),
                  pl.BlockSpec((1, w2.shape[1], HEAD_DIM), lambda kv, b, g: (kv, 0, 0))],
        out_specs=pl.BlockSpec((1, 1, 1, nrow, HEAD_DIM), lambda kv, b, g: (kv, b, g, 0, 0)),
        out_shape=jax.ShapeDtypeStruct((2, batch, C_KV_HEADS, nrow, HEAD_DIM), _F32),
        compiler_params=_params("arbitrary", "arbitrary", "arbitrary"),
        name="compress",
    )(slabs, pe, w1, w2)


def _cmp_select_kernel(q_ref, kc_ref, vc_ref, o_ref, sel_ref, *, nslc, n_sub):
    tq = q_ref.shape[2] // n_sub
    ncmp = kc_ref.shape[3]
    scale = HEAD_DIM ** -0.5
    n = lax.broadcasted_iota(jnp.int32, (tq, ncmp), 1)

    cj = lax.broadcasted_iota(jnp.int32, (LANE, ncmp), 0) * SLC_BLOCK
    cn = lax.broadcasted_iota(jnp.int32, (LANE, ncmp), 1) * CMP_STRIDE
    cover_t = ((cn < cj + SLC_BLOCK) & (cn + CMP_BLOCK > cj)).astype(_BF16)
    j = lax.broadcasted_iota(jnp.int32, (nslc, tq), 0)
    jf = j.astype(_F32)

    kc = kc_ref[0, 0, 0].astype(_BF16)
    vc = vc_ref[0, 0, 0].astype(_BF16)
    for u in range(n_sub):
        rows = slice(u * tq, (u + 1) * tq)
        t0 = (pl.program_id(2) * n_sub + u) * tq
        t = t0 + lax.broadcasted_iota(jnp.int32, (tq, ncmp), 0)
        cmask = n * CMP_STRIDE + (CMP_BLOCK - 1) <= t
        imp_t = jnp.zeros((LANE, tq), _F32)
        for hh in range(C_GROUP):
            sc = jnp.where(cmask, _dot_t(q_ref[0, hh, rows, :].astype(_BF16), kc) * scale, -jnp.inf)
            m = jnp.max(sc, axis=-1, keepdims=True)
            m = jnp.where(jnp.isfinite(m), m, 0.0)
            p = jnp.exp(sc - m)
            den = jnp.sum(p, axis=-1, keepdims=True)
            p = (p / jnp.maximum(den, 1.0)).astype(_BF16)
            o_ref[0, hh, rows, :] = _dot(p, vc)
            imp_t = imp_t + _dot_t(cover_t, p)

        tt = t0 + lax.broadcasted_iota(jnp.int32, (nslc, tq), 1)
        cur = lax.shift_right_logical(tt, SLC_BLOCK.bit_length() - 1)
        visible = j <= cur
        forced = visible & ((j == 0) | (j >= cur - 1))
        score = jnp.where(forced, FORCED_SCORE, jnp.where(visible, imp_t[0:nslc, :], -FORCED_SCORE))
        sel_t = jnp.zeros((nslc, tq), _F32)
        for _ in range(min(SLC_TOPN, nslc)):
            best = jnp.max(score, axis=0, keepdims=True)
            pick = jnp.min(jnp.where(score == best, jf, float(nslc)), axis=0, keepdims=True)
            hit = jf == pick
            sel_t = jnp.where(hit, 1.0, sel_t)
            score = jnp.where(hit, -jnp.inf, score)
        sel_t = jnp.concatenate([sel_t, jnp.zeros((LANE - nslc, tq), _F32)], axis=0)
        sel_ref[0, 0, rows, :] = sel_t.T.astype(sel_ref.dtype)


def _cmp_select(slabs, kv_cmp, tq=2048, n_sub=8):
    batch, _, seq, _ = slabs.shape
    cmp_spec = lambda which: pl.BlockSpec((1, 1, 1) + kv_cmp.shape[3:], lambda b, g, i: (which, b, g, 0, 0))
    return pl.pallas_call(
        functools.partial(_cmp_select_kernel, nslc=seq // SLC_BLOCK, n_sub=n_sub),
        grid=(batch, C_KV_HEADS, seq // tq),
        in_specs=[pl.BlockSpec((1, C_GROUP, tq, LANE), lambda b, g, i: (b, g, i, 0)),
                  cmp_spec(0), cmp_spec(1)],
        out_specs=[pl.BlockSpec((1, C_GROUP, tq, LANE), lambda b, g, i: (b, g, i, 0)),
                   pl.BlockSpec((1, 1, tq, LANE), lambda b, g, i: (b, g, i, 0))],
        out_shape=[jax.ShapeDtypeStruct((batch, C_HEADS, seq, HEAD_DIM), _F32),
                   jax.ShapeDtypeStruct((batch, C_KV_HEADS, seq, LANE), _BF16)],
        compiler_params=_params("arbitrary", "arbitrary", "arbitrary"),
        name="cmp_select",
    )(slabs, kv_cmp, kv_cmp)


def _group_softmax_pv(q, k, v1, bias_s, s_s, p_s, o_s):
    nk = k.shape[0]
    tq = bias_s.shape[0]
    pieces = [slice(c, min(c + 512, nk)) for c in range(0, nk, 512)]
    s = _dot_t(q, k)
    for hh in range(C_GROUP):
        r = slice(hh * tq, (hh + 1) * tq)
        s_s[r, 0:nk] = s[r] + bias_s[:, 0:nk]
    for hh in range(C_GROUP):
        r = slice(hh * tq, (hh + 1) * tq)
        m = None
        for c in pieces:
            mc = jnp.max(s_s[r, c], axis=-1, keepdims=True)
            m = mc if m is None else jnp.maximum(m, mc)
        for c in pieces:
            p_s[r, c] = jnp.exp2(s_s[r, c] - m).astype(_BF16)
    o = _dot(p_s[:, 0:nk], v1)
    o_s[...] = o[:, 0:HEAD_DIM] / o[:, HEAD_DIM:]


def _nsa_kernel(q_ref, ks_ref, vs_ref, kw_ref, vw_ref, sel_ref, oc_ref, gl_ref, cos_ref, sin_ref, o_ref,
                ksr, vs1, kwr, vw1, expand_s, qr, bias_s, s_s, p_s, bias_w, s_w, p_w, oslc_s, owin_s):
    n_sub = qr.shape[0]
    tq = q_ref.shape[2] // n_sub
    seq = ks_ref.shape[2]
    qi = pl.program_id(2)
    chunk = 256
    shift = SLC_BLOCK.bit_length() - 1

    @pl.when(qi == 0)
    def _():
        def prep(c, _):
            r = pl.ds(pl.multiple_of(c * chunk, chunk), chunk)
            cos, sin = cos_ref[0, r, :], sin_ref[0, r, :]
            ones = jnp.ones((chunk, HEAD_DIM), _BF16)
            ksr[r, :] = _rope(ks_ref[0, 0, r, :], cos, sin).astype(_BF16)
            kwr[r, :] = _rope(kw_ref[0, 0, r, :], cos, sin).astype(_BF16)
            vs1[r, 0:HEAD_DIM] = vs_ref[0, 0, r, :].astype(_BF16)
            vs1[r, HEAD_DIM:] = ones
            vw1[r, 0:HEAD_DIM] = vw_ref[0, 0, r, :].astype(_BF16)
            vw1[r, HEAD_DIM:] = ones
            return 0

        lax.fori_loop(0, seq // chunk, prep, 0)
        blk = lax.broadcasted_iota(jnp.int32, (LANE, seq), 0)
        key = lax.broadcasted_iota(jnp.int32, (LANE, seq), 1)
        expand_s[...] = (blk == lax.shift_right_logical(key, shift)).astype(_BF16)

    base = pl.multiple_of(qi * (n_sub * tq), n_sub * tq)
    sub_rows = [slice(u * tq, (u + 1) * tq) for u in range(n_sub)]
    for u in range(n_sub):
        cos_q = cos_ref[0, pl.ds(base + u * tq, tq), :]
        sin_q = sin_ref[0, pl.ds(base + u * tq, tq), :]
        for hh in range(C_GROUP):
            q_rot = _rope(q_ref[0, hh, sub_rows[u], :], cos_q, sin_q)
            qr[u, hh * tq:(hh + 1) * tq, :] = (q_rot * (HEAD_DIM ** -0.5 * LOG2_E)).astype(_BF16)

    def window_branch(u):
        q0 = base + u * tq
        span = WIN_SIZE + tq
        k0 = pl.multiple_of(jnp.maximum(q0 - WIN_SIZE, 0), tq)
        t = q0 + lax.broadcasted_iota(jnp.int32, (tq, span), 0)
        key = k0 + lax.broadcasted_iota(jnp.int32, (tq, span), 1)
        bias_w[u] = jnp.where((key <= t) & (key > t - WIN_SIZE), 0.0, -jnp.inf)
        _group_softmax_pv(qr[u], kwr[pl.ds(k0, span), :], vw1[pl.ds(k0, span), :],
                          bias_w.at[u], s_w.at[u], p_w.at[u], owin_s.at[u])

    step = n_sub * tq
    for nt in range(1, seq // step + 1):
        @pl.when(qi + 1 == nt)
        def _(nk=nt * step):
            for u in range(n_sub):
                for c in range(0, nk, step):
                    t = base + u * tq + lax.broadcasted_iota(jnp.int32, (tq, step), 0)
                    key = c + lax.broadcasted_iota(jnp.int32, (tq, step), 1)
                    ok = (_dot(sel_ref[0, 0, sub_rows[u], :], expand_s[:, c:c + step]) > 0.5) & (key <= t)
                    bias_s[u, :, c:c + step] = jnp.where(ok, 0.0, -jnp.inf)
                _group_softmax_pv(qr[u], ksr[0:nk, :], vs1[0:nk, :],
                                  bias_s.at[u], s_s.at[u], p_s.at[u], oslc_s.at[u])
                window_branch(u)

    gates = jax.nn.sigmoid(gl_ref[0, 0])
    for u in range(n_sub):
        for hh in range(C_GROUP):
            r = slice(hh * tq, (hh + 1) * tq)
            g_cmp, g_slc, g_win = (gates[sub_rows[u], 3 * hh + c:3 * hh + c + 1] for c in range(3))
            o = g_cmp * oc_ref[0, hh, sub_rows[u], :] + g_slc * oslc_s[u, r, :] + g_win * owin_s[u, r, :]
            o_ref[0, sub_rows[u], hh * HEAD_DIM:(hh + 1) * HEAD_DIM] = o.astype(o_ref.dtype)


def _nsa_attention(slabs, sel, o_cmp, cos, sin, tq=128, n_sub=2):
    batch, _, seq, _ = slabs.shape
    kv = lambda off: pl.BlockSpec((1, 1, seq, LANE), lambda b, g, i: (b, off + g, 0, 0))
    table = pl.BlockSpec((1, seq, LANE), lambda b, g, i: (b, 0, 0))
    rows = C_GROUP * tq
    span = WIN_SIZE + tq
    tile = n_sub * tq
    return pl.pallas_call(
        _nsa_kernel,
        grid=(batch, C_KV_HEADS, seq // tile),
        in_specs=[pl.BlockSpec((1, C_GROUP, tile, LANE), lambda b, g, i: (b, g, i, 0)),
                  kv(16), kv(18), kv(20), kv(22),
                  pl.BlockSpec((1, 1, tile, LANE), lambda b, g, i: (b, g, i, 0)),
                  pl.BlockSpec((1, C_GROUP, tile, LANE), lambda b, g, i: (b, g, i, 0)),
                  pl.BlockSpec((1, 1, tile, LANE), lambda b, g, i: (b, 24 + g, i, 0)),
                  table, table],
        out_specs=pl.BlockSpec((1, tile, C_GROUP * HEAD_DIM), lambda b, g, i: (b, i, g)),
        out_shape=jax.ShapeDtypeStruct((batch, seq, C_HEADS * HEAD_DIM), _BF16),
        scratch_shapes=[pltpu.VMEM((seq, HEAD_DIM), _BF16), pltpu.VMEM((seq, 2 * HEAD_DIM), _BF16)] * 2
        + [pltpu.VMEM((LANE, seq), _BF16), pltpu.VMEM((n_sub, rows, HEAD_DIM), _BF16),
           pltpu.VMEM((n_sub, tq, seq), _F32), pltpu.VMEM((n_sub, rows, seq), _F32),
           pltpu.VMEM((n_sub, rows, seq), _BF16),
           pltpu.VMEM((n_sub, tq, span), _F32), pltpu.VMEM((n_sub, rows, span), _F32),
           pltpu.VMEM((n_sub, rows, span), _BF16)]
        + [pltpu.VMEM((n_sub, rows, HEAD_DIM), _F32)] * 2,
        compiler_params=_params("arbitrary", "arbitrary", "arbitrary"),
        name="nsa_attention",
    )(slabs, slabs, slabs, slabs, slabs, sel, o_cmp, slabs, cos, sin)


def _conv_kernel(u_ref, c_ref, b_ref, w_ref, o_ref):
    seq = u_ref.shape[2]
    row = lax.broadcasted_iota(jnp.int32, (seq, LANE), 0)
    u = c_ref[0, 0] * u_ref[0, 0]
    taps = w_ref.shape[0]
    conv = _shift_rows(u, taps - 1, row) * w_ref[0:1, :]
    for j in range(1, taps):
        shifted = u if j == taps - 1 else _shift_rows(u, taps - 1 - j, row)
        conv = conv + shifted * w_ref[j:j + 1, :]
    o_ref[0] = (b_ref[0, 0] * conv).astype(o_ref.dtype)


def _short_conv(slabs, conv_w, first_slab):
    batch, _, seq, _ = slabs.shape
    nd = D_WIDTH // LANE
    part = lambda off: pl.BlockSpec((1, 1, seq, LANE), lambda b, j: (b, first_slab + off + j, 0, 0))
    return pl.pallas_call(
        _conv_kernel,
        grid=(batch, nd),
        in_specs=[part(0), part(nd), part(2 * nd),
                  pl.BlockSpec((conv_w.shape[0], LANE), lambda b, j: (0, j))],
        out_specs=pl.BlockSpec((1, seq, LANE), lambda b, j: (b, 0, j)),
        out_shape=jax.ShapeDtypeStruct((batch, seq, D_WIDTH), _BF16),
        compiler_params=_params("arbitrary", "arbitrary"),
        name="short_conv",
    )(slabs, slabs, slabs, conv_w)


def _pad_cols(w, total):
    return jnp.pad(w, ((0, 0), (0, total - w.shape[1])))


def _odd_in_weight(w):
    qkv = C_HEADS * HEAD_DIM + 6 * C_KV_HEADS * HEAD_DIM
    ngate = 3 * C_GROUP
    gates = [_pad_cols(w[:, qkv + g * ngate:qkv + (g + 1) * ngate], LANE) for g in range(C_KV_HEADS)]
    rest = w[:, qkv + C_KV_HEADS * ngate:]
    return _pad_cols(jnp.concatenate([w[:, :qkv]] + gates + [rest], axis=1), PROJ_SLABS * LANE)


def kernel(x, positions, norm_w, ffn_w_gate, ffn_w_up, ffn_w_down, ev_w_in, ev_w_out, pool_w, pool_scale,
           od_w_in, od_w_out, cmp_pe_k, cmp_w1_k, cmp_w2_k, cmp_pe_v, cmp_w1_v, cmp_w2_v, conv_w):
    batch, seq, d = x.shape
    depth = norm_w.shape[0]
    bf = lambda a: a.astype(_BF16)

    inv_freq = 1.0 / (ROPE_THETA ** (jnp.arange(0, HEAD_DIM, 2, dtype=_F32) / HEAD_DIM))
    ang = positions.astype(_F32)[..., None] * inv_freq
    cos = jnp.concatenate([jnp.cos(ang), jnp.cos(ang)], axis=-1)
    sin = jnp.concatenate([-jnp.sin(ang), jnp.sin(ang)], axis=-1)

    nw = norm_w.reshape(depth, 6, 1, d)
    h = x.reshape(batch * seq, d)

    w_gate, w_up, w_down = bf(ffn_w_gate), bf(ffn_w_up), bf(ffn_w_down)
    w_ev_in, w_ev_out = bf(ev_w_in), bf(ev_w_out)
    w_od_in, w_od_out = bf(jax.vmap(_odd_in_weight)(od_w_in)), bf(od_w_out)

    def ffn(h, layer, which):
        a = _ffn_up(h, nw[layer, 4 * which], w_gate, w_up, (layer, which))
        return _ffn_down(a, w_down, (layer, which), h, nw[layer, 4 * which + 1], HALF_STEP)

    for layer in range(depth):
        i = layer // 2
        h = ffn(h, layer, 0)
        if layer % 2 == 0:
            slabs = _norm_proj(h, nw[layer, 2], w_ev_in, (i,), batch, seq, rope=(cos, sin),
                               n_rope=2 * A_HEADS, n_scaled=A_HEADS, q_scale=HEAD_DIM ** -0.5 * LOG2_E)
            o_a = _dilated_attention(slabs)
            o_b = _multiscale_pool(slabs, bf(pool_w[i]), pool_scale[i].reshape(1, -1))
            h = _mix_out(o_a.reshape(batch * seq, -1), o_b.reshape(batch * seq, -1), w_ev_out, (i,), h,
                         nw[layer, 3])
        else:
            slabs = _norm_proj(h, nw[layer, 2], w_od_in, (i,), batch, seq)
            kv_cmp = _compress(slabs, jnp.stack([cmp_pe_k[i], cmp_pe_v[i]]),
                               bf(jnp.stack([cmp_w1_k[i], cmp_w1_v[i]])),
                               bf(jnp.stack([cmp_w2_k[i], cmp_w2_v[i]])), C_HEADS)
            o_cmp, sel = _cmp_select(slabs, kv_cmp)
            o_c = _nsa_attention(slabs, sel, o_cmp, cos, sin)
            y_d = _short_conv(slabs, conv_w[i], 26)
            h = _mix_out(o_c.reshape(batch * seq, -1), y_d.reshape(batch * seq, -1), w_od_out, (i,), h,
                         nw[layer, 3])
        h = ffn(h, layer, 1)
    return h.reshape(batch, seq, d)
```

```python
import functools

import jax
import jax.numpy as jnp
from jax import lax
from jax.experimental import pallas as pl
from jax.experimental.pallas import tpu as pltpu

HEAD_DIM = 128
ROPE_THETA = 10000.0
NORM_EPS = 1e-6
LOG2_E = 1.4426950408889634
HALF_STEP = 0.5
LANE = 128

A_HEADS = 12
A_DILATIONS = (1, 4, 16)
A_BACK = 128
B_WINDOWS = (2, 4, 8, 16)
B_GROUPS = 4

C_HEADS = 12
C_KV_HEADS = 2
C_GROUP = C_HEADS // C_KV_HEADS
CMP_BLOCK = 32
CMP_STRIDE = 16
SLC_BLOCK = 64
SLC_TOPN = 8
WIN_SIZE = 512
FORCED_SCORE = 1e9
D_WIDTH = 512
PROJ_SLABS = 40
NORM_SUB_TILES = 4

VMEM_LIMIT = 56 * 1024 * 1024

_BF16 = jnp.bfloat16
_F32 = jnp.float32


def _params(*sem):
    return pltpu.CompilerParams(dimension_semantics=sem, vmem_limit_bytes=VMEM_LIMIT)


def _rms(x, w):
    return x * lax.rsqrt(jnp.mean(x * x, axis=-1, keepdims=True) + NORM_EPS) * w


def _dot(a, b):
    return jnp.dot(a, b, preferred_element_type=_F32)


def _dot_t(a, b):
    return lax.dot_general(a, b, (((1,), (1,)), ((), ())), preferred_element_type=_F32)


def _rope(x, cos, sin):
    return x * cos + pltpu.roll(x, HEAD_DIM // 2, 1) * sin


def _shift_rows(x, k, row):
    return jnp.where(row >= k, pltpu.roll(x, k, 0), 0.0)


def _proj_kernel(h_ref, nw_ref, w_ref, *rest, n_rope, n_scaled, q_scale):
    if n_rope:
        cos_ref, sin_ref, o_ref, xn_ref = rest
    else:
        o_ref, xn_ref = rest
    per_step = o_ref.shape[1]
    tn = per_step * LANE

    def use(xn, r, step):
        if step is None:
            cols = pl.ds(pl.multiple_of(pl.program_id(1) * tn, tn), tn)
        else:
            cols = slice(step * tn, (step + 1) * tn)
        acc = _dot(xn, w_ref[:, cols])
        for s in range(per_step):
            x = acc[:, s * LANE:(s + 1) * LANE]
            slab = None if step is None else step * per_step + s
            if slab is not None and slab < n_rope:
                x = _rope(x, cos_ref[0, r, :], sin_ref[0, r, :])
                if slab < n_scaled:
                    x = x * q_scale
            o_ref[0, s, r, :] = x

    steps = w_ref.shape[1] // tn if n_rope else None
    _normed_rows(h_ref, nw_ref, xn_ref, use, static_steps=steps)


def _stacked(block, index_map, lead, **kwargs):
    return pl.BlockSpec((None,) * len(lead) + block, lambda *g: tuple(lead) + index_map(*g), **kwargs)


def _norm_proj(h, nw, w, lead, batch, seq, rope=None, n_rope=0, n_scaled=0, q_scale=1.0, tm=1024, tn=1024):
    t, d = h.shape
    n = w.shape[-1]
    per_b = seq // tm
    table = pl.BlockSpec((1, tm, LANE), lambda i, j: (i // per_b, i % per_b, 0))
    return pl.pallas_call(
        functools.partial(_proj_kernel, n_rope=n_rope, n_scaled=n_scaled, q_scale=q_scale),
        grid=(t // tm, n // tn),
        in_specs=[pl.BlockSpec((tm, d), lambda i, j: (i, 0)),
                  pl.BlockSpec((1, d), lambda i, j: (0, 0)),
                  _stacked((d, n), lambda i, j: (0, 0), lead, pipeline_mode=pl.Buffered(1))]
        + ([table, table] if n_rope else []),
        out_specs=pl.BlockSpec((1, tn // LANE, tm, LANE), lambda i, j: (i // per_b, j, i % per_b, 0)),
        out_shape=jax.ShapeDtypeStruct((batch, n // LANE, seq, LANE), _F32),
        scratch_shapes=[pltpu.VMEM((tm, d), _BF16)],
        compiler_params=_params("arbitrary", "arbitrary"),
        name="norm_proj",
    )(h, nw, w, *(rope if n_rope else ()))


def _normed_rows(h_ref, nw_ref, xn_ref, use, static_steps=None):
    j = pl.program_id(1)

    def first(step):
        sub = h_ref.shape[0] // NORM_SUB_TILES
        for s in range(NORM_SUB_TILES):
            r = slice(s * sub, (s + 1) * sub)
            xn = _rms(h_ref[r, :], nw_ref[...]).astype(_BF16)
            xn_ref[r, :] = xn
            use(xn, r, step)

    if static_steps is None:
        pl.when(j == 0)(lambda: first(None))
        pl.when(j > 0)(lambda: use(xn_ref[...], slice(None), None))
    else:
        pl.when(j == 0)(lambda: first(0))
        for step in range(1, static_steps):
            pl.when(j == step)(lambda step=step: use(xn_ref[...], slice(None), step))


def _ffn_up_kernel(h_ref, nw_ref, wg_ref, wu_ref, o_ref, xn_ref):
    def use(xn, r, step):
        g = _dot(xn, wg_ref[...])
        u = _dot(xn, wu_ref[...])
        o_ref[r, :] = (jax.nn.silu(g) * u).astype(_BF16)

    _normed_rows(h_ref, nw_ref, xn_ref, use)


def _ffn_up(h, nw, wg, wu, lead, tm=1024, tn=512):
    t, d = h.shape
    f = wg.shape[-1]
    return pl.pallas_call(
        _ffn_up_kernel,
        grid=(t // tm, f // tn),
        in_specs=[pl.BlockSpec((tm, d), lambda i, j: (i, 0)),
                  pl.BlockSpec((1, d), lambda i, j: (0, 0)),
                  _stacked((d, tn), lambda i, j: (0, j), lead),
                  _stacked((d, tn), lambda i, j: (0, j), lead)],
        out_specs=pl.BlockSpec((tm, tn), lambda i, j: (i, j)),
        out_shape=jax.ShapeDtypeStruct((t, f), _BF16),
        scratch_shapes=[pltpu.VMEM((tm, d), _BF16)],
        compiler_params=_params("arbitrary", "arbitrary"),
        name="ffn_up",
    )(h, nw, wg, wu)


def _ffn_down_kernel(a_ref, w_ref, h_ref, nw_ref, o_ref, *, scale, n_sub):
    sub = a_ref.shape[0] // n_sub
    for s in range(n_sub):
        r = slice(s * sub, (s + 1) * sub)
        f = _dot(a_ref[r, :], w_ref[...])
        o_ref[r, :] = h_ref[r, :] + scale * _rms(f, nw_ref[...])


def _ffn_down(a, w, lead, h, nw, scale, tm=512, n_sub=2):
    t, f = a.shape
    d = w.shape[-1]
    w_spec = _stacked((f, d), lambda i: (0, 0), lead, pipeline_mode=pl.Buffered(1))
    return pl.pallas_call(
        functools.partial(_ffn_down_kernel, scale=scale, n_sub=n_sub),
        grid=(t // tm,),
        in_specs=[pl.BlockSpec((tm, f), lambda i: (i, 0)),
                  w_spec,
                  pl.BlockSpec((tm, d), lambda i: (i, 0)),
                  pl.BlockSpec((1, d), lambda i: (0, 0))],
        out_specs=pl.BlockSpec((tm, d), lambda i: (i, 0)),
        out_shape=jax.ShapeDtypeStruct((t, d), _F32),
        compiler_params=_params("arbitrary"),
        name="ffn_down",
    )(a, w, h, nw)


def _mix_out_kernel(x1_ref, x2_ref, w_ref, h_ref, nw_ref, o_ref):
    c1 = x1_ref.shape[1]
    sub = x1_ref.shape[0] // 2
    for s in range(2):
        r = slice(s * sub, (s + 1) * sub)
        m = _dot(x1_ref[r, :], w_ref[0:c1, :]) + _dot(x2_ref[r, :], w_ref[c1:, :])
        o_ref[r, :] = h_ref[r, :] + _rms(m, nw_ref[...])


def _mix_out(x1, x2, w, lead, h, nw, tm=512):
    t, c1 = x1.shape
    c2 = x2.shape[1]
    d = w.shape[-1]
    return pl.pallas_call(
        _mix_out_kernel,
        grid=(t // tm,),
        in_specs=[pl.BlockSpec((tm, c1), lambda i: (i, 0)),
                  pl.BlockSpec((tm, c2), lambda i: (i, 0)),
                  _stacked((c1 + c2, d), lambda i: (0, 0), lead),
                  pl.BlockSpec((tm, d), lambda i: (i, 0)),
                  pl.BlockSpec((1, d), lambda i: (0, 0))],
        out_specs=pl.BlockSpec((tm, d), lambda i: (i, 0)),
        out_shape=jax.ShapeDtypeStruct((t, d), _F32),
        compiler_params=_params("arbitrary"),
        name="mix_out",
    )(x1, x2, w, h, nw)


def _rows(start, size, stride):
    return pl.ds(start, size) if stride == 1 else pl.ds(start, size, stride=stride)


def _dilated_kernel(q_ref, k_ref, v_ref, o_ref, s_s, p_s, pv_s, m_s, l_s):
    seq = q_ref.shape[2]
    blk = A_BACK
    chunk = 256

    qi = lax.broadcasted_iota(jnp.int32, (blk, blk), 0)
    kj = lax.broadcasted_iota(jnp.int32, (blk, blk), 1)
    cur_ok = kj <= qi
    both_ok = jnp.concatenate([kj >= qi, cur_ok], axis=1)
    ones = jnp.ones((blk, HEAD_DIM), _BF16)

    blocks = []
    for g, dil in enumerate(A_DILATIONS):
        for res in range(dil):
            prev = None
            for n in range(seq // (blk * dil)):
                rows = _rows(res + n * blk * dil, blk, dil)
                blocks.append((g, rows, prev))
                prev = rows

    def keys_bf16(rows):
        return k_ref[0, 0, rows, :].astype(_BF16)

    def values1_bf16(rows):
        return jnp.concatenate([v_ref[0, 0, rows, :].astype(_BF16), ones], axis=1)

    for b, (g, rows, prev) in enumerate(blocks):
        qb = q_ref[0, 0, rows, :].astype(_BF16)
        if prev is None:
            s_s[b, :, 0:blk] = jnp.where(cur_ok, _dot_t(qb, keys_bf16(rows)), -jnp.inf)
        else:
            k_all = jnp.concatenate([keys_bf16(prev), keys_bf16(rows)], axis=0)
            s_s[b] = jnp.where(both_ok, _dot_t(qb, k_all), -jnp.inf)

    for b, (g, rows, prev) in enumerate(blocks):
        width = blk if prev is None else 2 * blk
        s = s_s[b, :, 0:width]
        m = jnp.max(s, axis=-1, keepdims=True)
        p_s[b, :, 0:width] = jnp.exp2(s - m).astype(_BF16)
        m_s[g, rows, :] = jnp.broadcast_to(m, (blk, HEAD_DIM))

    for b, (g, rows, prev) in enumerate(blocks):
        if prev is None:
            pva = _dot(p_s[b, :, 0:blk], values1_bf16(rows))
        else:
            pva = _dot(p_s[b], jnp.concatenate([values1_bf16(prev), values1_bf16(rows)], axis=0))
        pv_s[g, rows, :] = pva[:, 0:HEAD_DIM]
        l_s[g, rows, :] = pva[:, HEAD_DIM:]

    for c in range(0, seq, chunk):
        r = slice(c, c + chunk)
        m_all = jnp.maximum(jnp.maximum(m_s[0, r, :], m_s[1, r, :]), m_s[2, r, :])
        num = jnp.zeros((chunk, HEAD_DIM), _F32)
        den = jnp.zeros((chunk, HEAD_DIM), _F32)
        for g in range(len(A_DILATIONS)):
            w = jnp.exp2(m_s[g, r, :] - m_all)
            num = num + w * pv_s[g, r, :]
            den = den + w * l_s[g, r, :]
        o_ref[0, r, :] = (num / den).astype(o_ref.dtype)


def _dilated_attention(slabs):
    batch, _, seq, _ = slabs.shape
    head = lambda off: pl.BlockSpec((1, 1, seq, LANE), lambda b, h: (b, off + h, 0, 0))
    nblocks = len(A_DILATIONS) * seq // A_BACK
    return pl.pallas_call(
        _dilated_kernel,
        grid=(batch, A_HEADS),
        in_specs=[head(0), head(A_HEADS), head(2 * A_HEADS)],
        out_specs=pl.BlockSpec((1, seq, LANE), lambda b, h: (b, 0, h)),
        out_shape=jax.ShapeDtypeStruct((batch, seq, A_HEADS * HEAD_DIM), _BF16),
        scratch_shapes=[pltpu.VMEM((nblocks, A_BACK, 2 * A_BACK), _F32),
                        pltpu.VMEM((nblocks, A_BACK, 2 * A_BACK), _BF16),
                        pltpu.VMEM((3, seq, HEAD_DIM), _F32), pltpu.VMEM((3, seq, HEAD_DIM), _F32),
                        pltpu.VMEM((3, seq, HEAD_DIM), _F32)],
        compiler_params=_params("arbitrary", "arbitrary"),
        name="dilated_attention",
    )(slabs, slabs, slabs)


def _pool_kernel(u_ref, w_ref, sc_ref, o_ref):
    seq = u_ref.shape[2]
    row = lax.broadcasted_iota(jnp.int32, (seq, LANE), 0)
    for g, win in enumerate(B_WINDOWS):
        x = u_ref[0, g]
        s = x
        step = 1
        while step < win:
            s = s + _shift_rows(s, step, row)
            step *= 2
        cnt = jnp.minimum(row + 1, win).astype(_F32)
        pooled = s / cnt - x
        mixed = _dot(pooled.astype(_BF16), w_ref[g]) * sc_ref[:, g * LANE:(g + 1) * LANE]
        o_ref[0, :, g * LANE:(g + 1) * LANE] = mixed.astype(o_ref.dtype)


def _multiscale_pool(slabs, pool_w, pool_scale):
    batch, _, seq, _ = slabs.shape
    first = 3 * A_HEADS // B_GROUPS
    return pl.pallas_call(
        _pool_kernel,
        grid=(batch,),
        in_specs=[pl.BlockSpec((1, B_GROUPS, seq, LANE), lambda b: (b, first, 0, 0)),
                  pl.BlockSpec((B_GROUPS, LANE, LANE), lambda b: (0, 0, 0)),
                  pl.BlockSpec((1, B_GROUPS * LANE), lambda b: (0, 0))],
        out_specs=pl.BlockSpec((1, seq, B_GROUPS * LANE), lambda b: (b, 0, 0)),
        out_shape=jax.ShapeDtypeStruct((batch, seq, B_GROUPS * LANE), _BF16),
        compiler_params=_params("arbitrary"),
        name="multiscale_pool",
    )(slabs, pool_w, pool_scale)


def _compress_kernel(x_ref, pe_ref, w1_ref, w2_ref, o_ref):
    nrow = o_ref.shape[3]
    hidden = w1_ref.shape[2]
    a = jnp.zeros((nrow, hidden), _F32)
    b = jnp.zeros((nrow, hidden), _F32)
    for tok in range(CMP_STRIDE):
        x = x_ref[0, 0, pl.ds(tok, nrow, stride=CMP_STRIDE), :]
        lo, hi = tok, CMP_STRIDE + tok
        a = a + _dot((x + pe_ref[0, lo:lo + 1, :]).astype(_BF16), w1_ref[0, lo * HEAD_DIM:(lo + 1) * HEAD_DIM, :])
        b = b + _dot((x + pe_ref[0, hi:hi + 1, :]).astype(_BF16), w1_ref[0, hi * HEAD_DIM:(hi + 1) * HEAD_DIM, :])
    hid = a + pltpu.roll(b, nrow - 1, 0)
    o_ref[0, 0, 0] = _dot(jax.nn.gelu(hid).astype(_BF16), w2_ref[0])


def _compress(slabs, pe, w1, w2, first_slab):
    batch, _, seq, _ = slabs.shape
    nrow = seq // CMP_STRIDE
    return pl.pallas_call(
        _compress_kernel,
        grid=(2, batch, C_KV_HEADS),
        in_specs=[pl.BlockSpec((1, 1, seq, LANE), lambda kv, b, g: (b, first_slab + 2 * kv + g, 0, 0)),
                  pl.BlockSpec((1, CMP_BLOCK, HEAD_DIM), lambda kv, b, g: (kv, 0, 0)),
                  pl.BlockSpec((1, CMP_BLOCK * HEAD_DIM, w1.shape[2]), lambda kv, b, g: (kv, 0, 0)),
                  pl.BlockSpec((1, w2.shape[1], HEAD_DIM), lambda kv, b, g: (kv, 0, 0))],
        out_specs=pl.BlockSpec((1, 1, 1, nrow, HEAD_DIM), lambda kv, b, g: (kv, b, g, 0, 0)),
        out_shape=jax.ShapeDtypeStruct((2, batch, C_KV_HEADS, nrow, HEAD_DIM), _F32),
        compiler_params=_params("arbitrary", "arbitrary", "arbitrary"),
        name="compress",
    )(slabs, pe, w1, w2)


def _cmp_select_kernel(q_ref, kc_ref, vc_ref, o_ref, sel_ref, *, nslc, n_sub):
    tq = q_ref.shape[2] // n_sub
    ncmp = kc_ref.shape[3]
    scale = HEAD_DIM ** -0.5
    n = lax.broadcasted_iota(jnp.int32, (tq, ncmp), 1)

    cj = lax.broadcasted_iota(jnp.int32, (LANE, ncmp), 0) * SLC_BLOCK
    cn = lax.broadcasted_iota(jnp.int32, (LANE, ncmp), 1) * CMP_STRIDE
    cover_t = ((cn < cj + SLC_BLOCK) & (cn + CMP_BLOCK > cj)).astype(_BF16)
    j = lax.broadcasted_iota(jnp.int32, (nslc, tq), 0)
    jf = j.astype(_F32)

    kc = kc_ref[0, 0, 0].astype(_BF16)
    vc = vc_ref[0, 0, 0].astype(_BF16)
    for u in range(n_sub):
        rows = slice(u * tq, (u + 1) * tq)
        t0 = (pl.program_id(2) * n_sub + u) * tq
        t = t0 + lax.broadcasted_iota(jnp.int32, (tq, ncmp), 0)
        cmask = n * CMP_STRIDE + (CMP_BLOCK - 1) <= t
        imp_t = jnp.zeros((LANE, tq), _F32)
        for hh in range(C_GROUP):
            sc = jnp.where(cmask, _dot_t(q_ref[0, hh, rows, :].astype(_BF16), kc) * scale, -jnp.inf)
            m = jnp.max(sc, axis=-1, keepdims=True)
            m = jnp.where(jnp.isfinite(m), m, 0.0)
            p = jnp.exp(sc - m)
            den = jnp.sum(p, axis=-1, keepdims=True)
            p = (p / jnp.maximum(den, 1.0)).astype(_BF16)
            o_ref[0, hh, rows, :] = _dot(p, vc)
            imp_t = imp_t + _dot_t(cover_t, p)

        tt = t0 + lax.broadcasted_iota(jnp.int32, (nslc, tq), 1)
        cur = lax.shift_right_logical(tt, SLC_BLOCK.bit_length() - 1)
        visible = j <= cur
        forced = visible & ((j == 0) | (j >= cur - 1))
        score = jnp.where(forced, FORCED_SCORE, jnp.where(visible, imp_t[0:nslc, :], -FORCED_SCORE))
        sel_t = jnp.zeros((nslc, tq), _F32)
        for _ in range(min(SLC_TOPN, nslc)):
            best = jnp.max(score, axis=0, keepdims=True)
            pick = jnp.min(jnp.where(score == best, jf, float(nslc)), axis=0, keepdims=True)
            hit = jf == pick
            sel_t = jnp.where(hit, 1.0, sel_t)
            score = jnp.where(hit, -jnp.inf, score)
        sel_t = jnp.concatenate([sel_t, jnp.zeros((LANE - nslc, tq), _F32)], axis=0)
        sel_ref[0, 0, rows, :] = sel_t.T.astype(sel_ref.dtype)


def _cmp_select(slabs, kv_cmp, tq=2048, n_sub=8):
    batch, _, seq, _ = slabs.shape
    cmp_spec = lambda which: pl.BlockSpec((1, 1, 1) + kv_cmp.shape[3:], lambda b, g, i: (which, b, g, 0, 0))
    return pl.pallas_call(
        functools.partial(_cmp_select_kernel, nslc=seq // SLC_BLOCK, n_sub=n_sub),
        grid=(batch, C_KV_HEADS, seq // tq),
        in_specs=[pl.BlockSpec((1, C_GROUP, tq, LANE), lambda b, g, i: (b, g, i, 0)),
                  cmp_spec(0), cmp_spec(1)],
        out_specs=[pl.BlockSpec((1, C_GROUP, tq, LANE), lambda b, g, i: (b, g, i, 0)),
                   pl.BlockSpec((1, 1, tq, LANE), lambda b, g, i: (b, g, i, 0))],
        out_shape=[jax.ShapeDtypeStruct((batch, C_HEADS, seq, HEAD_DIM), _F32),
                   jax.ShapeDtypeStruct((batch, C_KV_HEADS, seq, LANE), _BF16)],
        compiler_params=_params("arbitrary", "arbitrary", "arbitrary"),
        name="cmp_select",
    )(slabs, kv_cmp, kv_cmp)


def _group_softmax_pv(q, k, v1, bias_s, s_s, p_s, o_s):
    nk = k.shape[0]
    tq = bias_s.shape[0]
    pieces = [slice(c, min(c + 512, nk)) for c in range(0, nk, 512)]
    s = _dot_t(q, k)
    for hh in range(C_GROUP):
        r = slice(hh * tq, (hh + 1) * tq)
        s_s[r, 0:nk] = s[r] + bias_s[:, 0:nk]
    for hh in range(C_GROUP):
        r = slice(hh * tq, (hh + 1) * tq)
        m = None
        for c in pieces:
            mc = jnp.max(s_s[r, c], axis=-1, keepdims=True)
            m = mc if m is None else jnp.maximum(m, mc)
        for c in pieces:
            p_s[r, c] = jnp.exp2(s_s[r, c] - m).astype(_BF16)
    o = _dot(p_s[:, 0:nk], v1)
    o_s[...] = o[:, 0:HEAD_DIM] / o[:, HEAD_DIM:]


def _nsa_kernel(q_ref, ks_ref, vs_ref, kw_ref, vw_ref, sel_ref, oc_ref, gl_ref, cos_ref, sin_ref, o_ref,
                ksr, vs1, kwr, vw1, expand_s, qr, bias_s, s_s, p_s, bias_w, s_w, p_w, oslc_s, owin_s):
    n_sub = qr.shape[0]
    tq = q_ref.shape[2] // n_sub
    seq = ks_ref.shape[2]
    qi = pl.program_id(2)
    chunk = 256
    shift = SLC_BLOCK.bit_length() - 1

    @pl.when(qi == 0)
    def _():
        def prep(c, _):
            r = pl.ds(pl.multiple_of(c * chunk, chunk), chunk)
            cos, sin = cos_ref[0, r, :], sin_ref[0, r, :]
            ones = jnp.ones((chunk, HEAD_DIM), _BF16)
            ksr[r, :] = _rope(ks_ref[0, 0, r, :], cos, sin).astype(_BF16)
            kwr[r, :] = _rope(kw_ref[0, 0, r, :], cos, sin).astype(_BF16)
            vs1[r, 0:HEAD_DIM] = vs_ref[0, 0, r, :].astype(_BF16)
            vs1[r, HEAD_DIM:] = ones
            vw1[r, 0:HEAD_DIM] = vw_ref[0, 0, r, :].astype(_BF16)
            vw1[r, HEAD_DIM:] = ones
            return 0

        lax.fori_loop(0, seq // chunk, prep, 0)
        blk = lax.broadcasted_iota(jnp.int32, (LANE, seq), 0)
        key = lax.broadcasted_iota(jnp.int32, (LANE, seq), 1)
        expand_s[...] = (blk == lax.shift_right_logical(key, shift)).astype(_BF16)

    base = pl.multiple_of(qi * (n_sub * tq), n_sub * tq)
    sub_rows = [slice(u * tq, (u + 1) * tq) for u in range(n_sub)]
    for u in range(n_sub):
        cos_q = cos_ref[0, pl.ds(base + u * tq, tq), :]
        sin_q = sin_ref[0, pl.ds(base + u * tq, tq), :]
        for hh in range(C_GROUP):
            q_rot = _rope(q_ref[0, hh, sub_rows[u], :], cos_q, sin_q)
            qr[u, hh * tq:(hh + 1) * tq, :] = (q_rot * (HEAD_DIM ** -0.5 * LOG2_E)).astype(_BF16)

    def window_branch(u):
        q0 = base + u * tq
        span = WIN_SIZE + tq
        k0 = pl.multiple_of(jnp.maximum(q0 - WIN_SIZE, 0), tq)
        t = q0 + lax.broadcasted_iota(jnp.int32, (tq, span), 0)
        key = k0 + lax.broadcasted_iota(jnp.int32, (tq, span), 1)
        bias_w[u] = jnp.where((key <= t) & (key > t - WIN_SIZE), 0.0, -jnp.inf)
        _group_softmax_pv(qr[u], kwr[pl.ds(k0, span), :], vw1[pl.ds(k0, span), :],
                          bias_w.at[u], s_w.at[u], p_w.at[u], owin_s.at[u])

    step = n_sub * tq
    for nt in range(1, seq // step + 1):
        @pl.when(qi + 1 == nt)
        def _(nk=nt * step):
            for u in range(n_sub):
                for c in range(0, nk, step):
                    t = base + u * tq + lax.broadcasted_iota(jnp.int32, (tq, step), 0)
                    key = c + lax.broadcasted_iota(jnp.int32, (tq, step), 1)
                    ok = (_dot(sel_ref[0, 0, sub_rows[u], :], expand_s[:, c:c + step]) > 0.5) & (key <= t)
                    bias_s[u, :, c:c + step] = jnp.where(ok, 0.0, -jnp.inf)
                _group_softmax_pv(qr[u], ksr[0:nk, :], vs1[0:nk, :],
                                  bias_s.at[u], s_s.at[u], p_s.at[u], oslc_s.at[u])
                window_branch(u)

    gates = jax.nn.sigmoid(gl_ref[0, 0])
    for u in range(n_sub):
        for hh in range(C_GROUP):
            r = slice(hh * tq, (hh + 1) * tq)
            g_cmp, g_slc, g_win = (gates[sub_rows[u], 3 * hh + c:3 * hh + c + 1] for c in range(3))
            o = g_cmp * oc_ref[0, hh, sub_rows[u], :] + g_slc * oslc_s[u, r, :] + g_win * owin_s[u, r, :]
            o_ref[0, sub_rows[u], hh * HEAD_DIM:(hh + 1) * HEAD_DIM] = o.astype(o_ref.dtype)


def _nsa_attention(slabs, sel, o_cmp, cos, sin, tq=128, n_sub=2):
    batch, _, seq, _ = slabs.shape
    kv = lambda off: pl.BlockSpec((1, 1, seq, LANE), lambda b, g, i: (b, off + g, 0, 0))
    table = pl.BlockSpec((1, seq, LANE), lambda b, g, i: (b, 0, 0))
    rows = C_GROUP * tq
    span = WIN_SIZE + tq
    tile = n_sub * tq
    return pl.pallas_call(
        _nsa_kernel,
        grid=(batch, C_KV_HEADS, seq // tile),
        in_specs=[pl.BlockSpec((1, C_GROUP, tile, LANE), lambda b, g, i: (b, g, i, 0)),
                  kv(16), kv(18), kv(20), kv(22),
                  pl.BlockSpec((1, 1, tile, LANE), lambda b, g, i: (b, g, i, 0)),
                  pl.BlockSpec((1, C_GROUP, tile, LANE), lambda b, g, i: (b, g, i, 0)),
                  pl.BlockSpec((1, 1, tile, LANE), lambda b, g, i: (b, 24 + g, i, 0)),
                  table, table],
        out_specs=pl.BlockSpec((1, tile, C_GROUP * HEAD_DIM), lambda b, g, i: (b, i, g)),
        out_shape=jax.ShapeDtypeStruct((batch, seq, C_HEADS * HEAD_DIM), _BF16),
        scratch_shapes=[pltpu.VMEM((seq, HEAD_DIM), _BF16), pltpu.VMEM((seq, 2 * HEAD_DIM), _BF16)] * 2
        + [pltpu.VMEM((LANE, seq), _BF16), pltpu.VMEM((n_sub, rows, HEAD_DIM), _BF16),
           pltpu.VMEM((n_sub, tq, seq), _F32), pltpu.VMEM((n_sub, rows, seq), _F32),
           pltpu.VMEM((n_sub, rows, seq), _BF16),
           pltpu.VMEM((n_sub, tq, span), _F32), pltpu.VMEM((n_sub, rows, span), _F32),
           pltpu.VMEM((n_sub, rows, span), _BF16)]
        + [pltpu.VMEM((n_sub, rows, HEAD_DIM), _F32)] * 2,
        compiler_params=_params("arbitrary", "arbitrary", "arbitrary"),
        name="nsa_attention",
    )(slabs, slabs, slabs, slabs, slabs, sel, o_cmp, slabs, cos, sin)


def _conv_kernel(u_ref, c_ref, b_ref, w_ref, o_ref):
    seq = u_ref.shape[2]
    row = lax.broadcasted_iota(jnp.int32, (seq, LANE), 0)
    taps = w_ref.shape[0]
    for s in range(u_ref.shape[1]):
        cols = slice(s * LANE, (s + 1) * LANE)
        u = c_ref[0, s] * u_ref[0, s]
        conv = _shift_rows(u, taps - 1, row) * w_ref[0:1, cols]
        for j in range(1, taps):
            shifted = u if j == taps - 1 else _shift_rows(u, taps - 1 - j, row)
            conv = conv + shifted * w_ref[j:j + 1, cols]
        o_ref[0, :, cols] = (b_ref[0, s] * conv).astype(o_ref.dtype)


def _short_conv(slabs, conv_w, first_slab, per_step=2):
    batch, _, seq, _ = slabs.shape
    nd = D_WIDTH // LANE
    assert first_slab % per_step == 0 and nd % per_step == 0
    part = lambda off: pl.BlockSpec((1, per_step, seq, LANE),
                                    lambda b, j: (b, (first_slab + off) // per_step + j, 0, 0))
    return pl.pallas_call(
        _conv_kernel,
        grid=(batch, nd // per_step),
        in_specs=[part(0), part(nd), part(2 * nd),
                  pl.BlockSpec((conv_w.shape[0], per_step * LANE), lambda b, j: (0, j))],
        out_specs=pl.BlockSpec((1, seq, per_step * LANE), lambda b, j: (b, 0, j)),
        out_shape=jax.ShapeDtypeStruct((batch, seq, D_WIDTH), _BF16),
        compiler_params=_params("arbitrary", "arbitrary"),
        name="short_conv",
    )(slabs, slabs, slabs, conv_w)


def _pad_cols(w, total):
    return jnp.pad(w, ((0, 0), (0, total - w.shape[1])))


def _odd_in_weight(w):
    qkv = C_HEADS * HEAD_DIM + 6 * C_KV_HEADS * HEAD_DIM
    ngate = 3 * C_GROUP
    gates = [_pad_cols(w[:, qkv + g * ngate:qkv + (g + 1) * ngate], LANE) for g in range(C_KV_HEADS)]
    rest = w[:, qkv + C_KV_HEADS * ngate:]
    return _pad_cols(jnp.concatenate([w[:, :qkv]] + gates + [rest], axis=1), PROJ_SLABS * LANE)


def kernel(x, positions, norm_w, ffn_w_gate, ffn_w_up, ffn_w_down, ev_w_in, ev_w_out, pool_w, pool_scale,
           od_w_in, od_w_out, cmp_pe_k, cmp_w1_k, cmp_w2_k, cmp_pe_v, cmp_w1_v, cmp_w2_v, conv_w):
    batch, seq, d = x.shape
    depth = norm_w.shape[0]
    bf = lambda a: a.astype(_BF16)

    inv_freq = 1.0 / (ROPE_THETA ** (jnp.arange(0, HEAD_DIM, 2, dtype=_F32) / HEAD_DIM))
    ang = positions.astype(_F32)[..., None] * inv_freq
    cos = jnp.concatenate([jnp.cos(ang), jnp.cos(ang)], axis=-1)
    sin = jnp.concatenate([-jnp.sin(ang), jnp.sin(ang)], axis=-1)

    nw = norm_w.reshape(depth, 6, 1, d)
    h = x.reshape(batch * seq, d)

    w_gate, w_up, w_down = bf(ffn_w_gate), bf(ffn_w_up), bf(ffn_w_down)
    w_ev_in, w_ev_out = bf(ev_w_in), bf(ev_w_out)
    w_od_in, w_od_out = bf(jax.vmap(_odd_in_weight)(od_w_in)), bf(od_w_out)

    def ffn(h, layer, which):
        a = _ffn_up(h, nw[layer, 4 * which], w_gate, w_up, (layer, which))
        return _ffn_down(a, w_down, (layer, which), h, nw[layer, 4 * which + 1], HALF_STEP)

    for layer in range(depth):
        i = layer // 2
        h = ffn(h, layer, 0)
        if layer % 2 == 0:
            slabs = _norm_proj(h, nw[layer, 2], w_ev_in, (i,), batch, seq, rope=(cos, sin),
                               n_rope=2 * A_HEADS, n_scaled=A_HEADS, q_scale=HEAD_DIM ** -0.5 * LOG2_E)
            o_a = _dilated_attention(slabs)
            o_b = _multiscale_pool(slabs, bf(pool_w[i]), pool_scale[i].reshape(1, -1))
            h = _mix_out(o_a.reshape(batch * seq, -1), o_b.reshape(batch * seq, -1), w_ev_out, (i,), h,
                         nw[layer, 3])
        else:
            slabs = _norm_proj(h, nw[layer, 2], w_od_in, (i,), batch, seq)
            kv_cmp = _compress(slabs, jnp.stack([cmp_pe_k[i], cmp_pe_v[i]]),
                               bf(jnp.stack([cmp_w1_k[i], cmp_w1_v[i]])),
                               bf(jnp.stack([cmp_w2_k[i], cmp_w2_v[i]])), C_HEADS)
            o_cmp, sel = _cmp_select(slabs, kv_cmp)
            o_c = _nsa_attention(slabs, sel, o_cmp, cos, sin)
            y_d = _short_conv(slabs, conv_w[i], 26)
            h = _mix_out(o_c.reshape(batch * seq, -1), y_d.reshape(batch * seq, -1), w_od_out, (i,), h,
                         nw[layer, 3])
        h = ffn(h, layer, 1)
    return h.reshape(batch, seq, d)
```
